```python
import math
import jax, jax.numpy as jnp
from jax import lax
import numpy as np

D_MODEL = 2048
BATCH = 2
SEQ = 8192
DEPTH = 2

CONV_DIM = 1024
CONV_WIDTH = 31
N_HEADS = 16
N_KV_HEADS = 4
HEAD_DIM = 64
WINDOW = 128
BLOCK = 128
ATTN_DIM = N_HEADS * HEAD_DIM
KV_DIM = N_KV_HEADS * HEAD_DIM
N_BRANCHES = 2
IN_DIM = 2 * CONV_DIM + ATTN_DIM + 2 * KV_DIM + N_BRANCHES * D_MODEL
NUM_BUCKETS = 32
MAX_DISTANCE = 128
N_KEYS = 128
N_EXPERTS = N_KEYS * N_KEYS
PEER_HEADS = 8
PEER_TOPK = 16
PEER_QDIM = 256
PEER_HALF = PEER_QDIM // 2
PEER_CHUNK = 128
EPS = 1e-6

kernel_name = 'hybrid_conv_swa_peer_block'


def rms_norm(x, g):
    xf = x.astype(jnp.float32)
    y = xf * lax.rsqrt(jnp.mean(xf * xf, axis=-1, keepdims=True) + EPS)
    return (y * g.astype(jnp.float32)).astype(x.dtype)


def t5_bucket_causal(dist):
    max_exact = NUM_BUCKETS // 2
    d = jnp.maximum(dist, 0)
    large = max_exact + (jnp.log(jnp.maximum(d, 1).astype(jnp.float32) / max_exact)
                         / math.log(MAX_DISTANCE / max_exact)
                         * (NUM_BUCKETS - max_exact)).astype(jnp.int32)
    large = jnp.minimum(large, NUM_BUCKETS - 1)
    return jnp.where(d < max_exact, d, large)


def conformer_conv(a, b, dw_w, dw_b, ln_g, ln_b, w_out):
    u = a * jax.nn.sigmoid(b)
    u = jnp.pad(u, ((0, 0), (CONV_WIDTH - 1, 0), (0, 0)))
    u = lax.conv_general_dilated(u, dw_w[:, None, :], window_strides=(1,), padding='VALID',
                                 dimension_numbers=('NWC', 'WIO', 'NWC'),
                                 feature_group_count=CONV_DIM) + dw_b
    uf = u.astype(jnp.float32)
    mu = jnp.mean(uf, axis=-1, keepdims=True)
    var = jnp.mean(jnp.square(uf - mu), axis=-1, keepdims=True)
    uf = (uf - mu) * lax.rsqrt(var + EPS) * ln_g.astype(jnp.float32) + ln_b.astype(jnp.float32)
    u = jax.nn.silu(uf).astype(a.dtype)
    return u @ w_out


def sliding_window_attention(q, k, v, sinks, rel_bias):
    B, S = q.shape[0], q.shape[1]
    nb = S // BLOCK
    G = N_HEADS // N_KV_HEADS
    qb = q.reshape(B, nb, BLOCK, N_KV_HEADS, G, HEAD_DIM)
    kb = k.reshape(B, nb, BLOCK, N_KV_HEADS, HEAD_DIM)
    vb = v.reshape(B, nb, BLOCK, N_KV_HEADS, HEAD_DIM)
    prev = lambda t: jnp.concatenate([jnp.zeros_like(t[:, :1]), t[:, :-1]], axis=1)
    kk = jnp.concatenate([prev(kb), kb], axis=2)
    vv = jnp.concatenate([prev(vb), vb], axis=2)
    scores = jnp.einsum('bnqkgd,bnskd->bnkgqs', qb, kk).astype(jnp.float32) * (HEAD_DIM ** -0.5)
    qi = jnp.arange(BLOCK)[:, None] + BLOCK
    kj = jnp.arange(2 * BLOCK)[None, :]
    dist = qi - kj
    bias = rel_bias.astype(jnp.float32)[t5_bucket_causal(dist)]
    bias = bias.transpose(2, 0, 1).reshape(N_KV_HEADS, G, BLOCK, 2 * BLOCK)
    band = (dist >= 0) & (dist < WINDOW)
    kpos = jnp.arange(nb)[:, None] * BLOCK - BLOCK + kj
    valid = band[None] & (kpos >= 0)[:, None, :]
    scores = jnp.where(valid[None, :, None, None], scores + bias, -jnp.inf)
    sink = jnp.broadcast_to(sinks.astype(jnp.float32).reshape(N_KV_HEADS, G, 1, 1),
                            scores.shape[:-1] + (1,))
    probs = jax.nn.softmax(jnp.concatenate([scores, sink], axis=-1), axis=-1)[..., :-1]
    out = jnp.einsum('bnkgqs,bnskd->bnqkgd', probs.astype(v.dtype), vv)
    return out.reshape(B, S, ATTN_DIM)


def peer(h, w_pq, sub_keys, u_tab, v_tab):
    B, S, D = h.shape
    q = (h @ w_pq).reshape(B, S, PEER_HEADS, 2, PEER_HALF)
    s = jnp.einsum('bshpd,hpnd->bshpn', q, sub_keys).astype(jnp.float32)
    s_top, i_top = lax.top_k(s, PEER_TOPK)
    cand = s_top[..., 0, :, None] + s_top[..., 1, None, :]
    cand_idx = i_top[..., 0, :, None] * N_KEYS + i_top[..., 1, None, :]
    cand = cand.reshape(B, S, PEER_HEADS, PEER_TOPK * PEER_TOPK)
    cand_idx = cand_idx.reshape(B, S, PEER_HEADS, PEER_TOPK * PEER_TOPK)
    best, pos = lax.top_k(cand, PEER_TOPK)
    idx = jnp.take_along_axis(cand_idx, pos, axis=-1)
    gates = jax.nn.softmax(best, axis=-1)
    T = B * S
    n_chunks = T // PEER_CHUNK
    hc = h.reshape(n_chunks, PEER_CHUNK, D)
    ic = idx.reshape(n_chunks, PEER_CHUNK, PEER_HEADS * PEER_TOPK)
    gc = gates.reshape(n_chunks, PEER_CHUNK, PEER_HEADS * PEER_TOPK).astype(h.dtype)

    def chunk(args):
        xt, it, gt = args
        act = jnp.einsum('td,tkd->tk', xt, u_tab[it])
        w = gt * jax.nn.gelu(act, approximate=False)
        return jnp.einsum('tk,tkd->td', w, v_tab[it])

    out = lax.map(chunk, (hc, ic, gc))
    return out.reshape(B, S, D)


def setup_inputs(seed: int = 0) -> dict:
    key = jax.random.key(seed)
    ks = jax.random.split(key, 24)
    nrm = lambda k, shape, scale: jax.random.normal(k, shape, jnp.float32) * scale
    L, D = DEPTH, D_MODEL
    return {
        'x': nrm(ks[0], (BATCH, SEQ, D), 1.0),
        'c': nrm(ks[1], (BATCH, D), 1.0),
        'rel_bias': nrm(ks[2], (NUM_BUCKETS, N_HEADS), 0.2),
        'ada_w': nrm(ks[3], (L, D, 6 * D), 0.5 * D ** -0.5),
        'ada_b': nrm(ks[4], (L, 6 * D), 0.01),
        'norm1_g': 1.0 + nrm(ks[5], (L, D), 0.02),
        'w_in': nrm(ks[6], (L, D, IN_DIM), D ** -0.5),
        'dw_w': nrm(ks[7], (L, CONV_WIDTH, CONV_DIM), CONV_WIDTH ** -0.5),
        'dw_b': nrm(ks[8], (L, CONV_DIM), 0.01),
        'conv_ln_g': 1.0 + nrm(ks[9], (L, CONV_DIM), 0.02),
        'conv_ln_b': nrm(ks[10], (L, CONV_DIM), 0.01),
        'w_conv_out': nrm(ks[11], (L, CONV_DIM, D), CONV_DIM ** -0.5),
        'attn_sinks': nrm(ks[12], (L, N_HEADS), 0.5),
        'w_attn_out': nrm(ks[13], (L, ATTN_DIM, D), ATTN_DIM ** -0.5),
        'w_out': nrm(ks[14], (L, D, D), D ** -0.5),
        'norm2_g': 1.0 + nrm(ks[15], (L, D), 0.02),
        'w_pq': nrm(ks[16], (L, D, PEER_HEADS * PEER_QDIM), D ** -0.5),
        'sub_keys': nrm(ks[17], (L, PEER_HEADS, 2, N_KEYS, PEER_HALF), PEER_HALF ** -0.5),
        'peer_u': nrm(ks[18], (L, N_EXPERTS, D), D ** -0.5),
        'peer_v': nrm(ks[19], (L, N_EXPERTS, D), PEER_HEADS ** -0.5),
        'final_g': 1.0 + nrm(ks[20], (D,), 0.02),
    }


def reference(x, c, rel_bias, ada_w, ada_b, norm1_g, w_in, dw_w, dw_b, conv_ln_g, conv_ln_b,
              w_conv_out, attn_sinks, w_attn_out, w_out, norm2_g, w_pq, sub_keys, peer_u, peer_v,
              final_g):
    split_pts = [CONV_DIM, 2 * CONV_DIM, 2 * CONV_DIM + ATTN_DIM,
                 2 * CONV_DIM + ATTN_DIM + KV_DIM, 2 * CONV_DIM + ATTN_DIM + 2 * KV_DIM]
    cs = jax.nn.silu(c)
    for l in range(DEPTH):
        mod = cs @ ada_w[l] + ada_b[l]
        sh1, sc1, g1, sh2, sc2, g2 = jnp.split(mod, 6, axis=-1)
        h = rms_norm(x, norm1_g[l]) * (1.0 + sc1[:, None]) + sh1[:, None]
        proj = h @ w_in[l]
        a, b, q, k, v, gate_logits = jnp.split(proj, split_pts, axis=-1)
        y_conv = conformer_conv(a, b, dw_w[l], dw_b[l], conv_ln_g[l], conv_ln_b[l], w_conv_out[l])
        y_attn = sliding_window_attention(q, k, v, attn_sinks[l], rel_bias) @ w_attn_out[l]
        gates = jax.nn.sigmoid(gate_logits.astype(jnp.float32)).astype(x.dtype)
        gate_conv, gate_attn = jnp.split(gates, N_BRANCHES, axis=-1)
        mixed = (gate_conv * y_conv + gate_attn * y_attn) @ w_out[l]
        x = x + g1[:, None] * mixed
        h2 = rms_norm(x, norm2_g[l]) * (1.0 + sc2[:, None]) + sh2[:, None]
        x = x + g2[:, None] * peer(h2, w_pq[l], sub_keys[l], peer_u[l], peer_v[l])
    return rms_norm(x, final_g)
```

```python
import functools
import math

import numpy as np
import jax
import jax.numpy as jnp
from jax import lax
from jax.experimental import pallas as pl
from jax.experimental.pallas import tpu as pltpu

F32 = jnp.float32
BF16 = jnp.bfloat16
I32 = jnp.int32

EPS = 1e-6
CONV_WIDTH = 31
HALO = 32
N_HEADS = 16
N_KV_HEADS = 4
HEAD_DIM = 64
BLOCK = 128
NUM_BUCKETS = 32
MAX_DISTANCE = 128
N_KEYS = 128
PEER_HEADS = 8
PEER_TOPK = 16
PEER_HALF = 128
PAIRS = PEER_HEADS * PEER_TOPK
SLAB_ROWS = 16
LANES = 128
N_TABLE_BLOCKS = 2
VMEM_LIMIT = 56 * 1024 * 1024


def _cparams(sem, vmem=VMEM_LIMIT):
    return pltpu.CompilerParams(dimension_semantics=sem, vmem_limit_bytes=vmem)


def _resident(block_shape, index_map):
    return pl.BlockSpec(block_shape, index_map, pipeline_mode=pl.Buffered(1))


def _ada_kernel(c_ref, w_ref, b_ref, o_ref):
    c = c_ref[...]
    cs = c * jax.nn.sigmoid(c)
    o_ref[0] = jnp.dot(cs.astype(BF16), w_ref[0].astype(BF16), preferred_element_type=F32) + b_ref[0]


def _ada(c_pad, ada_w, ada_b):
    L, D, N = ada_w.shape
    tn = 1024
    return pl.pallas_call(
        _ada_kernel,
        grid=(L, N // tn),
        in_specs=[
            pl.BlockSpec((8, D), lambda l, j: (0, 0)),
            pl.BlockSpec((1, D, tn), lambda l, j: (l, 0, j)),
            pl.BlockSpec((1, 1, tn), lambda l, j: (l, 0, j)),
        ],
        out_specs=pl.BlockSpec((1, 8, tn), lambda l, j: (l, 0, j)),
        out_shape=jax.ShapeDtypeStruct((L, 8, N), F32),
        compiler_params=_cparams(("parallel", "parallel")),
        name="ada_mod",
    )(c_pad, ada_w, ada_b.reshape(L, 1, N))


def _bias_kernel(bucket_ref, rb_ref, o_ref):
    h = pl.program_id(0)
    bucket = bucket_ref[...]
    acc = jnp.zeros(bucket.shape, F32)
    for b in range(NUM_BUCKETS):
        acc = jnp.where(bucket == b, rb_ref[b, h], acc)
    o_ref[0] = acc


def _t5_bucket_table():
    qi = np.arange(BLOCK)[:, None] + BLOCK
    kj = np.arange(2 * BLOCK)[None, :]
    d = np.maximum(qi - kj, 0)
    max_exact = NUM_BUCKETS // 2
    ratio = (np.log(np.maximum(d, 1).astype(np.float64) / max_exact)
             / math.log(MAX_DISTANCE / max_exact) * (NUM_BUCKETS - max_exact))
    large = np.minimum(max_exact + ratio.astype(np.int64), NUM_BUCKETS - 1)
    return np.where(d < max_exact, d, large).astype(np.int32)


def _rel_bias_table(rel_bias):
    bucket = jnp.asarray(_t5_bucket_table())
    return pl.pallas_call(
        _bias_kernel,
        grid=(N_HEADS,),
        in_specs=[
            pl.BlockSpec((BLOCK, 2 * BLOCK), lambda h: (0, 0)),
            pl.BlockSpec(memory_space=pltpu.SMEM),
        ],
        out_specs=pl.BlockSpec((1, BLOCK, 2 * BLOCK), lambda h: (h, 0, 0)),
        out_shape=jax.ShapeDtypeStruct((N_HEADS, BLOCK, 2 * BLOCK), F32),
        compiler_params=_cparams(("arbitrary",)),
        name="rel_bias_table",
    )(bucket, rel_bias)


def _norm_mod(x, g, sc, sh):
    ms = jnp.mean(x * x, axis=-1, keepdims=True)
    y = x * lax.rsqrt(ms + EPS)
    return (y * g) * (1.0 + sc) + sh


def _inproj_kernel(x_ref, g_ref, sc_ref, sh_ref, w_ref, o_ref, h_scr):
    @pl.when(pl.program_id(1) == 0)
    def _():
        h_scr[...] = _norm_mod(x_ref[...], g_ref[...], sc_ref[0], sh_ref[0]).astype(BF16)

    o_ref[...] = jnp.dot(h_scr[...], w_ref[...], preferred_element_type=F32)


def _inproj(x, g, sc, sh, w, seq):
    T, D = x.shape
    N = w.shape[1]
    tm, tn = 512, 512
    per_seq = seq // tm
    return pl.pallas_call(
        _inproj_kernel,
        grid=(T // tm, N // tn),
        in_specs=[
            pl.BlockSpec((tm, D), lambda i, j: (i, 0)),
            pl.BlockSpec((1, D), lambda i, j: (0, 0)),
            pl.BlockSpec((1, 1, D), lambda i, j: (i // per_seq, 0, 0)),
            pl.BlockSpec((1, 1, D), lambda i, j: (i // per_seq, 0, 0)),
            pl.BlockSpec((D, tn), lambda i, j: (0, j)),
        ],
        out_specs=pl.BlockSpec((tm, tn), lambda i, j: (i, j)),
        out_shape=jax.ShapeDtypeStruct((T, N), F32),
        scratch_shapes=[pltpu.VMEM((tm, D), BF16)],
        compiler_params=_cparams(("parallel", "arbitrary")),
        name="in_proj",
    )(x, g, sc, sh, w)


def _conv_kernel(per_seq, a_ref, b_ref, ha_ref, hb_ref, dw_ref, db_ref, lg_ref, lb_ref, o_ref, ext):
    ts = a_ref.shape[0]
    first = (pl.program_id(0) % per_seq) == 0
    hu = ha_ref[...] * jax.nn.sigmoid(hb_ref[...])
    ext[0:HALO, :] = jnp.where(first, 0.0, hu)
    ext[HALO:, :] = a_ref[...] * jax.nn.sigmoid(b_ref[...])
    acc = jnp.zeros(a_ref.shape, F32)
    base = HALO - (CONV_WIDTH - 1)
    for w in range(CONV_WIDTH):
        acc = acc + ext[base + w:base + w + ts, :] * dw_ref[w:w + 1, :]
    acc = acc + db_ref[...]
    mu = jnp.mean(acc, axis=-1, keepdims=True)
    cen = acc - mu
    var = jnp.mean(cen * cen, axis=-1, keepdims=True)
    y = cen * lax.rsqrt(var + EPS) * lg_ref[...] + lb_ref[...]
    o_ref[...] = (y * jax.nn.sigmoid(y)).astype(BF16)


def _conv(proj, dw_w, dw_b, ln_g, ln_b, seq, a_col, b_col):
    T = proj.shape[0]
    C = dw_w.shape[1]
    ts = 256
    per_seq = seq // ts
    rb = ts // HALO
    halo_map = lambda col: (lambda i: (jnp.maximum(i * rb - 1, 0), col))
    return pl.pallas_call(
        functools.partial(_conv_kernel, per_seq),
        grid=(T // ts,),
        in_specs=[
            pl.BlockSpec((ts, C), lambda i: (i, a_col)),
            pl.BlockSpec((ts, C), lambda i: (i, b_col)),
            pl.BlockSpec((HALO, C), halo_map(a_col)),
            pl.BlockSpec((HALO, C), halo_map(b_col)),
            pl.BlockSpec((CONV_WIDTH, C), lambda i: (0, 0)),
            pl.BlockSpec((1, C), lambda i: (0, 0)),
            pl.BlockSpec((1, C), lambda i: (0, 0)),
            pl.BlockSpec((1, C), lambda i: (0, 0)),
        ],
        out_specs=pl.BlockSpec((ts, C), lambda i: (i, 0)),
        out_shape=jax.ShapeDtypeStruct((T, C), BF16),
        scratch_shapes=[pltpu.VMEM((HALO + ts, C), F32)],
        compiler_params=_cparams(("parallel",)),
        name="conv_branch",
    )(proj, proj, proj, proj, dw_w, dw_b, ln_g, ln_b)


def _attn_kernel(nb, q_ref, kc_ref, kp_ref, vc_ref, vp_ref, bias_ref, sink_ref, o_ref):
    first = (pl.program_id(0) % nb) == 0
    kk = jnp.concatenate([kp_ref[...], kc_ref[...]], axis=0).astype(BF16)
    vv = jnp.concatenate([vp_ref[...], vc_ref[...]], axis=0).astype(BF16)
    row = lax.broadcasted_iota(I32, (BLOCK, 2 * BLOCK), 0)
    col = lax.broadcasted_iota(I32, (BLOCK, 2 * BLOCK), 1)
    dist = row + BLOCK - col
    valid = (dist >= 0) & (dist < BLOCK) & (jnp.logical_not(first) | (col >= BLOCK))
    group = N_HEADS // N_KV_HEADS
    for kvh in range(N_KV_HEADS):
        k_h = kk[:, kvh * HEAD_DIM:(kvh + 1) * HEAD_DIM]
        v_h = vv[:, kvh * HEAD_DIM:(kvh + 1) * HEAD_DIM]
        qg = q_ref[:, kvh * group * HEAD_DIM:(kvh + 1) * group * HEAD_DIM].astype(BF16)
        outs = []
        for g in range(group):
            h = kvh * group + g
            q_h = qg[:, g * HEAD_DIM:(g + 1) * HEAD_DIM]
            s = lax.dot_general(q_h, k_h, (((1,), (1,)), ((), ())), preferred_element_type=F32)
            s = s * (HEAD_DIM ** -0.5) + bias_ref[h]
            s = jnp.where(valid, s, -jnp.inf)
            sink = sink_ref[h]
            m = jnp.maximum(jnp.max(s, axis=-1, keepdims=True), sink)
            e = jnp.exp(s - m)
            denom = jnp.sum(e, axis=-1, keepdims=True) + jnp.exp(sink - m)
            p = e / denom
            outs.append(jnp.dot(p.astype(BF16), v_h, preferred_element_type=F32))
        o_ref[:, kvh * group * HEAD_DIM:(kvh + 1) * group * HEAD_DIM] = (
            jnp.concatenate(outs, axis=1).astype(BF16))


def _attn(proj, bias_tab, sinks, seq, q_col, k_col, v_col):
    T = proj.shape[0]
    nb = seq // BLOCK
    attn_dim = N_HEADS * HEAD_DIM
    kv_dim = N_KV_HEADS * HEAD_DIM
    prev = lambda col: (lambda i: (jnp.where(i % nb == 0, i, i - 1), col))
    return pl.pallas_call(
        functools.partial(_attn_kernel, nb),
        grid=(T // BLOCK,),
        in_specs=[
            pl.BlockSpec((BLOCK, attn_dim), lambda i: (i, q_col)),
            pl.BlockSpec((BLOCK, kv_dim), lambda i: (i, k_col)),
            pl.BlockSpec((BLOCK, kv_dim), prev(k_col)),
            pl.BlockSpec((BLOCK, kv_dim), lambda i: (i, v_col)),
            pl.BlockSpec((BLOCK, kv_dim), prev(v_col)),
            pl.BlockSpec((N_HEADS, BLOCK, 2 * BLOCK), lambda i: (0, 0, 0)),
            pl.BlockSpec(memory_space=pltpu.SMEM),
        ],
        out_specs=pl.BlockSpec((BLOCK, attn_dim), lambda i: (i, 0)),
        out_shape=jax.ShapeDtypeStruct((T, attn_dim), BF16),
        compiler_params=_cparams(("parallel",)),
        name="swa_attention",
    )(proj, proj, proj, proj, proj, bias_tab, sinks)


def _mix_kernel(ca_ref, at_ref, gc_ref, ga_ref, x_ref, g1_ref, wc_ref, wa_ref, wo_ref, o_ref):
    yc = jnp.dot(ca_ref[...], wc_ref[...], preferred_element_type=F32)
    ya = jnp.dot(at_ref[...], wa_ref[...], preferred_element_type=F32)
    mixed = jax.nn.sigmoid(gc_ref[...]) * yc + jax.nn.sigmoid(ga_ref[...]) * ya
    y = jnp.dot(mixed.astype(BF16), wo_ref[...], preferred_element_type=F32)
    o_ref[...] = x_ref[...] + g1_ref[0] * y


def _mix(cact, attn, proj, x, g1, wc, wa, wo, seq, gc_col, ga_col):
    T, D = x.shape
    C = cact.shape[1]
    A = attn.shape[1]
    tm = 256
    per_seq = seq // tm
    return pl.pallas_call(
        _mix_kernel,
        grid=(T // tm,),
        in_specs=[
            pl.BlockSpec((tm, C), lambda i: (i, 0)),
            pl.BlockSpec((tm, A), lambda i: (i, 0)),
            pl.BlockSpec((tm, D), lambda i: (i, gc_col)),
            pl.BlockSpec((tm, D), lambda i: (i, ga_col)),
            pl.BlockSpec((tm, D), lambda i: (i, 0)),
            pl.BlockSpec((1, 1, D), lambda i: (i // per_seq, 0, 0)),
            _resident((C, D), lambda i: (0, 0)),
            _resident((A, D), lambda i: (0, 0)),
            _resident((D, D), lambda i: (0, 0)),
        ],
        out_specs=pl.BlockSpec((tm, D), lambda i: (i, 0)),
        out_shape=jax.ShapeDtypeStruct((T, D), F32),
        compiler_params=_cparams(("parallel",)),
        name="merge_out_proj",
    )(cact, attn, proj, proj, x, g1, wc, wa, wo)


def _peer_q_kernel(x_ref, g_ref, sc_ref, sh_ref, w_ref, k_ref, h_ref, st_ref):
    hb = _norm_mod(x_ref[...], g_ref[...], sc_ref[0], sh_ref[0]).astype(BF16)
    h_ref[...] = hb
    qb = jnp.dot(hb, w_ref[...], preferred_element_type=F32).astype(BF16)
    for hp in range(2 * PEER_HEADS):
        q_hp = qb[:, hp * PEER_HALF:(hp + 1) * PEER_HALF]
        st_ref[hp * N_KEYS:(hp + 1) * N_KEYS, :] = lax.dot_general(
            k_ref[hp], q_hp, (((1,), (1,)), ((), ())), preferred_element_type=F32)


def _peer_q(x, g, sc, sh, w_pq, keys, seq):
    T, D = x.shape
    Q = w_pq.shape[1]
    tm = 256
    per_seq = seq // tm
    n_rows = 2 * PEER_HEADS * N_KEYS
    return pl.pallas_call(
        _peer_q_kernel,
        grid=(T // tm,),
        in_specs=[
            pl.BlockSpec((tm, D), lambda i: (i, 0)),
            pl.BlockSpec((1, D), lambda i: (0, 0)),
            pl.BlockSpec((1, 1, D), lambda i: (i // per_seq, 0, 0)),
            pl.BlockSpec((1, 1, D), lambda i: (i // per_seq, 0, 0)),
            _resident((D, Q), lambda i: (0, 0)),
            _resident((2 * PEER_HEADS, N_KEYS, PEER_HALF), lambda i: (0, 0, 0)),
        ],
        out_specs=[
            pl.BlockSpec((tm, D), lambda i: (i, 0)),
            pl.BlockSpec((n_rows, tm), lambda i: (0, i)),
        ],
        out_shape=[
            jax.ShapeDtypeStruct((T, D), BF16),
            jax.ShapeDtypeStruct((n_rows, T), F32),
        ],
        compiler_params=_cparams(("parallel",)),
        name="peer_query_scores",
    )(x, g, sc, sh, w_pq, keys)


def _top_rows(x, payload, k):
    n_rows = x.shape[0]
    rows = lax.broadcasted_iota(I32, x.shape, 0)
    vals, pays = [], []
    for _ in range(k):
        m = jnp.max(x, axis=0, keepdims=True)
        first = jnp.min(jnp.where(x == m, rows, n_rows), axis=0, keepdims=True)
        onehot = rows == first
        if payload is None:
            pays.append(first)
        else:
            pays.append(jnp.sum(jnp.where(onehot, payload, 0), axis=0, keepdims=True))
        vals.append(m)
        x = jnp.where(onehot, -jnp.inf, x)
    return jnp.concatenate(vals, axis=0), jnp.concatenate(pays, axis=0)


def _route_kernel(block_rows, st_ref, idx_ref, idxl_ref, gate_ref, idx_scr, gate_scr):
    def head(h, carry):
        base = pl.multiple_of(h * (2 * N_KEYS), 2 * N_KEYS)
        v0, i0 = _top_rows(st_ref[pl.ds(base, N_KEYS), :], None, PEER_TOPK)
        v1, i1 = _top_rows(st_ref[pl.ds(base + N_KEYS, N_KEYS), :], None, PEER_TOPK)
        cand = jnp.concatenate([v0[k:k + 1] + v1 for k in range(PEER_TOPK)], axis=0)
        cidx = jnp.concatenate([i0[k:k + 1] * N_KEYS + i1 for k in range(PEER_TOPK)], axis=0)
        best, eid = _top_rows(cand, cidx, PEER_TOPK)
        e = jnp.exp(best - jnp.max(best, axis=0, keepdims=True))
        gates = e / jnp.sum(e, axis=0, keepdims=True)
        off = pl.multiple_of(h * PEER_TOPK, PEER_TOPK)
        idx_scr[pl.ds(off, PEER_TOPK), :] = eid.astype(F32)
        gate_scr[pl.ds(off, PEER_TOPK), :] = gates
        return carry

    lax.fori_loop(0, PEER_HEADS, head, 0)
    idx = idx_scr[...].T.astype(I32)
    idx_ref[...] = idx
    gate_ref[...] = gate_scr[...].T
    for b in range(N_TABLE_BLOCKS):
        inb = (idx >= b * block_rows) & (idx < (b + 1) * block_rows)
        idxl_ref[b] = jnp.where(inb, idx - b * block_rows, 0)


def _route(st, block_rows):
    n_rows, T = st.shape
    tr = 128
    return pl.pallas_call(
        functools.partial(_route_kernel, block_rows),
        grid=(T // tr,),
        in_specs=[pl.BlockSpec((n_rows, tr), lambda i: (0, i))],
        out_specs=[
            pl.BlockSpec((tr, PAIRS), lambda i: (i, 0)),
            pl.BlockSpec((N_TABLE_BLOCKS, tr, PAIRS), lambda i: (0, i, 0)),
            pl.BlockSpec((tr, PAIRS), lambda i: (i, 0)),
        ],
        out_shape=[
            jax.ShapeDtypeStruct((T, PAIRS), I32),
            jax.ShapeDtypeStruct((N_TABLE_BLOCKS, T, PAIRS), I32),
            jax.ShapeDtypeStruct((T, PAIRS), F32),
        ],
        scratch_shapes=[pltpu.VMEM((PAIRS, tr), F32), pltpu.VMEM((PAIRS, tr), F32)],
        compiler_params=_cparams(("parallel",)),
        name="peer_route",
    )(st)


def _diag_mask():
    r = lax.broadcasted_iota(I32, (SLAB_ROWS, PAIRS * SLAB_ROWS), 0)
    c = lax.broadcasted_iota(I32, (SLAB_ROWS, PAIRS * SLAB_ROWS), 1)
    return r == (c % SLAB_ROWS)


def _peer_u_kernel(idx_ref, h_ref, u_ref, o_ref):
    tm = h_ref.shape[0]
    eye = _diag_mask()
    n_chunks = PAIRS * SLAB_ROWS // LANES
    per_chunk = LANES // SLAB_ROWS
    lane = lax.broadcasted_iota(I32, (n_chunks, LANES), 1)
    rowi = lax.broadcasted_iota(I32, (n_chunks, LANES), 0)

    def tok(t, carry):
        slabs = []
        for v in range(n_chunks):
            for g in range(per_chunk):
                slabs.append(u_ref[idx_ref[0, t, SLAB_ROWS * g + v]])
        usel = jnp.concatenate(slabs, axis=0)
        r = lax.dot_general(h_ref[t], usel, (((1,), (1,)), ((), ())), preferred_element_type=F32)
        cs = jnp.sum(jnp.where(eye, r, 0.0), axis=0, keepdims=True)
        y = jnp.concatenate([cs[:, LANES * v:LANES * (v + 1)] for v in range(n_chunks)], axis=0)
        for d in (1, 2, 4, 8):
            up = pltpu.roll(y, LANES - d, 1)
            dn = pltpu.roll(y, d, 1)
            y = y + jnp.where((lane & d) != 0, dn, up)
        act = jnp.sum(jnp.where((lane % SLAB_ROWS) == rowi, y, 0.0), axis=0, keepdims=True)
        o_ref[0, pl.ds(t, 1), :] = act
        return carry

    lax.fori_loop(0, tm, tok, 0)


def _peer_u(idxl, h_slab, u_tab):
    T = h_slab.shape[0]
    block_rows = u_tab.shape[0] // N_TABLE_BLOCKS
    tm = 128
    return pl.pallas_call(
        _peer_u_kernel,
        grid=(N_TABLE_BLOCKS, T // tm),
        in_specs=[
            pl.BlockSpec((1, tm, PAIRS), lambda b, i: (b, i, 0), memory_space=pltpu.SMEM),
            pl.BlockSpec((tm, SLAB_ROWS, LANES), lambda b, i: (i, 0, 0)),
            _resident((block_rows, SLAB_ROWS, LANES), lambda b, i: (b, 0, 0)),
        ],
        out_specs=pl.BlockSpec((1, tm, PAIRS), lambda b, i: (b, i, 0)),
        out_shape=jax.ShapeDtypeStruct((N_TABLE_BLOCKS, T, PAIRS), F32),
        compiler_params=_cparams(("arbitrary", "arbitrary")),
        name="peer_expert_u",
    )(idxl, h_slab, u_tab)


def _peer_v_kernel(block_rows, idxl_ref, idx_ref, gate_ref, act_ref, ex_ref, v_ref, o_ref, wexp):
    tm = idx_ref.shape[0]
    b = pl.program_id(0)
    idx = idx_ref[...]
    act = act_ref[0]
    for k in range(1, N_TABLE_BLOCKS):
        act = jnp.where(idx >= k * block_rows, act_ref[k], act)
    gelu = 0.5 * act * (1.0 + lax.erf(act * np.float32(math.sqrt(0.5))))
    inb = (idx >= b * block_rows) & (idx < (b + 1) * block_rows)
    w = jnp.where(inb, gate_ref[...] * gelu, 0.0)
    wexp[...] = jnp.dot(w.astype(BF16), ex_ref[...], preferred_element_type=F32)
    eye = _diag_mask()

    def tok(t, carry):
        vsel = jnp.concatenate([v_ref[idxl_ref[0, t, p]] for p in range(PAIRS)], axis=0)
        wrow = jnp.broadcast_to(wexp[pl.ds(t, 1), :], (SLAB_ROWS, PAIRS * SLAB_ROWS))
        wbd = jnp.where(eye, wrow, 0.0).astype(BF16)
        o_ref[0, t] = jnp.dot(wbd, vsel, preferred_element_type=F32)
        return carry

    lax.fori_loop(0, tm, tok, 0)


def _peer_v(idxl, idx, gate, act, expand, v_tab):
    T = idx.shape[0]
    block_rows = v_tab.shape[0] // N_TABLE_BLOCKS
    tm = 128
    return pl.pallas_call(
        functools.partial(_peer_v_kernel, block_rows),
        grid=(N_TABLE_BLOCKS, T // tm),
        in_specs=[
            pl.BlockSpec((1, tm, PAIRS), lambda b, i: (b, i, 0), memory_space=pltpu.SMEM),
            pl.BlockSpec((tm, PAIRS), lambda b, i: (i, 0)),
            pl.BlockSpec((tm, PAIRS), lambda b, i: (i, 0)),
            pl.BlockSpec((N_TABLE_BLOCKS, tm, PAIRS), lambda b, i: (0, i, 0)),
            _resident((PAIRS, PAIRS * SLAB_ROWS), lambda b, i: (0, 0)),
            _resident((block_rows, SLAB_ROWS, LANES), lambda b, i: (b, 0, 0)),
        ],
        out_specs=pl.BlockSpec((1, tm, SLAB_ROWS, LANES), lambda b, i: (b, i, 0, 0)),
        out_shape=jax.ShapeDtypeStruct((N_TABLE_BLOCKS, T, SLAB_ROWS, LANES), F32),
        scratch_shapes=[pltpu.VMEM((tm, PAIRS * SLAB_ROWS), F32)],
        compiler_params=_cparams(("arbitrary", "arbitrary")),
        name="peer_expert_v",
    )(idxl, idx, gate, act, expand, v_tab)


def _resid_kernel(final, x_ref, o_ref, g2_ref, fg_ref, y_ref):
    acc = o_ref[0]
    for k in range(1, N_TABLE_BLOCKS):
        acc = acc + o_ref[k]
    x = x_ref[...] + g2_ref[0] * acc
    if final:
        ms = jnp.mean(x * x, axis=-1, keepdims=True)
        x = (x * lax.rsqrt(ms + EPS)) * fg_ref[...]
    y_ref[...] = x


def _resid(x, peer_out, g2, final_g, seq, final):
    T, D = x.shape
    tm = 512
    per_seq = seq // tm
    return pl.pallas_call(
        functools.partial(_resid_kernel, final),
        grid=(T // tm,),
        in_specs=[
            pl.BlockSpec((tm, D), lambda i: (i, 0)),
            pl.BlockSpec((N_TABLE_BLOCKS, tm, D), lambda i: (0, i, 0)),
            pl.BlockSpec((1, 1, D), lambda i: (i // per_seq, 0, 0)),
            pl.BlockSpec((1, D), lambda i: (0, 0)),
        ],
        out_specs=pl.BlockSpec((tm, D), lambda i: (i, 0)),
        out_shape=jax.ShapeDtypeStruct((T, D), F32),
        compiler_params=_cparams(("parallel",)),
        name="peer_residual",
    )(x, peer_out, g2, final_g)


def _pair_expand_matrix():
    p = np.arange(PAIRS)[:, None]
    c = np.arange(PAIRS * SLAB_ROWS)[None, :]
    return jnp.asarray(p == c // SLAB_ROWS, dtype=BF16)


def kernel(x, c, rel_bias, ada_w, ada_b, norm1_g, w_in, dw_w, dw_b, conv_ln_g, conv_ln_b, w_conv_out,
           attn_sinks, w_attn_out, w_out, norm2_g, w_pq, sub_keys, peer_u, peer_v, final_g):
    B, S, D = x.shape
    L = ada_w.shape[0]
    T = B * S
    C = dw_w.shape[2]
    A = N_HEADS * HEAD_DIM
    KV = N_KV_HEADS * HEAD_DIM
    E = peer_u.shape[1]
    assert S % 512 == 0 and D == SLAB_ROWS * LANES and E % N_TABLE_BLOCKS == 0

    o_a, o_b, o_q, o_k, o_v, o_g = 0, C, 2 * C, 2 * C + A, 2 * C + A + KV, 2 * C + A + 2 * KV
    order = np.concatenate([np.arange(o_a, o_a + 2 * C), np.arange(o_g, o_g + 2 * D),
                            np.arange(o_q, o_q + A + 2 * KV)])
    n_a, n_b, n_gc, n_ga, n_q, n_k, n_v = 0, C, 2 * C, 2 * C + D, 2 * C + 2 * D, 2 * C + 2 * D + A, 2 * C + 2 * D + A + KV

    c_pad = jnp.pad(c, ((0, 8 - B), (0, 0)))
    mod = _ada(c_pad, ada_w, ada_b)
    bias_tab = _rel_bias_table(rel_bias)
    expand = _pair_expand_matrix()
    final_g2 = final_g.reshape(1, D)

    xt = x.reshape(T, D)
    for l in range(L):
        m = mod[l, :B].reshape(B, 6, 1, D)
        sh1, sc1, g1, sh2, sc2, g2 = (m[:, k] for k in range(6))

        w_in_l = w_in[l][:, order].astype(BF16)
        proj = _inproj(xt, norm1_g[l].reshape(1, D), sc1, sh1, w_in_l, S)
        cact = _conv(proj, dw_w[l], dw_b[l].reshape(1, C), conv_ln_g[l].reshape(1, C),
                     conv_ln_b[l].reshape(1, C), S, n_a // C, n_b // C)
        attn = _attn(proj, bias_tab, attn_sinks[l], S, n_q // A, n_k // KV, n_v // KV)
        xt = _mix(cact, attn, proj, xt, g1, w_conv_out[l].astype(BF16), w_attn_out[l].astype(BF16),
                  w_out[l].astype(BF16), S, n_gc // D, n_ga // D)

        keys = sub_keys[l].reshape(2 * PEER_HEADS, N_KEYS, PEER_HALF).astype(BF16)
        h2, st = _peer_q(xt, norm2_g[l].reshape(1, D), sc2, sh2, w_pq[l].astype(BF16), keys, S)
        idx, idxl, gate = _route(st, E // N_TABLE_BLOCKS)
        u_tab = peer_u[l].astype(BF16).reshape(E, SLAB_ROWS, LANES)
        v_tab = peer_v[l].astype(BF16).reshape(E, SLAB_ROWS, LANES)
        act = _peer_u(idxl, h2.reshape(T, SLAB_ROWS, LANES), u_tab)
        pout = _peer_v(idxl, idx, gate, act, expand, v_tab)
        xt = _resid(xt, pout.reshape(N_TABLE_BLOCKS, T, D), g2, final_g2, S, l == L - 1)
    return xt.reshape(B, S, D)
```

```python
import functools
import math

import numpy as np
import jax
import jax.numpy as jnp
from jax import lax
from jax.experimental import pallas as pl
from jax.experimental.pallas import tpu as pltpu

F32 = jnp.float32
BF16 = jnp.bfloat16
I32 = jnp.int32

EPS = 1e-6
CONV_WIDTH = 31
HALO = 32
N_HEADS = 16
N_KV_HEADS = 4
HEAD_DIM = 64
BLOCK = 128
NUM_BUCKETS = 32
MAX_DISTANCE = 128
N_KEYS = 128
PEER_HEADS = 8
PEER_TOPK = 16
PEER_HALF = 128
PAIRS = PEER_HEADS * PEER_TOPK
SLAB_ROWS = 16
LANES = 128
N_TABLE_BLOCKS = 2
TOKEN_UNROLL = 4
VMEM_LIMIT = 56 * 1024 * 1024


def _cparams(sem, vmem=VMEM_LIMIT):
    return pltpu.CompilerParams(dimension_semantics=sem, vmem_limit_bytes=vmem)


def _resident(block_shape, index_map):
    return pl.BlockSpec(block_shape, index_map, pipeline_mode=pl.Buffered(1))


def _ada_kernel(c_ref, w_ref, b_ref, o_ref):
    c = c_ref[...]
    cs = c * jax.nn.sigmoid(c)
    o_ref[0] = jnp.dot(cs.astype(BF16), w_ref[0].astype(BF16), preferred_element_type=F32) + b_ref[0]


def _ada(c_pad, ada_w, ada_b):
    L, D, N = ada_w.shape
    tn = 1024
    return pl.pallas_call(
        _ada_kernel,
        grid=(L, N // tn),
        in_specs=[
            pl.BlockSpec((8, D), lambda l, j: (0, 0)),
            pl.BlockSpec((1, D, tn), lambda l, j: (l, 0, j)),
            pl.BlockSpec((1, 1, tn), lambda l, j: (l, 0, j)),
        ],
        out_specs=pl.BlockSpec((1, 8, tn), lambda l, j: (l, 0, j)),
        out_shape=jax.ShapeDtypeStruct((L, 8, N), F32),
        compiler_params=_cparams(("parallel", "parallel")),
        name="ada_mod",
    )(c_pad, ada_w, ada_b.reshape(L, 1, N))


def _bias_kernel(bucket_ref, rb_ref, o_ref):
    h = pl.program_id(0)
    bucket = bucket_ref[...]
    acc = jnp.zeros(bucket.shape, F32)
    for b in range(NUM_BUCKETS):
        acc = jnp.where(bucket == b, rb_ref[b, h], acc)
    o_ref[0] = acc


def _t5_bucket_table():
    qi = np.arange(BLOCK)[:, None] + BLOCK
    kj = np.arange(2 * BLOCK)[None, :]
    d = np.maximum(qi - kj, 0)
    max_exact = NUM_BUCKETS // 2
    ratio = (np.log(np.maximum(d, 1).astype(np.float64) / max_exact)
             / math.log(MAX_DISTANCE / max_exact) * (NUM_BUCKETS - max_exact))
    large = np.minimum(max_exact + ratio.astype(np.int64), NUM_BUCKETS - 1)
    return np.where(d < max_exact, d, large).astype(np.int32)


def _rel_bias_table(rel_bias):
    bucket = jnp.asarray(_t5_bucket_table())
    return pl.pallas_call(
        _bias_kernel,
        grid=(N_HEADS,),
        in_specs=[
            pl.BlockSpec((BLOCK, 2 * BLOCK), lambda h: (0, 0)),
            pl.BlockSpec(memory_space=pltpu.SMEM),
        ],
        out_specs=pl.BlockSpec((1, BLOCK, 2 * BLOCK), lambda h: (h, 0, 0)),
        out_shape=jax.ShapeDtypeStruct((N_HEADS, BLOCK, 2 * BLOCK), F32),
        compiler_params=_cparams(("arbitrary",)),
        name="rel_bias_table",
    )(bucket, rel_bias)


def _norm_mod(x, g, sc, sh):
    ms = jnp.mean(x * x, axis=-1, keepdims=True)
    y = x * lax.rsqrt(ms + EPS)
    return (y * g) * (1.0 + sc) + sh


def _inproj_kernel(x_ref, g_ref, sc_ref, sh_ref, w_ref, o_ref, h_scr):
    @pl.when(pl.program_id(1) == 0)
    def _():
        h_scr[...] = _norm_mod(x_ref[...], g_ref[...], sc_ref[0], sh_ref[0]).astype(BF16)

    o_ref[...] = jnp.dot(h_scr[...], w_ref[...], preferred_element_type=F32)


def _inproj(x, g, sc, sh, w, seq):
    T, D = x.shape
    N = w.shape[1]
    tm, tn = 512, 512
    per_seq = seq // tm
    return pl.pallas_call(
        _inproj_kernel,
        grid=(T // tm, N // tn),
        in_specs=[
            pl.BlockSpec((tm, D), lambda i, j: (i, 0)),
            pl.BlockSpec((1, D), lambda i, j: (0, 0)),
            pl.BlockSpec((1, 1, D), lambda i, j: (i // per_seq, 0, 0)),
            pl.BlockSpec((1, 1, D), lambda i, j: (i // per_seq, 0, 0)),
            pl.BlockSpec((D, tn), lambda i, j: (0, j)),
        ],
        out_specs=pl.BlockSpec((tm, tn), lambda i, j: (i, j)),
        out_shape=jax.ShapeDtypeStruct((T, N), F32),
        scratch_shapes=[pltpu.VMEM((tm, D), BF16)],
        compiler_params=_cparams(("parallel", "arbitrary")),
        name="in_proj",
    )(x, g, sc, sh, w)


def _conv_kernel(per_seq, a_ref, b_ref, ha_ref, hb_ref, dw_ref, db_ref, lg_ref, lb_ref, o_ref, ext):
    ts = a_ref.shape[0]
    first = (pl.program_id(0) % per_seq) == 0
    hu = ha_ref[...] * jax.nn.sigmoid(hb_ref[...])
    ext[0:HALO, :] = jnp.where(first, 0.0, hu)
    ext[HALO:, :] = a_ref[...] * jax.nn.sigmoid(b_ref[...])
    acc = jnp.zeros(a_ref.shape, F32)
    base = HALO - (CONV_WIDTH - 1)
    for w in range(CONV_WIDTH):
        acc = acc + ext[base + w:base + w + ts, :] * dw_ref[w:w + 1, :]
    acc = acc + db_ref[...]
    mu = jnp.mean(acc, axis=-1, keepdims=True)
    cen = acc - mu
    var = jnp.mean(cen * cen, axis=-1, keepdims=True)
    y = cen * lax.rsqrt(var + EPS) * lg_ref[...] + lb_ref[...]
    o_ref[...] = (y * jax.nn.sigmoid(y)).astype(BF16)


def _conv(proj, dw_w, dw_b, ln_g, ln_b, seq, a_col, b_col):
    T = proj.shape[0]
    C = dw_w.shape[1]
    ts = 256
    per_seq = seq // ts
    rb = ts // HALO
    halo_map = lambda col: (lambda i: (jnp.maximum(i * rb - 1, 0), col))
    return pl.pallas_call(
        functools.partial(_conv_kernel, per_seq),
        grid=(T // ts,),
        in_specs=[
            pl.BlockSpec((ts, C), lambda i: (i, a_col)),
            pl.BlockSpec((ts, C), lambda i: (i, b_col)),
            pl.BlockSpec((HALO, C), halo_map(a_col)),
            pl.BlockSpec((HALO, C), halo_map(b_col)),
            pl.BlockSpec((CONV_WIDTH, C), lambda i: (0, 0)),
            pl.BlockSpec((1, C), lambda i: (0, 0)),
            pl.BlockSpec((1, C), lambda i: (0, 0)),
            pl.BlockSpec((1, C), lambda i: (0, 0)),
        ],
        out_specs=pl.BlockSpec((ts, C), lambda i: (i, 0)),
        out_shape=jax.ShapeDtypeStruct((T, C), BF16),
        scratch_shapes=[pltpu.VMEM((HALO + ts, C), F32)],
        compiler_params=_cparams(("parallel",)),
        name="conv_branch",
    )(proj, proj, proj, proj, dw_w, dw_b, ln_g, ln_b)


def _attn_kernel(nb, q_ref, kc_ref, kp_ref, vc_ref, vp_ref, bias_ref, sink_ref, o_ref):
    first = (pl.program_id(0) % nb) == 0
    kk = jnp.concatenate([kp_ref[...], kc_ref[...]], axis=0).astype(BF16)
    vv = jnp.concatenate([vp_ref[...], vc_ref[...]], axis=0).astype(BF16)
    row = lax.broadcasted_iota(I32, (BLOCK, 2 * BLOCK), 0)
    col = lax.broadcasted_iota(I32, (BLOCK, 2 * BLOCK), 1)
    dist = row + BLOCK - col
    valid = (dist >= 0) & (dist < BLOCK) & (jnp.logical_not(first) | (col >= BLOCK))
    group = N_HEADS // N_KV_HEADS
    for kvh in range(N_KV_HEADS):
        k_h = kk[:, kvh * HEAD_DIM:(kvh + 1) * HEAD_DIM]
        v_h = vv[:, kvh * HEAD_DIM:(kvh + 1) * HEAD_DIM]
        qg = q_ref[:, kvh * group * HEAD_DIM:(kvh + 1) * group * HEAD_DIM].astype(BF16)
        outs = []
        for g in range(group):
            h = kvh * group + g
            q_h = qg[:, g * HEAD_DIM:(g + 1) * HEAD_DIM]
            s = lax.dot_general(q_h, k_h, (((1,), (1,)), ((), ())), preferred_element_type=F32)
            s = s * (HEAD_DIM ** -0.5) + bias_ref[h]
            s = jnp.where(valid, s, -jnp.inf)
            sink = sink_ref[h]
            m = jnp.maximum(jnp.max(s, axis=-1, keepdims=True), sink)
            e = jnp.exp(s - m)
            denom = jnp.sum(e, axis=-1, keepdims=True) + jnp.exp(sink - m)
            p = e / denom
            outs.append(jnp.dot(p.astype(BF16), v_h, preferred_element_type=F32))
        o_ref[:, kvh * group * HEAD_DIM:(kvh + 1) * group * HEAD_DIM] = (
            jnp.concatenate(outs, axis=1).astype(BF16))


def _attn(proj, bias_tab, sinks, seq, q_col, k_col, v_col):
    T = proj.shape[0]
    nb = seq // BLOCK
    attn_dim = N_HEADS * HEAD_DIM
    kv_dim = N_KV_HEADS * HEAD_DIM
    prev = lambda col: (lambda i: (jnp.where(i % nb == 0, i, i - 1), col))
    return pl.pallas_call(
        functools.partial(_attn_kernel, nb),
        grid=(T // BLOCK,),
        in_specs=[
            pl.BlockSpec((BLOCK, attn_dim), lambda i: (i, q_col)),
            pl.BlockSpec((BLOCK, kv_dim), lambda i: (i, k_col)),
            pl.BlockSpec((BLOCK, kv_dim), prev(k_col)),
            pl.BlockSpec((BLOCK, kv_dim), lambda i: (i, v_col)),
            pl.BlockSpec((BLOCK, kv_dim), prev(v_col)),
            pl.BlockSpec((N_HEADS, BLOCK, 2 * BLOCK), lambda i: (0, 0, 0)),
            pl.BlockSpec(memory_space=pltpu.SMEM),
        ],
        out_specs=pl.BlockSpec((BLOCK, attn_dim), lambda i: (i, 0)),
        out_shape=jax.ShapeDtypeStruct((T, attn_dim), BF16),
        compiler_params=_cparams(("parallel",)),
        name="swa_attention",
    )(proj, proj, proj, proj, proj, bias_tab, sinks)


def _mix_kernel(ca_ref, at_ref, gc_ref, ga_ref, x_ref, g1_ref, wc_ref, wa_ref, wo_ref, o_ref):
    yc = jnp.dot(ca_ref[...], wc_ref[...], preferred_element_type=F32)
    ya = jnp.dot(at_ref[...], wa_ref[...], preferred_element_type=F32)
    mixed = jax.nn.sigmoid(gc_ref[...]) * yc + jax.nn.sigmoid(ga_ref[...]) * ya
    y = jnp.dot(mixed.astype(BF16), wo_ref[...], preferred_element_type=F32)
    o_ref[...] = x_ref[...] + g1_ref[0] * y


def _mix(cact, attn, proj, x, g1, wc, wa, wo, seq, gc_col, ga_col):
    T, D = x.shape
    C = cact.shape[1]
    A = attn.shape[1]
    tm = 256
    per_seq = seq // tm
    return pl.pallas_call(
        _mix_kernel,
        grid=(T // tm,),
        in_specs=[
            pl.BlockSpec((tm, C), lambda i: (i, 0)),
            pl.BlockSpec((tm, A), lambda i: (i, 0)),
            pl.BlockSpec((tm, D), lambda i: (i, gc_col)),
            pl.BlockSpec((tm, D), lambda i: (i, ga_col)),
            pl.BlockSpec((tm, D), lambda i: (i, 0)),
            pl.BlockSpec((1, 1, D), lambda i: (i // per_seq, 0, 0)),
            _resident((C, D), lambda i: (0, 0)),
            _resident((A, D), lambda i: (0, 0)),
            _resident((D, D), lambda i: (0, 0)),
        ],
        out_specs=pl.BlockSpec((tm, D), lambda i: (i, 0)),
        out_shape=jax.ShapeDtypeStruct((T, D), F32),
        compiler_params=_cparams(("parallel",)),
        name="merge_out_proj",
    )(cact, attn, proj, proj, x, g1, wc, wa, wo)


def _peer_q_kernel(x_ref, g_ref, sc_ref, sh_ref, w_ref, k_ref, h_ref, st_ref):
    hb = _norm_mod(x_ref[...], g_ref[...], sc_ref[0], sh_ref[0]).astype(BF16)
    h_ref[...] = hb
    qb = jnp.dot(hb, w_ref[...], preferred_element_type=F32).astype(BF16)
    for hp in range(2 * PEER_HEADS):
        q_hp = qb[:, hp * PEER_HALF:(hp + 1) * PEER_HALF]
        st_ref[hp * N_KEYS:(hp + 1) * N_KEYS, :] = lax.dot_general(
            k_ref[hp], q_hp, (((1,), (1,)), ((), ())), preferred_element_type=F32)


def _peer_q(x, g, sc, sh, w_pq, keys, seq):
    T, D = x.shape
    Q = w_pq.shape[1]
    tm = 256
    per_seq = seq // tm
    n_rows = 2 * PEER_HEADS * N_KEYS
    return pl.pallas_call(
        _peer_q_kernel,
        grid=(T // tm,),
        in_specs=[
            pl.BlockSpec((tm, D), lambda i: (i, 0)),
            pl.BlockSpec((1, D), lambda i: (0, 0)),
            pl.BlockSpec((1, 1, D), lambda i: (i // per_seq, 0, 0)),
            pl.BlockSpec((1, 1, D), lambda i: (i // per_seq, 0, 0)),
            _resident((D, Q), lambda i: (0, 0)),
            _resident((2 * PEER_HEADS, N_KEYS, PEER_HALF), lambda i: (0, 0, 0)),
        ],
        out_specs=[
            pl.BlockSpec((tm, D), lambda i: (i, 0)),
            pl.BlockSpec((n_rows, tm), lambda i: (0, i)),
        ],
        out_shape=[
            jax.ShapeDtypeStruct((T, D), BF16),
            jax.ShapeDtypeStruct((n_rows, T), F32),
        ],
        compiler_params=_cparams(("parallel",)),
        name="peer_query_scores",
    )(x, g, sc, sh, w_pq, keys)


def _top_rows(x, payload, k):
    n_rows = x.shape[0]
    rows = lax.broadcasted_iota(I32, x.shape, 0)
    vals, pays = [], []
    for _ in range(k):
        m = jnp.max(x, axis=0, keepdims=True)
        first = jnp.min(jnp.where(x == m, rows, n_rows), axis=0, keepdims=True)
        onehot = rows == first
        if payload is None:
            pays.append(first)
        else:
            pays.append(jnp.sum(jnp.where(onehot, payload, 0), axis=0, keepdims=True))
        vals.append(m)
        x = jnp.where(onehot, -jnp.inf, x)
    return jnp.concatenate(vals, axis=0), jnp.concatenate(pays, axis=0)


def _route_kernel(block_rows, st_ref, idx_ref, idxl_ref, gate_ref, idx_scr, gate_scr):
    def head(h, carry):
        base = pl.multiple_of(h * (2 * N_KEYS), 2 * N_KEYS)
        v0, i0 = _top_rows(st_ref[pl.ds(base, N_KEYS), :], None, PEER_TOPK)
        v1, i1 = _top_rows(st_ref[pl.ds(base + N_KEYS, N_KEYS), :], None, PEER_TOPK)
        cand = jnp.concatenate([v0[k:k + 1] + v1 for k in range(PEER_TOPK)], axis=0)
        cidx = jnp.concatenate([i0[k:k + 1] * N_KEYS + i1 for k in range(PEER_TOPK)], axis=0)
        best, eid = _top_rows(cand, cidx, PEER_TOPK)
        e = jnp.exp(best - jnp.max(best, axis=0, keepdims=True))
        gates = e / jnp.sum(e, axis=0, keepdims=True)
        off = pl.multiple_of(h * PEER_TOPK, PEER_TOPK)
        idx_scr[pl.ds(off, PEER_TOPK), :] = eid.astype(F32)
        gate_scr[pl.ds(off, PEER_TOPK), :] = gates
        return carry

    lax.fori_loop(0, PEER_HEADS, head, 0)
    idx = idx_scr[...].T.astype(I32)
    idx_ref[...] = idx
    gate_ref[...] = gate_scr[...].T
    for b in range(N_TABLE_BLOCKS):
        inb = (idx >= b * block_rows) & (idx < (b + 1) * block_rows)
        idxl_ref[b] = jnp.where(inb, idx - b * block_rows, 0)


def _route(st, block_rows):
    n_rows, T = st.shape
    tr = 128
    return pl.pallas_call(
        functools.partial(_route_kernel, block_rows),
        grid=(T // tr,),
        in_specs=[pl.BlockSpec((n_rows, tr), lambda i: (0, i))],
        out_specs=[
            pl.BlockSpec((tr, PAIRS), lambda i: (i, 0)),
            pl.BlockSpec((N_TABLE_BLOCKS, tr, PAIRS), lambda i: (0, i, 0)),
            pl.BlockSpec((tr, PAIRS), lambda i: (i, 0)),
        ],
        out_shape=[
            jax.ShapeDtypeStruct((T, PAIRS), I32),
            jax.ShapeDtypeStruct((N_TABLE_BLOCKS, T, PAIRS), I32),
            jax.ShapeDtypeStruct((T, PAIRS), F32),
        ],
        scratch_shapes=[pltpu.VMEM((PAIRS, tr), F32), pltpu.VMEM((PAIRS, tr), F32)],
        compiler_params=_cparams(("parallel",)),
        name="peer_route",
    )(st)


def _diag_mask():
    r = lax.broadcasted_iota(I32, (SLAB_ROWS, PAIRS * SLAB_ROWS), 0)
    c = lax.broadcasted_iota(I32, (SLAB_ROWS, PAIRS * SLAB_ROWS), 1)
    return r == (c % SLAB_ROWS)


def _peer_u_kernel(idx_ref, h_ref, gsum_ref, u_ref, o_ref, diag_scr):
    tm = h_ref.shape[0]
    eye = _diag_mask()

    def tok(t, carry):
        usel = jnp.concatenate([u_ref[idx_ref[0, t, p]] for p in range(PAIRS)], axis=0)
        r = lax.dot_general(h_ref[t], usel, (((1,), (1,)), ((), ())), preferred_element_type=F32)
        diag_scr[pl.ds(t, 1), :] = jnp.sum(jnp.where(eye, r, 0.0), axis=0, keepdims=True)
        return carry

    lax.fori_loop(0, tm, tok, 0, unroll=TOKEN_UNROLL)
    d = diag_scr[...]
    hi = d.astype(BF16)
    lo = (d - hi.astype(F32)).astype(BF16)
    o_ref[0] = (jnp.dot(hi, gsum_ref[...], preferred_element_type=F32)
                + jnp.dot(lo, gsum_ref[...], preferred_element_type=F32))


def _peer_u(idxl, h_slab, group_sum, u_tab):
    T = h_slab.shape[0]
    block_rows = u_tab.shape[0] // N_TABLE_BLOCKS
    tm = 128
    return pl.pallas_call(
        _peer_u_kernel,
        grid=(N_TABLE_BLOCKS, T // tm),
        in_specs=[
            pl.BlockSpec((1, tm, PAIRS), lambda b, i: (b, i, 0), memory_space=pltpu.SMEM),
            pl.BlockSpec((tm, SLAB_ROWS, LANES), lambda b, i: (i, 0, 0)),
            _resident((PAIRS * SLAB_ROWS, PAIRS), lambda b, i: (0, 0)),
            _resident((block_rows, SLAB_ROWS, LANES), lambda b, i: (b, 0, 0)),
        ],
        out_specs=pl.BlockSpec((1, tm, PAIRS), lambda b, i: (b, i, 0)),
        out_shape=jax.ShapeDtypeStruct((N_TABLE_BLOCKS, T, PAIRS), F32),
        scratch_shapes=[pltpu.VMEM((tm, PAIRS * SLAB_ROWS), F32)],
        compiler_params=_cparams(("arbitrary", "arbitrary")),
        name="peer_expert_u",
    )(idxl, h_slab, group_sum, u_tab)


def _peer_v_kernel(block_rows, idxl_ref, idx_ref, gate_ref, act_ref, ex_ref, v_ref, o_ref, wexp):
    tm = idx_ref.shape[0]
    b = pl.program_id(0)
    idx = idx_ref[...]
    act = act_ref[0]
    for k in range(1, N_TABLE_BLOCKS):
        act = jnp.where(idx >= k * block_rows, act_ref[k], act)
    gelu = 0.5 * act * (1.0 + lax.erf(act * np.float32(math.sqrt(0.5))))
    inb = (idx >= b * block_rows) & (idx < (b + 1) * block_rows)
    w = jnp.where(inb, gate_ref[...] * gelu, 0.0)
    wexp[...] = jnp.dot(w.astype(BF16), ex_ref[...], preferred_element_type=F32)
    eye = _diag_mask()

    def tok(t, carry):
        vsel = jnp.concatenate([v_ref[idxl_ref[0, t, p]] for p in range(PAIRS)], axis=0)
        wrow = jnp.broadcast_to(wexp[pl.ds(t, 1), :], (SLAB_ROWS, PAIRS * SLAB_ROWS))
        wbd = jnp.where(eye, wrow, 0.0).astype(BF16)
        o_ref[0, t] = jnp.dot(wbd, vsel, preferred_element_type=F32)
        return carry

    lax.fori_loop(0, tm, tok, 0, unroll=TOKEN_UNROLL)


def _peer_v(idxl, idx, gate, act, expand, v_tab):
    T = idx.shape[0]
    block_rows = v_tab.shape[0] // N_TABLE_BLOCKS
    tm = 128
    return pl.pallas_call(
        functools.partial(_peer_v_kernel, block_rows),
        grid=(N_TABLE_BLOCKS, T // tm),
        in_specs=[
            pl.BlockSpec((1, tm, PAIRS), lambda b, i: (b, i, 0), memory_space=pltpu.SMEM),
            pl.BlockSpec((tm, PAIRS), lambda b, i: (i, 0)),
            pl.BlockSpec((tm, PAIRS), lambda b, i: (i, 0)),
            pl.BlockSpec((N_TABLE_BLOCKS, tm, PAIRS), lambda b, i: (0, i, 0)),
            _resident((PAIRS, PAIRS * SLAB_ROWS), lambda b, i: (0, 0)),
            _resident((block_rows, SLAB_ROWS, LANES), lambda b, i: (b, 0, 0)),
        ],
        out_specs=pl.BlockSpec((1, tm, SLAB_ROWS, LANES), lambda b, i: (b, i, 0, 0)),
        out_shape=jax.ShapeDtypeStruct((N_TABLE_BLOCKS, T, SLAB_ROWS, LANES), F32),
        scratch_shapes=[pltpu.VMEM((tm, PAIRS * SLAB_ROWS), F32)],
        compiler_params=_cparams(("arbitrary", "arbitrary")),
        name="peer_expert_v",
    )(idxl, idx, gate, act, expand, v_tab)


def _resid_kernel(final, x_ref, o_ref, g2_ref, fg_ref, y_ref):
    acc = o_ref[0]
    for k in range(1, N_TABLE_BLOCKS):
        acc = acc + o_ref[k]
    x = x_ref[...] + g2_ref[0] * acc
    if final:
        ms = jnp.mean(x * x, axis=-1, keepdims=True)
        x = (x * lax.rsqrt(ms + EPS)) * fg_ref[...]
    y_ref[...] = x


def _resid(x, peer_out, g2, final_g, seq, final):
    T, D = x.shape
    tm = 512
    per_seq = seq // tm
    return pl.pallas_call(
        functools.partial(_resid_kernel, final),
        grid=(T // tm,),
        in_specs=[
            pl.BlockSpec((tm, D), lambda i: (i, 0)),
            pl.BlockSpec((N_TABLE_BLOCKS, tm, D), lambda i: (0, i, 0)),
            pl.BlockSpec((1, 1, D), lambda i: (i // per_seq, 0, 0)),
            pl.BlockSpec((1, D), lambda i: (0, 0)),
        ],
        out_specs=pl.BlockSpec((tm, D), lambda i: (i, 0)),
        out_shape=jax.ShapeDtypeStruct((T, D), F32),
        compiler_params=_cparams(("parallel",)),
        name="peer_residual",
    )(x, peer_out, g2, final_g)


def _pair_expand_matrix():
    p = np.arange(PAIRS)[:, None]
    c = np.arange(PAIRS * SLAB_ROWS)[None, :]
    return (p == c // SLAB_ROWS).astype(np.float32)


def kernel(x, c, rel_bias, ada_w, ada_b, norm1_g, w_in, dw_w, dw_b, conv_ln_g, conv_ln_b, w_conv_out,
           attn_sinks, w_attn_out, w_out, norm2_g, w_pq, sub_keys, peer_u, peer_v, final_g):
    B, S, D = x.shape
    L = ada_w.shape[0]
    T = B * S
    C = dw_w.shape[2]
    A = N_HEADS * HEAD_DIM
    KV = N_KV_HEADS * HEAD_DIM
    E = peer_u.shape[1]
    assert S % 512 == 0 and D == SLAB_ROWS * LANES and E % N_TABLE_BLOCKS == 0

    o_a, o_b, o_q, o_k, o_v, o_g = 0, C, 2 * C, 2 * C + A, 2 * C + A + KV, 2 * C + A + 2 * KV
    order = np.concatenate([np.arange(o_a, o_a + 2 * C), np.arange(o_g, o_g + 2 * D),
                            np.arange(o_q, o_q + A + 2 * KV)])
    n_a, n_b, n_gc, n_ga, n_q, n_k, n_v = 0, C, 2 * C, 2 * C + D, 2 * C + 2 * D, 2 * C + 2 * D + A, 2 * C + 2 * D + A + KV

    c_pad = jnp.pad(c, ((0, 8 - B), (0, 0)))
    mod = _ada(c_pad, ada_w, ada_b)
    bias_tab = _rel_bias_table(rel_bias)
    expand = jnp.asarray(_pair_expand_matrix(), dtype=BF16)
    group_sum = jnp.asarray(_pair_expand_matrix().T, dtype=BF16)
    final_g2 = final_g.reshape(1, D)

    xt = x.reshape(T, D)
    for l in range(L):
        m = mod[l, :B].reshape(B, 6, 1, D)
        sh1, sc1, g1, sh2, sc2, g2 = (m[:, k] for k in range(6))

        w_in_l = w_in[l][:, order].astype(BF16)
        proj = _inproj(xt, norm1_g[l].reshape(1, D), sc1, sh1, w_in_l, S)
        cact = _conv(proj, dw_w[l], dw_b[l].reshape(1, C), conv_ln_g[l].reshape(1, C),
                     conv_ln_b[l].reshape(1, C), S, n_a // C, n_b // C)
        attn = _attn(proj, bias_tab, attn_sinks[l], S, n_q // A, n_k // KV, n_v // KV)
        xt = _mix(cact, attn, proj, xt, g1, w_conv_out[l].astype(BF16), w_attn_out[l].astype(BF16),
                  w_out[l].astype(BF16), S, n_gc // D, n_ga // D)

        keys = sub_keys[l].reshape(2 * PEER_HEADS, N_KEYS, PEER_HALF).astype(BF16)
        h2, st = _peer_q(xt, norm2_g[l].reshape(1, D), sc2, sh2, w_pq[l].astype(BF16), keys, S)
        idx, idxl, gate = _route(st, E // N_TABLE_BLOCKS)
        u_tab = peer_u[l].astype(BF16).reshape(E, SLAB_ROWS, LANES)
        v_tab = peer_v[l].astype(BF16).reshape(E, SLAB_ROWS, LANES)
        act = _peer_u(idxl, h2.reshape(T, SLAB_ROWS, LANES), group_sum, u_tab)
        pout = _peer_v(idxl, idx, gate, act, expand, v_tab)
        xt = _resid(xt, pout.reshape(N_TABLE_BLOCKS, T, D), g2, final_g2, S, l == L - 1)
    return xt.reshape(B, S, D)
```

```python
import functools
import math

import numpy as np
import jax
import jax.numpy as jnp
from jax import lax
from jax.experimental import pallas as pl
from jax.experimental.pallas import tpu as pltpu

F32 = jnp.float32
BF16 = jnp.bfloat16
I32 = jnp.int32

EPS = 1e-6
CONV_WIDTH = 31
HALO = 32
N_HEADS = 16
N_KV_HEADS = 4
HEAD_DIM = 64
BLOCK = 128
NUM_BUCKETS = 32
MAX_DISTANCE = 128
N_KEYS = 128
PEER_HEADS = 8
PEER_TOPK = 16
PEER_HALF = 128
PAIRS = PEER_HEADS * PEER_TOPK
SLAB_ROWS = 16
LANES = 128
N_TABLE_BLOCKS = 2
TOKEN_UNROLL = 4
WINDOW_SLOTS = 80
assert PAIRS & (PAIRS - 1) == 0 and WINDOW_SLOTS % 8 == 0 and 2 * WINDOW_SLOTS >= PAIRS
VMEM_LIMIT = 56 * 1024 * 1024


def _cparams(sem, vmem=VMEM_LIMIT):
    return pltpu.CompilerParams(dimension_semantics=sem, vmem_limit_bytes=vmem)


def _resident(block_shape, index_map):
    return pl.BlockSpec(block_shape, index_map, pipeline_mode=pl.Buffered(1))


def _ada_kernel(c_ref, w_ref, b_ref, o_ref):
    c = c_ref[...]
    cs = c * jax.nn.sigmoid(c)
    o_ref[0] = jnp.dot(cs.astype(BF16), w_ref[0].astype(BF16), preferred_element_type=F32) + b_ref[0]


def _ada(c_pad, ada_w, ada_b):
    L, D, N = ada_w.shape
    tn = 1024
    return pl.pallas_call(
        _ada_kernel,
        grid=(L, N // tn),
        in_specs=[
            pl.BlockSpec((8, D), lambda l, j: (0, 0)),
            pl.BlockSpec((1, D, tn), lambda l, j: (l, 0, j)),
            pl.BlockSpec((1, 1, tn), lambda l, j: (l, 0, j)),
        ],
        out_specs=pl.BlockSpec((1, 8, tn), lambda l, j: (l, 0, j)),
        out_shape=jax.ShapeDtypeStruct((L, 8, N), F32),
        compiler_params=_cparams(("parallel", "parallel")),
        name="ada_mod",
    )(c_pad, ada_w, ada_b.reshape(L, 1, N))


def _bias_kernel(bucket_ref, rb_ref, o_ref):
    h = pl.program_id(0)
    bucket = bucket_ref[...]
    acc = jnp.zeros(bucket.shape, F32)
    for b in range(NUM_BUCKETS):
        acc = jnp.where(bucket == b, rb_ref[b, h], acc)
    o_ref[0] = acc


def _t5_bucket_table():
    qi = np.arange(BLOCK)[:, None] + BLOCK
    kj = np.arange(2 * BLOCK)[None, :]
    d = np.maximum(qi - kj, 0)
    max_exact = NUM_BUCKETS // 2
    ratio = (np.log(np.maximum(d, 1).astype(np.float64) / max_exact)
             / math.log(MAX_DISTANCE / max_exact) * (NUM_BUCKETS - max_exact))
    large = np.minimum(max_exact + ratio.astype(np.int64), NUM_BUCKETS - 1)
    return np.where(d < max_exact, d, large).astype(np.int32)


def _rel_bias_table(rel_bias):
    bucket = jnp.asarray(_t5_bucket_table())
    return pl.pallas_call(
        _bias_kernel,
        grid=(N_HEADS,),
        in_specs=[
            pl.BlockSpec((BLOCK, 2 * BLOCK), lambda h: (0, 0)),
            pl.BlockSpec(memory_space=pltpu.SMEM),
        ],
        out_specs=pl.BlockSpec((1, BLOCK, 2 * BLOCK), lambda h: (h, 0, 0)),
        out_shape=jax.ShapeDtypeStruct((N_HEADS, BLOCK, 2 * BLOCK), F32),
        compiler_params=_cparams(("arbitrary",)),
        name="rel_bias_table",
    )(bucket, rel_bias)


def _norm_mod(x, g, sc, sh):
    ms = jnp.mean(x * x, axis=-1, keepdims=True)
    y = x * lax.rsqrt(ms + EPS)
    return (y * g) * (1.0 + sc) + sh


def _inproj_kernel(x_ref, g_ref, sc_ref, sh_ref, w_ref, o_ref, h_scr):
    @pl.when(pl.program_id(1) == 0)
    def _():
        h_scr[...] = _norm_mod(x_ref[...], g_ref[...], sc_ref[0], sh_ref[0]).astype(BF16)

    o_ref[...] = jnp.dot(h_scr[...], w_ref[...], preferred_element_type=F32)


def _inproj(x, g, sc, sh, w, seq):
    T, D = x.shape
    N = w.shape[1]
    tm, tn = 512, 512
    per_seq = seq // tm
    return pl.pallas_call(
        _inproj_kernel,
        grid=(T // tm, N // tn),
        in_specs=[
            pl.BlockSpec((tm, D), lambda i, j: (i, 0)),
            pl.BlockSpec((1, D), lambda i, j: (0, 0)),
            pl.BlockSpec((1, 1, D), lambda i, j: (i // per_seq, 0, 0)),
            pl.BlockSpec((1, 1, D), lambda i, j: (i // per_seq, 0, 0)),
            pl.BlockSpec((D, tn), lambda i, j: (0, j)),
        ],
        out_specs=pl.BlockSpec((tm, tn), lambda i, j: (i, j)),
        out_shape=jax.ShapeDtypeStruct((T, N), F32),
        scratch_shapes=[pltpu.VMEM((tm, D), BF16)],
        compiler_params=_cparams(("parallel", "arbitrary")),
        name="in_proj",
    )(x, g, sc, sh, w)


def _conv_kernel(per_seq, a_ref, b_ref, ha_ref, hb_ref, dw_ref, db_ref, lg_ref, lb_ref, o_ref, ext):
    ts = a_ref.shape[0]
    first = (pl.program_id(0) % per_seq) == 0
    hu = ha_ref[...] * jax.nn.sigmoid(hb_ref[...])
    ext[0:HALO, :] = jnp.where(first, 0.0, hu)
    ext[HALO:, :] = a_ref[...] * jax.nn.sigmoid(b_ref[...])
    acc = jnp.zeros(a_ref.shape, F32)
    base = HALO - (CONV_WIDTH - 1)
    for w in range(CONV_WIDTH):
        acc = acc + ext[base + w:base + w + ts, :] * dw_ref[w:w + 1, :]
    acc = acc + db_ref[...]
    mu = jnp.mean(acc, axis=-1, keepdims=True)
    cen = acc - mu
    var = jnp.mean(cen * cen, axis=-1, keepdims=True)
    y = cen * lax.rsqrt(var + EPS) * lg_ref[...] + lb_ref[...]
    o_ref[...] = (y * jax.nn.sigmoid(y)).astype(BF16)


def _conv(proj, dw_w, dw_b, ln_g, ln_b, seq, a_col, b_col):
    T = proj.shape[0]
    C = dw_w.shape[1]
    ts = 256
    per_seq = seq // ts
    rb = ts // HALO
    halo_map = lambda col: (lambda i: (jnp.maximum(i * rb - 1, 0), col))
    return pl.pallas_call(
        functools.partial(_conv_kernel, per_seq),
        grid=(T // ts,),
        in_specs=[
            pl.BlockSpec((ts, C), lambda i: (i, a_col)),
            pl.BlockSpec((ts, C), lambda i: (i, b_col)),
            pl.BlockSpec((HALO, C), halo_map(a_col)),
            pl.BlockSpec((HALO, C), halo_map(b_col)),
            pl.BlockSpec((CONV_WIDTH, C), lambda i: (0, 0)),
            pl.BlockSpec((1, C), lambda i: (0, 0)),
            pl.BlockSpec((1, C), lambda i: (0, 0)),
            pl.BlockSpec((1, C), lambda i: (0, 0)),
        ],
        out_specs=pl.BlockSpec((ts, C), lambda i: (i, 0)),
        out_shape=jax.ShapeDtypeStruct((T, C), BF16),
        scratch_shapes=[pltpu.VMEM((HALO + ts, C), F32)],
        compiler_params=_cparams(("parallel",)),
        name="conv_branch",
    )(proj, proj, proj, proj, dw_w, dw_b, ln_g, ln_b)


def _attn_kernel(nb, q_ref, kc_ref, kp_ref, vc_ref, vp_ref, bias_ref, sink_ref, o_ref):
    first = (pl.program_id(0) % nb) == 0
    kk = jnp.concatenate([kp_ref[...], kc_ref[...]], axis=0).astype(BF16)
    vv = jnp.concatenate([vp_ref[...], vc_ref[...]], axis=0).astype(BF16)
    row = lax.broadcasted_iota(I32, (BLOCK, 2 * BLOCK), 0)
    col = lax.broadcasted_iota(I32, (BLOCK, 2 * BLOCK), 1)
    dist = row + BLOCK - col
    valid = (dist >= 0) & (dist < BLOCK) & (jnp.logical_not(first) | (col >= BLOCK))
    group = N_HEADS // N_KV_HEADS
    for kvh in range(N_KV_HEADS):
        k_h = kk[:, kvh * HEAD_DIM:(kvh + 1) * HEAD_DIM]
        v_h = vv[:, kvh * HEAD_DIM:(kvh + 1) * HEAD_DIM]
        qg = q_ref[:, kvh * group * HEAD_DIM:(kvh + 1) * group * HEAD_DIM].astype(BF16)
        outs = []
        for g in range(group):
            h = kvh * group + g
            q_h = qg[:, g * HEAD_DIM:(g + 1) * HEAD_DIM]
            s = lax.dot_general(q_h, k_h, (((1,), (1,)), ((), ())), preferred_element_type=F32)
            s = s * (HEAD_DIM ** -0.5) + bias_ref[h]
            s = jnp.where(valid, s, -jnp.inf)
            sink = sink_ref[h]
            m = jnp.maximum(jnp.max(s, axis=-1, keepdims=True), sink)
            e = jnp.exp(s - m)
            denom = jnp.sum(e, axis=-1, keepdims=True) + jnp.exp(sink - m)
            p = e / denom
            outs.append(jnp.dot(p.astype(BF16), v_h, preferred_element_type=F32))
        o_ref[:, kvh * group * HEAD_DIM:(kvh + 1) * group * HEAD_DIM] = (
            jnp.concatenate(outs, axis=1).astype(BF16))


def _attn(proj, bias_tab, sinks, seq, q_col, k_col, v_col):
    T = proj.shape[0]
    nb = seq // BLOCK
    attn_dim = N_HEADS * HEAD_DIM
    kv_dim = N_KV_HEADS * HEAD_DIM
    prev = lambda col: (lambda i: (jnp.where(i % nb == 0, i, i - 1), col))
    return pl.pallas_call(
        functools.partial(_attn_kernel, nb),
        grid=(T // BLOCK,),
        in_specs=[
            pl.BlockSpec((BLOCK, attn_dim), lambda i: (i, q_col)),
            pl.BlockSpec((BLOCK, kv_dim), lambda i: (i, k_col)),
            pl.BlockSpec((BLOCK, kv_dim), prev(k_col)),
            pl.BlockSpec((BLOCK, kv_dim), lambda i: (i, v_col)),
            pl.BlockSpec((BLOCK, kv_dim), prev(v_col)),
            pl.BlockSpec((N_HEADS, BLOCK, 2 * BLOCK), lambda i: (0, 0, 0)),
            pl.BlockSpec(memory_space=pltpu.SMEM),
        ],
        out_specs=pl.BlockSpec((BLOCK, attn_dim), lambda i: (i, 0)),
        out_shape=jax.ShapeDtypeStruct((T, attn_dim), BF16),
        compiler_params=_cparams(("parallel",)),
        name="swa_attention",
    )(proj, proj, proj, proj, proj, bias_tab, sinks)


def _mix_kernel(ca_ref, at_ref, gc_ref, ga_ref, x_ref, g1_ref, wc_ref, wa_ref, wo_ref, o_ref):
    yc = jnp.dot(ca_ref[...], wc_ref[...], preferred_element_type=F32)
    ya = jnp.dot(at_ref[...], wa_ref[...], preferred_element_type=F32)
    mixed = jax.nn.sigmoid(gc_ref[...]) * yc + jax.nn.sigmoid(ga_ref[...]) * ya
    y = jnp.dot(mixed.astype(BF16), wo_ref[...], preferred_element_type=F32)
    o_ref[...] = x_ref[...] + g1_ref[0] * y


def _mix(cact, attn, proj, x, g1, wc, wa, wo, seq, gc_col, ga_col):
    T, D = x.shape
    C = cact.shape[1]
    A = attn.shape[1]
    tm = 256
    per_seq = seq // tm
    return pl.pallas_call(
        _mix_kernel,
        grid=(T // tm,),
        in_specs=[
            pl.BlockSpec((tm, C), lambda i: (i, 0)),
            pl.BlockSpec((tm, A), lambda i: (i, 0)),
            pl.BlockSpec((tm, D), lambda i: (i, gc_col)),
            pl.BlockSpec((tm, D), lambda i: (i, ga_col)),
            pl.BlockSpec((tm, D), lambda i: (i, 0)),
            pl.BlockSpec((1, 1, D), lambda i: (i // per_seq, 0, 0)),
            _resident((C, D), lambda i: (0, 0)),
            _resident((A, D), lambda i: (0, 0)),
            _resident((D, D), lambda i: (0, 0)),
        ],
        out_specs=pl.BlockSpec((tm, D), lambda i: (i, 0)),
        out_shape=jax.ShapeDtypeStruct((T, D), F32),
        compiler_params=_cparams(("parallel",)),
        name="merge_out_proj",
    )(cact, attn, proj, proj, x, g1, wc, wa, wo)


def _peer_q_kernel(x_ref, g_ref, sc_ref, sh_ref, w_ref, k_ref, h_ref, st_ref):
    hb = _norm_mod(x_ref[...], g_ref[...], sc_ref[0], sh_ref[0]).astype(BF16)
    h_ref[...] = hb
    qb = jnp.dot(hb, w_ref[...], preferred_element_type=F32).astype(BF16)
    for hp in range(2 * PEER_HEADS):
        q_hp = qb[:, hp * PEER_HALF:(hp + 1) * PEER_HALF]
        st_ref[hp * N_KEYS:(hp + 1) * N_KEYS, :] = lax.dot_general(
            k_ref[hp], q_hp, (((1,), (1,)), ((), ())), preferred_element_type=F32)


def _peer_q(x, g, sc, sh, w_pq, keys, seq):
    T, D = x.shape
    Q = w_pq.shape[1]
    tm = 256
    per_seq = seq // tm
    n_rows = 2 * PEER_HEADS * N_KEYS
    return pl.pallas_call(
        _peer_q_kernel,
        grid=(T // tm,),
        in_specs=[
            pl.BlockSpec((tm, D), lambda i: (i, 0)),
            pl.BlockSpec((1, D), lambda i: (0, 0)),
            pl.BlockSpec((1, 1, D), lambda i: (i // per_seq, 0, 0)),
            pl.BlockSpec((1, 1, D), lambda i: (i // per_seq, 0, 0)),
            _resident((D, Q), lambda i: (0, 0)),
            _resident((2 * PEER_HEADS, N_KEYS, PEER_HALF), lambda i: (0, 0, 0)),
        ],
        out_specs=[
            pl.BlockSpec((tm, D), lambda i: (i, 0)),
            pl.BlockSpec((n_rows, tm), lambda i: (0, i)),
        ],
        out_shape=[
            jax.ShapeDtypeStruct((T, D), BF16),
            jax.ShapeDtypeStruct((n_rows, T), F32),
        ],
        compiler_params=_cparams(("parallel",)),
        name="peer_query_scores",
    )(x, g, sc, sh, w_pq, keys)


def _top_rows(x, payload, k):
    n_rows = x.shape[0]
    rows = lax.broadcasted_iota(I32, x.shape, 0)
    vals, pays = [], []
    for _ in range(k):
        m = jnp.max(x, axis=0, keepdims=True)
        first = jnp.min(jnp.where(x == m, rows, n_rows), axis=0, keepdims=True)
        onehot = rows == first
        if payload is None:
            pays.append(first)
        else:
            pays.append(jnp.sum(jnp.where(onehot, payload, 0), axis=0, keepdims=True))
        vals.append(m)
        x = jnp.where(onehot, -jnp.inf, x)
    return jnp.concatenate(vals, axis=0), jnp.concatenate(pays, axis=0)


def _sort_rows(keys, pays):
    n_groups = len(keys)
    n_rows = 8 * n_groups
    sub = lax.broadcasted_iota(I32, keys[0].shape, 0)
    k = 2
    while k <= n_rows:
        j = k // 2
        while j >= 1:
            if j >= 8:
                gj = j // 8
                for lo in range(n_groups):
                    if lo & gj:
                        continue
                    hi = lo | gj
                    ascending = ((8 * lo) & k) == 0
                    a, b = keys[lo], keys[hi]
                    swap = (a > b) if ascending else (a < b)
                    keys[lo], keys[hi] = jnp.where(swap, b, a), jnp.where(swap, a, b)
                    pa, pb = pays[lo], pays[hi]
                    pays[lo], pays[hi] = jnp.where(swap, pb, pa), jnp.where(swap, pa, pb)
            else:
                lower = (sub & j) == 0
                upper = (sub & j) != 0
                for g in range(n_groups):
                    if k >= 8:
                        ascending = ((8 * g) & k) == 0
                        want_max = upper if ascending else lower
                    else:
                        want_max = jnp.logical_xor(upper, (sub & k) != 0)
                    x, p = keys[g], pays[g]
                    xp = jnp.where(lower, pltpu.roll(x, 8 - j, 0), pltpu.roll(x, j, 0))
                    pp = jnp.where(lower, pltpu.roll(p, 8 - j, 0), pltpu.roll(p, j, 0))
                    swap = jnp.logical_xor(xp < x, want_max)
                    keys[g] = jnp.where(swap, xp, x)
                    pays[g] = jnp.where(swap, pp, p)
            j //= 2
        k *= 2
    return keys, pays


def _route_kernel(block_rows, st_ref, idxl_ref, gate_ref, ovf_ref, idx_scr, gate_scr):
    def head(h, carry):
        base = pl.multiple_of(h * (2 * N_KEYS), 2 * N_KEYS)
        v0, i0 = _top_rows(st_ref[pl.ds(base, N_KEYS), :], None, PEER_TOPK)
        v1, i1 = _top_rows(st_ref[pl.ds(base + N_KEYS, N_KEYS), :], None, PEER_TOPK)
        cand = jnp.concatenate([v0[k:k + 1] + v1 for k in range(PEER_TOPK)], axis=0)
        cidx = jnp.concatenate([i0[k:k + 1] * N_KEYS + i1 for k in range(PEER_TOPK)], axis=0)
        best, eid = _top_rows(cand, cidx, PEER_TOPK)
        e = jnp.exp(best - jnp.max(best, axis=0, keepdims=True))
        gates = e / jnp.sum(e, axis=0, keepdims=True)
        off = pl.multiple_of(h * PEER_TOPK, PEER_TOPK)
        idx_scr[pl.ds(off, PEER_TOPK), :] = eid
        gate_scr[pl.ds(off, PEER_TOPK), :] = gates
        return carry

    lax.fori_loop(0, PEER_HEADS, head, 0)

    n_groups = PAIRS // 8
    sub = lax.broadcasted_iota(I32, (8, idx_scr.shape[1]), 0)
    keys = [idx_scr[8 * g:8 * g + 8, :] * PAIRS + (sub + 8 * g) for g in range(n_groups)]
    pays = [gate_scr[8 * g:8 * g + 8, :] for g in range(n_groups)]
    keys, pays = _sort_rows(keys, pays)
    eids = [jnp.right_shift(k, PAIRS.bit_length() - 1) for k in keys]
    row_hi, row_lo = WINDOW_SLOTS, PAIRS - WINDOW_SLOTS - 1
    ovf0 = jnp.where(eids[row_hi // 8][row_hi % 8:row_hi % 8 + 1, :] < block_rows, 1, 0)
    ovf1 = jnp.where(eids[row_lo // 8][row_lo % 8:row_lo % 8 + 1, :] >= block_rows, 1, 0)
    ovf_ref[...] = jnp.concatenate([ovf0, ovf1], axis=0).astype(I32)
    rot = (PAIRS - WINDOW_SLOTS) // 8
    for b in range(N_TABLE_BLOCKS):
        order = [(g + b * rot) % n_groups for g in range(n_groups)]
        e = jnp.concatenate([eids[g] for g in order], axis=0)
        inb = (e >= b * block_rows) & (e < (b + 1) * block_rows)
        local = jnp.where(inb, e - b * block_rows, 0)
        gate = jnp.where(inb, jnp.concatenate([pays[g] for g in order], axis=0), 0.0)
        idxl_ref[b] = local.astype(F32).T.astype(I32)
        gate_ref[b] = gate.T


def _route(st, block_rows):
    n_rows, T = st.shape
    tr = 128
    return pl.pallas_call(
        functools.partial(_route_kernel, block_rows),
        grid=(T // tr,),
        in_specs=[pl.BlockSpec((n_rows, tr), lambda i: (0, i))],
        out_specs=[
            pl.BlockSpec((N_TABLE_BLOCKS, tr, PAIRS), lambda i: (0, i, 0)),
            pl.BlockSpec((N_TABLE_BLOCKS, tr, PAIRS), lambda i: (0, i, 0)),
            pl.BlockSpec((N_TABLE_BLOCKS, tr), lambda i: (0, i)),
        ],
        out_shape=[
            jax.ShapeDtypeStruct((N_TABLE_BLOCKS, T, PAIRS), I32),
            jax.ShapeDtypeStruct((N_TABLE_BLOCKS, T, PAIRS), F32),
            jax.ShapeDtypeStruct((N_TABLE_BLOCKS, T), I32),
        ],
        scratch_shapes=[pltpu.VMEM((PAIRS, tr), I32), pltpu.VMEM((PAIRS, tr), F32)],
        compiler_params=_cparams(("parallel",)),
        name="peer_route",
    )(st)


def _diag_mask():
    r = lax.broadcasted_iota(I32, (SLAB_ROWS, PAIRS * SLAB_ROWS), 0)
    c = lax.broadcasted_iota(I32, (SLAB_ROWS, PAIRS * SLAB_ROWS), 1)
    return r == (c % SLAB_ROWS)


def _gather_slabs(tab_ref, idx_ref, t, slots):
    return jnp.concatenate([tab_ref[idx_ref[0, t, j]] for j in slots], axis=0)


def _peer_u_kernel(idx_ref, ovf_ref, h_ref, gsum_ref, u_ref, o_ref, diag_scr):
    tm = h_ref.shape[0]
    b = pl.program_id(0)
    eye = _diag_mask()
    wcols = WINDOW_SLOTS * SLAB_ROWS

    def diag(t, slots, lo, hi):
        usel = _gather_slabs(u_ref, idx_ref, t, slots)
        r = lax.dot_general(h_ref[t], usel, (((1,), (1,)), ((), ())), preferred_element_type=F32)
        diag_scr[pl.ds(t, 1), lo:hi] = jnp.sum(jnp.where(eye[:, lo:hi], r, 0.0), axis=0, keepdims=True)

    def tok(t, carry):
        diag(t, range(WINDOW_SLOTS), 0, wcols)
        return carry

    def rare(t, carry):
        @pl.when(ovf_ref[b, t] != 0)
        def _():
            diag(t, range(WINDOW_SLOTS, PAIRS), wcols, PAIRS * SLAB_ROWS)
        return carry

    diag_scr[:, wcols:] = jnp.zeros((tm, PAIRS * SLAB_ROWS - wcols), F32)
    lax.fori_loop(0, tm, tok, 0, unroll=TOKEN_UNROLL)
    lax.fori_loop(0, tm, rare, 0)
    d = diag_scr[...]
    hi = d.astype(BF16)
    lo = (d - hi.astype(F32)).astype(BF16)
    o_ref[0] = (jnp.dot(hi, gsum_ref[...], preferred_element_type=F32)
                + jnp.dot(lo, gsum_ref[...], preferred_element_type=F32))


def _peer_u(idxl, ovf, h_slab, group_sum, u_tab):
    T = h_slab.shape[0]
    block_rows = u_tab.shape[0] // N_TABLE_BLOCKS
    tm = 128
    return pl.pallas_call(
        _peer_u_kernel,
        grid=(N_TABLE_BLOCKS, T // tm),
        in_specs=[
            pl.BlockSpec((1, tm, PAIRS), lambda b, i: (b, i, 0), memory_space=pltpu.SMEM),
            pl.BlockSpec((N_TABLE_BLOCKS, tm), lambda b, i: (0, i), memory_space=pltpu.SMEM),
            pl.BlockSpec((tm, SLAB_ROWS, LANES), lambda b, i: (i, 0, 0)),
            _resident((PAIRS * SLAB_ROWS, PAIRS), lambda b, i: (0, 0)),
            _resident((block_rows, SLAB_ROWS, LANES), lambda b, i: (b, 0, 0)),
        ],
        out_specs=pl.BlockSpec((1, tm, PAIRS), lambda b, i: (b, i, 0)),
        out_shape=jax.ShapeDtypeStruct((N_TABLE_BLOCKS, T, PAIRS), F32),
        scratch_shapes=[pltpu.VMEM((tm, PAIRS * SLAB_ROWS), F32)],
        compiler_params=_cparams(("arbitrary", "arbitrary")),
        name="peer_expert_u",
    )(idxl, ovf, h_slab, group_sum, u_tab)


def _peer_v_kernel(idx_ref, ovf_ref, gate_ref, act_ref, ex_ref, v_ref, o_ref, wexp):
    tm = gate_ref.shape[1]
    b = pl.program_id(0)
    act = act_ref[0]
    gelu = 0.5 * act * (1.0 + lax.erf(act * np.float32(math.sqrt(0.5))))
    w = gate_ref[0] * gelu
    wexp[...] = jnp.dot(w.astype(BF16), ex_ref[...], preferred_element_type=F32)
    eye = _diag_mask()
    wcols = WINDOW_SLOTS * SLAB_ROWS

    def weighted(t, slots, lo, hi):
        vsel = _gather_slabs(v_ref, idx_ref, t, slots)
        wrow = jnp.broadcast_to(wexp[pl.ds(t, 1), lo:hi], (SLAB_ROWS, hi - lo))
        wbd = jnp.where(eye[:, lo:hi], wrow, 0.0).astype(BF16)
        return jnp.dot(wbd, vsel, preferred_element_type=F32)

    def tok(t, carry):
        o_ref[0, t] = weighted(t, range(WINDOW_SLOTS), 0, wcols)
        return carry

    def rare(t, carry):
        @pl.when(ovf_ref[b, t] != 0)
        def _():
            o_ref[0, t] = o_ref[0, t] + weighted(t, range(WINDOW_SLOTS, PAIRS), wcols, PAIRS * SLAB_ROWS)
        return carry

    lax.fori_loop(0, tm, tok, 0, unroll=TOKEN_UNROLL)
    lax.fori_loop(0, tm, rare, 0)


def _peer_v(idxl, ovf, gate, act, expand, v_tab):
    T = gate.shape[1]
    block_rows = v_tab.shape[0] // N_TABLE_BLOCKS
    tm = 128
    return pl.pallas_call(
        _peer_v_kernel,
        grid=(N_TABLE_BLOCKS, T // tm),
        in_specs=[
            pl.BlockSpec((1, tm, PAIRS), lambda b, i: (b, i, 0), memory_space=pltpu.SMEM),
            pl.BlockSpec((N_TABLE_BLOCKS, tm), lambda b, i: (0, i), memory_space=pltpu.SMEM),
            pl.BlockSpec((1, tm, PAIRS), lambda b, i: (b, i, 0)),
            pl.BlockSpec((1, tm, PAIRS), lambda b, i: (b, i, 0)),
            _resident((PAIRS, PAIRS * SLAB_ROWS), lambda b, i: (0, 0)),
            _resident((block_rows, SLAB_ROWS, LANES), lambda b, i: (b, 0, 0)),
        ],
        out_specs=pl.BlockSpec((1, tm, SLAB_ROWS, LANES), lambda b, i: (b, i, 0, 0)),
        out_shape=jax.ShapeDtypeStruct((N_TABLE_BLOCKS, T, SLAB_ROWS, LANES), F32),
        scratch_shapes=[pltpu.VMEM((tm, PAIRS * SLAB_ROWS), F32)],
        compiler_params=_cparams(("arbitrary", "arbitrary")),
        name="peer_expert_v",
    )(idxl, ovf, gate, act, expand, v_tab)


def _resid_kernel(final, x_ref, o_ref, g2_ref, fg_ref, y_ref):
    acc = o_ref[0]
    for k in range(1, N_TABLE_BLOCKS):
        acc = acc + o_ref[k]
    x = x_ref[...] + g2_ref[0] * acc
    if final:
        ms = jnp.mean(x * x, axis=-1, keepdims=True)
        x = (x * lax.rsqrt(ms + EPS)) * fg_ref[...]
    y_ref[...] = x


def _resid(x, peer_out, g2, final_g, seq, final):
    T, D = x.shape
    tm = 512
    per_seq = seq // tm
    return pl.pallas_call(
        functools.partial(_resid_kernel, final),
        grid=(T // tm,),
        in_specs=[
            pl.BlockSpec((tm, D), lambda i: (i, 0)),
            pl.BlockSpec((N_TABLE_BLOCKS, tm, D), lambda i: (0, i, 0)),
            pl.BlockSpec((1, 1, D), lambda i: (i // per_seq, 0, 0)),
            pl.BlockSpec((1, D), lambda i: (0, 0)),
        ],
        out_specs=pl.BlockSpec((tm, D), lambda i: (i, 0)),
        out_shape=jax.ShapeDtypeStruct((T, D), F32),
        compiler_params=_cparams(("parallel",)),
        name="peer_residual",
    )(x, peer_out, g2, final_g)


def _pair_expand_matrix():
    p = np.arange(PAIRS)[:, None]
    c = np.arange(PAIRS * SLAB_ROWS)[None, :]
    return (p == c // SLAB_ROWS).astype(np.float32)


def kernel(x, c, rel_bias, ada_w, ada_b, norm1_g, w_in, dw_w, dw_b, conv_ln_g, conv_ln_b, w_conv_out,
           attn_sinks, w_attn_out, w_out, norm2_g, w_pq, sub_keys, peer_u, peer_v, final_g):
    B, S, D = x.shape
    L = ada_w.shape[0]
    T = B * S
    C = dw_w.shape[2]
    A = N_HEADS * HEAD_DIM
    KV = N_KV_HEADS * HEAD_DIM
    E = peer_u.shape[1]
    assert S % 512 == 0 and D == SLAB_ROWS * LANES and E % N_TABLE_BLOCKS == 0

    o_a, o_b, o_q, o_k, o_v, o_g = 0, C, 2 * C, 2 * C + A, 2 * C + A + KV, 2 * C + A + 2 * KV
    order = np.concatenate([np.arange(o_a, o_a + 2 * C), np.arange(o_g, o_g + 2 * D),
                            np.arange(o_q, o_q + A + 2 * KV)])
    n_a, n_b, n_gc, n_ga, n_q, n_k, n_v = 0, C, 2 * C, 2 * C + D, 2 * C + 2 * D, 2 * C + 2 * D + A, 2 * C + 2 * D + A + KV

    c_pad = jnp.pad(c, ((0, 8 - B), (0, 0)))
    mod = _ada(c_pad, ada_w, ada_b)
    bias_tab = _rel_bias_table(rel_bias)
    expand = jnp.asarray(_pair_expand_matrix(), dtype=BF16)
    group_sum = jnp.asarray(_pair_expand_matrix().T, dtype=BF16)
    final_g2 = final_g.reshape(1, D)

    xt = x.reshape(T, D)
    for l in range(L):
        m = mod[l, :B].reshape(B, 6, 1, D)
        sh1, sc1, g1, sh2, sc2, g2 = (m[:, k] for k in range(6))

        w_in_l = w_in[l][:, order].astype(BF16)
        proj = _inproj(xt, norm1_g[l].reshape(1, D), sc1, sh1, w_in_l, S)
        cact = _conv(proj, dw_w[l], dw_b[l].reshape(1, C), conv_ln_g[l].reshape(1, C),
                     conv_ln_b[l].reshape(1, C), S, n_a // C, n_b // C)
        attn = _attn(proj, bias_tab, attn_sinks[l], S, n_q // A, n_k // KV, n_v // KV)
        xt = _mix(cact, attn, proj, xt, g1, w_conv_out[l].astype(BF16), w_attn_out[l].astype(BF16),
                  w_out[l].astype(BF16), S, n_gc // D, n_ga // D)

        keys = sub_keys[l].reshape(2 * PEER_HEADS, N_KEYS, PEER_HALF).astype(BF16)
        h2, st = _peer_q(xt, norm2_g[l].reshape(1, D), sc2, sh2, w_pq[l].astype(BF16), keys, S)
        idxl, gate, ovf = _route(st, E // N_TABLE_BLOCKS)
        u_tab = peer_u[l].astype(BF16).reshape(E, SLAB_ROWS, LANES)
        v_tab = peer_v[l].astype(BF16).reshape(E, SLAB_ROWS, LANES)
        act = _peer_u(idxl, ovf, h2.reshape(T, SLAB_ROWS, LANES), group_sum, u_tab)
        pout = _peer_v(idxl, ovf, gate, act, expand, v_tab)
        xt = _resid(xt, pout.reshape(N_TABLE_BLOCKS, T, D), g2, final_g2, S, l == L - 1)
    return xt.reshape(B, S, D)
```

```python
import functools
import math

import numpy as np
import jax
import jax.numpy as jnp
from jax import lax
from jax.experimental import pallas as pl
from jax.experimental.pallas import tpu as pltpu

F32 = jnp.float32
BF16 = jnp.bfloat16
I32 = jnp.int32

EPS = 1e-6
CONV_WIDTH = 31
HALO = 32
N_HEADS = 16
N_KV_HEADS = 4
HEAD_DIM = 64
BLOCK = 128
NUM_BUCKETS = 32
MAX_DISTANCE = 128
N_KEYS = 128
PEER_HEADS = 8
PEER_TOPK = 16
PEER_HALF = 128
PAIRS = PEER_HEADS * PEER_TOPK
SLAB_ROWS = 16
LANES = 128
N_TABLE_BLOCKS = 2
TOKEN_UNROLL = 4
WINDOW_SLOTS = 80
assert PAIRS & (PAIRS - 1) == 0 and WINDOW_SLOTS % 8 == 0 and 2 * WINDOW_SLOTS >= PAIRS
VMEM_LIMIT = 56 * 1024 * 1024


def _cparams(sem, vmem=VMEM_LIMIT):
    return pltpu.CompilerParams(dimension_semantics=sem, vmem_limit_bytes=vmem)


def _resident(block_shape, index_map):
    return pl.BlockSpec(block_shape, index_map, pipeline_mode=pl.Buffered(1))


def _ada_kernel(c_ref, w_ref, b_ref, o_ref):
    c = c_ref[...]
    cs = c * jax.nn.sigmoid(c)
    o_ref[0] = jnp.dot(cs.astype(BF16), w_ref[0].astype(BF16), preferred_element_type=F32) + b_ref[0]


def _ada(c_pad, ada_w, ada_b):
    L, D, N = ada_w.shape
    tn = 1024
    return pl.pallas_call(
        _ada_kernel,
        grid=(L, N // tn),
        in_specs=[
            pl.BlockSpec((8, D), lambda l, j: (0, 0)),
            pl.BlockSpec((1, D, tn), lambda l, j: (l, 0, j)),
            pl.BlockSpec((1, 1, tn), lambda l, j: (l, 0, j)),
        ],
        out_specs=pl.BlockSpec((1, 8, tn), lambda l, j: (l, 0, j)),
        out_shape=jax.ShapeDtypeStruct((L, 8, N), F32),
        compiler_params=_cparams(("parallel", "parallel")),
        name="ada_mod",
    )(c_pad, ada_w, ada_b.reshape(L, 1, N))


def _bias_kernel(bucket_ref, rb_ref, o_ref):
    h = pl.program_id(0)
    bucket = bucket_ref[...]
    acc = jnp.zeros(bucket.shape, F32)
    for b in range(NUM_BUCKETS):
        acc = jnp.where(bucket == b, rb_ref[b, h], acc)
    o_ref[0] = acc


def _t5_bucket_table():
    qi = np.arange(BLOCK)[:, None] + BLOCK
    kj = np.arange(2 * BLOCK)[None, :]
    d = np.maximum(qi - kj, 0)
    max_exact = NUM_BUCKETS // 2
    ratio = (np.log(np.maximum(d, 1).astype(np.float64) / max_exact)
             / math.log(MAX_DISTANCE / max_exact) * (NUM_BUCKETS - max_exact))
    large = np.minimum(max_exact + ratio.astype(np.int64), NUM_BUCKETS - 1)
    return np.where(d < max_exact, d, large).astype(np.int32)


def _rel_bias_table(rel_bias):
    bucket = jnp.asarray(_t5_bucket_table())
    return pl.pallas_call(
        _bias_kernel,
        grid=(N_HEADS,),
        in_specs=[
            pl.BlockSpec((BLOCK, 2 * BLOCK), lambda h: (0, 0)),
            pl.BlockSpec(memory_space=pltpu.SMEM),
        ],
        out_specs=pl.BlockSpec((1, BLOCK, 2 * BLOCK), lambda h: (h, 0, 0)),
        out_shape=jax.ShapeDtypeStruct((N_HEADS, BLOCK, 2 * BLOCK), F32),
        compiler_params=_cparams(("arbitrary",)),
        name="rel_bias_table",
    )(bucket, rel_bias)


def _norm_mod(x, g, sc, sh):
    ms = jnp.mean(x * x, axis=-1, keepdims=True)
    y = x * lax.rsqrt(ms + EPS)
    return (y * g) * (1.0 + sc) + sh


def _inproj_kernel(x_ref, g_ref, sc_ref, sh_ref, w_ref, o_ref, h_scr):
    @pl.when(pl.program_id(1) == 0)
    def _():
        h_scr[...] = _norm_mod(x_ref[...], g_ref[...], sc_ref[0], sh_ref[0]).astype(BF16)

    o_ref[...] = jnp.dot(h_scr[...], w_ref[...], preferred_element_type=F32)


def _inproj(x, g, sc, sh, w, seq):
    T, D = x.shape
    N = w.shape[1]
    tm, tn = 1024, 512
    per_seq = seq // tm
    return pl.pallas_call(
        _inproj_kernel,
        grid=(T // tm, N // tn),
        in_specs=[
            pl.BlockSpec((tm, D), lambda i, j: (i, 0)),
            pl.BlockSpec((1, D), lambda i, j: (0, 0)),
            pl.BlockSpec((1, 1, D), lambda i, j: (i // per_seq, 0, 0)),
            pl.BlockSpec((1, 1, D), lambda i, j: (i // per_seq, 0, 0)),
            pl.BlockSpec((D, tn), lambda i, j: (0, j)),
        ],
        out_specs=pl.BlockSpec((tm, tn), lambda i, j: (i, j)),
        out_shape=jax.ShapeDtypeStruct((T, N), F32),
        scratch_shapes=[pltpu.VMEM((tm, D), BF16)],
        compiler_params=_cparams(("parallel", "arbitrary")),
        name="in_proj",
    )(x, g, sc, sh, w)


def _conv_kernel(per_seq, a_ref, b_ref, ha_ref, hb_ref, dw_ref, db_ref, lg_ref, lb_ref, o_ref, ext):
    ts = a_ref.shape[0]
    first = (pl.program_id(0) % per_seq) == 0
    hu = ha_ref[...] * jax.nn.sigmoid(hb_ref[...])
    ext[0:HALO, :] = jnp.where(first, 0.0, hu)
    ext[HALO:, :] = a_ref[...] * jax.nn.sigmoid(b_ref[...])
    acc = jnp.zeros(a_ref.shape, F32)
    base = HALO - (CONV_WIDTH - 1)
    for w in range(CONV_WIDTH):
        acc = acc + ext[base + w:base + w + ts, :] * dw_ref[w:w + 1, :]
    acc = acc + db_ref[...]
    mu = jnp.mean(acc, axis=-1, keepdims=True)
    cen = acc - mu
    var = jnp.mean(cen * cen, axis=-1, keepdims=True)
    y = cen * lax.rsqrt(var + EPS) * lg_ref[...] + lb_ref[...]
    o_ref[...] = (y * jax.nn.sigmoid(y)).astype(BF16)


def _conv(proj, dw_w, dw_b, ln_g, ln_b, seq, a_col, b_col):
    T = proj.shape[0]
    C = dw_w.shape[1]
    ts = 256
    per_seq = seq // ts
    rb = ts // HALO
    halo_map = lambda col: (lambda i: (jnp.maximum(i * rb - 1, 0), col))
    return pl.pallas_call(
        functools.partial(_conv_kernel, per_seq),
        grid=(T // ts,),
        in_specs=[
            pl.BlockSpec((ts, C), lambda i: (i, a_col)),
            pl.BlockSpec((ts, C), lambda i: (i, b_col)),
            pl.BlockSpec((HALO, C), halo_map(a_col)),
            pl.BlockSpec((HALO, C), halo_map(b_col)),
            pl.BlockSpec((CONV_WIDTH, C), lambda i: (0, 0)),
            pl.BlockSpec((1, C), lambda i: (0, 0)),
            pl.BlockSpec((1, C), lambda i: (0, 0)),
            pl.BlockSpec((1, C), lambda i: (0, 0)),
        ],
        out_specs=pl.BlockSpec((ts, C), lambda i: (i, 0)),
        out_shape=jax.ShapeDtypeStruct((T, C), BF16),
        scratch_shapes=[pltpu.VMEM((HALO + ts, C), F32)],
        compiler_params=_cparams(("parallel",)),
        name="conv_branch",
    )(proj, proj, proj, proj, dw_w, dw_b, ln_g, ln_b)


def _attn_kernel(nb, q_ref, kc_ref, kp_ref, vc_ref, vp_ref, bias_ref, sink_ref, o_ref):
    first = (pl.program_id(0) % nb) == 0
    kk = jnp.concatenate([kp_ref[...], kc_ref[...]], axis=0).astype(BF16)
    vv = jnp.concatenate([vp_ref[...], vc_ref[...]], axis=0).astype(BF16)
    row = lax.broadcasted_iota(I32, (BLOCK, 2 * BLOCK), 0)
    col = lax.broadcasted_iota(I32, (BLOCK, 2 * BLOCK), 1)
    dist = row + BLOCK - col
    valid = (dist >= 0) & (dist < BLOCK) & (jnp.logical_not(first) | (col >= BLOCK))
    group = N_HEADS // N_KV_HEADS
    for kvh in range(N_KV_HEADS):
        k_h = kk[:, kvh * HEAD_DIM:(kvh + 1) * HEAD_DIM]
        v_h = vv[:, kvh * HEAD_DIM:(kvh + 1) * HEAD_DIM]
        qg = q_ref[:, kvh * group * HEAD_DIM:(kvh + 1) * group * HEAD_DIM].astype(BF16)
        outs = []
        for g in range(group):
            h = kvh * group + g
            q_h = qg[:, g * HEAD_DIM:(g + 1) * HEAD_DIM]
            s = lax.dot_general(q_h, k_h, (((1,), (1,)), ((), ())), preferred_element_type=F32)
            s = s * (HEAD_DIM ** -0.5) + bias_ref[h]
            s = jnp.where(valid, s, -jnp.inf)
            sink = sink_ref[h]
            m = jnp.maximum(jnp.max(s, axis=-1, keepdims=True), sink)
            e = jnp.exp(s - m)
            denom = jnp.sum(e, axis=-1, keepdims=True) + jnp.exp(sink - m)
            p = e / denom
            outs.append(jnp.dot(p.astype(BF16), v_h, preferred_element_type=F32))
        o_ref[:, kvh * group * HEAD_DIM:(kvh + 1) * group * HEAD_DIM] = (
            jnp.concatenate(outs, axis=1).astype(BF16))


def _attn(proj, bias_tab, sinks, seq, q_col, k_col, v_col):
    T = proj.shape[0]
    nb = seq // BLOCK
    attn_dim = N_HEADS * HEAD_DIM
    kv_dim = N_KV_HEADS * HEAD_DIM
    prev = lambda col: (lambda i: (jnp.where(i % nb == 0, i, i - 1), col))
    return pl.pallas_call(
        functools.partial(_attn_kernel, nb),
        grid=(T // BLOCK,),
        in_specs=[
            pl.BlockSpec((BLOCK, attn_dim), lambda i: (i, q_col)),
            pl.BlockSpec((BLOCK, kv_dim), lambda i: (i, k_col)),
            pl.BlockSpec((BLOCK, kv_dim), prev(k_col)),
            pl.BlockSpec((BLOCK, kv_dim), lambda i: (i, v_col)),
            pl.BlockSpec((BLOCK, kv_dim), prev(v_col)),
            pl.BlockSpec((N_HEADS, BLOCK, 2 * BLOCK), lambda i: (0, 0, 0)),
            pl.BlockSpec(memory_space=pltpu.SMEM),
        ],
        out_specs=pl.BlockSpec((BLOCK, attn_dim), lambda i: (i, 0)),
        out_shape=jax.ShapeDtypeStruct((T, attn_dim), BF16),
        compiler_params=_cparams(("parallel",)),
        name="swa_attention",
    )(proj, proj, proj, proj, proj, bias_tab, sinks)


def _mix_kernel(ca_ref, at_ref, gc_ref, ga_ref, x_ref, g1_ref, wc_ref, wa_ref, wo_ref, o_ref):
    yc = jnp.dot(ca_ref[...], wc_ref[...], preferred_element_type=F32)
    ya = jnp.dot(at_ref[...], wa_ref[...], preferred_element_type=F32)
    mixed = jax.nn.sigmoid(gc_ref[...]) * yc + jax.nn.sigmoid(ga_ref[...]) * ya
    y = jnp.dot(mixed.astype(BF16), wo_ref[...], preferred_element_type=F32)
    o_ref[...] = x_ref[...] + g1_ref[0] * y


def _mix(cact, attn, proj, x, g1, wc, wa, wo, seq, gc_col, ga_col):
    T, D = x.shape
    C = cact.shape[1]
    A = attn.shape[1]
    tm = 256
    per_seq = seq // tm
    return pl.pallas_call(
        _mix_kernel,
        grid=(T // tm,),
        in_specs=[
            pl.BlockSpec((tm, C), lambda i: (i, 0)),
            pl.BlockSpec((tm, A), lambda i: (i, 0)),
            pl.BlockSpec((tm, D), lambda i: (i, gc_col)),
            pl.BlockSpec((tm, D), lambda i: (i, ga_col)),
            pl.BlockSpec((tm, D), lambda i: (i, 0)),
            pl.BlockSpec((1, 1, D), lambda i: (i // per_seq, 0, 0)),
            _resident((C, D), lambda i: (0, 0)),
            _resident((A, D), lambda i: (0, 0)),
            _resident((D, D), lambda i: (0, 0)),
        ],
        out_specs=pl.BlockSpec((tm, D), lambda i: (i, 0)),
        out_shape=jax.ShapeDtypeStruct((T, D), F32),
        compiler_params=_cparams(("parallel",)),
        name="merge_out_proj",
    )(cact, attn, proj, proj, x, g1, wc, wa, wo)


def _peer_q_kernel(x_ref, g_ref, sc_ref, sh_ref, w_ref, k_ref, h_ref, st_ref):
    hb = _norm_mod(x_ref[...], g_ref[...], sc_ref[0], sh_ref[0]).astype(BF16)
    h_ref[...] = hb
    qb = jnp.dot(hb, w_ref[...], preferred_element_type=F32).astype(BF16)
    for hp in range(2 * PEER_HEADS):
        q_hp = qb[:, hp * PEER_HALF:(hp + 1) * PEER_HALF]
        st_ref[hp * N_KEYS:(hp + 1) * N_KEYS, :] = lax.dot_general(
            k_ref[hp], q_hp, (((1,), (1,)), ((), ())), preferred_element_type=F32)


def _peer_q(x, g, sc, sh, w_pq, keys, seq):
    T, D = x.shape
    Q = w_pq.shape[1]
    tm = 256
    per_seq = seq // tm
    n_rows = 2 * PEER_HEADS * N_KEYS
    return pl.pallas_call(
        _peer_q_kernel,
        grid=(T // tm,),
        in_specs=[
            pl.BlockSpec((tm, D), lambda i: (i, 0)),
            pl.BlockSpec((1, D), lambda i: (0, 0)),
            pl.BlockSpec((1, 1, D), lambda i: (i // per_seq, 0, 0)),
            pl.BlockSpec((1, 1, D), lambda i: (i // per_seq, 0, 0)),
            _resident((D, Q), lambda i: (0, 0)),
            _resident((2 * PEER_HEADS, N_KEYS, PEER_HALF), lambda i: (0, 0, 0)),
        ],
        out_specs=[
            pl.BlockSpec((tm, D), lambda i: (i, 0)),
            pl.BlockSpec((n_rows, tm), lambda i: (0, i)),
        ],
        out_shape=[
            jax.ShapeDtypeStruct((T, D), BF16),
            jax.ShapeDtypeStruct((n_rows, T), F32),
        ],
        compiler_params=_cparams(("parallel",)),
        name="peer_query_scores",
    )(x, g, sc, sh, w_pq, keys)


def _top_rows(x, payload, k, rank=None):
    rows = lax.broadcasted_iota(I32, x.shape, 0) if rank is None else rank
    vals, pays = [], []
    for _ in range(k):
        m = jnp.max(x, axis=0, keepdims=True)
        first = jnp.min(jnp.where(x == m, rows, jnp.iinfo(jnp.int32).max), axis=0, keepdims=True)
        onehot = rows == first
        if payload is None:
            pays.append(first)
        else:
            pays.append(jnp.sum(jnp.where(onehot, payload, 0), axis=0, keepdims=True))
        vals.append(m)
        x = jnp.where(onehot, -jnp.inf, x)
    return jnp.concatenate(vals, axis=0), jnp.concatenate(pays, axis=0)


def _sort_rows(keys, pays):
    n_groups = len(keys)
    n_rows = 8 * n_groups
    sub = lax.broadcasted_iota(I32, keys[0].shape, 0)
    k = 2
    while k <= n_rows:
        j = k // 2
        while j >= 1:
            if j >= 8:
                gj = j // 8
                for lo in range(n_groups):
                    if lo & gj:
                        continue
                    hi = lo | gj
                    ascending = ((8 * lo) & k) == 0
                    a, b = keys[lo], keys[hi]
                    swap = (a > b) if ascending else (a < b)
                    keys[lo], keys[hi] = jnp.where(swap, b, a), jnp.where(swap, a, b)
                    pa, pb = pays[lo], pays[hi]
                    pays[lo], pays[hi] = jnp.where(swap, pb, pa), jnp.where(swap, pa, pb)
            else:
                lower = (sub & j) == 0
                upper = (sub & j) != 0
                for g in range(n_groups):
                    if k >= 8:
                        ascending = ((8 * g) & k) == 0
                        want_max = upper if ascending else lower
                    else:
                        want_max = jnp.logical_xor(upper, (sub & k) != 0)
                    x, p = keys[g], pays[g]
                    xp = jnp.where(lower, pltpu.roll(x, 8 - j, 0), pltpu.roll(x, j, 0))
                    pp = jnp.where(lower, pltpu.roll(p, 8 - j, 0), pltpu.roll(p, j, 0))
                    swap = jnp.logical_xor(xp < x, want_max)
                    keys[g] = jnp.where(swap, xp, x)
                    pays[g] = jnp.where(swap, pp, p)
            j //= 2
        k *= 2
    return keys, pays


def _route_kernel(block_rows, st_ref, idxl_ref, gate_ref, ovf_ref, idx_scr, gate_scr):
    def head(h, carry):
        base = pl.multiple_of(h * (2 * N_KEYS), 2 * N_KEYS)
        v0, i0 = _top_rows(st_ref[pl.ds(base, N_KEYS), :], None, PEER_TOPK)
        v1, i1 = _top_rows(st_ref[pl.ds(base + N_KEYS, N_KEYS), :], None, PEER_TOPK)
        half = PEER_TOPK // 2
        sub = lax.broadcasted_iota(I32, (half, v0.shape[1]), 0)
        cand = [v0[k:k + 1] + v1[:half] for k in range(half)]
        cidx = [i0[k:k + 1] * N_KEYS + i1[:half] for k in range(half)]
        flat = [sub + k * PEER_TOPK for k in range(half)]
        cand += [v0[:1] + v1[half:], v0[half:] + v1[:1]]
        cidx += [i0[:1] * N_KEYS + i1[half:], i0[half:] * N_KEYS + i1[:1]]
        flat += [sub + half, (sub + half) * PEER_TOPK]
        best, eid = _top_rows(jnp.concatenate(cand, axis=0), jnp.concatenate(cidx, axis=0), PEER_TOPK,
                              rank=jnp.concatenate(flat, axis=0))
        e = jnp.exp(best - jnp.max(best, axis=0, keepdims=True))
        gates = e / jnp.sum(e, axis=0, keepdims=True)
        off = pl.multiple_of(h * PEER_TOPK, PEER_TOPK)
        idx_scr[pl.ds(off, PEER_TOPK), :] = eid
        gate_scr[pl.ds(off, PEER_TOPK), :] = gates
        return carry

    lax.fori_loop(0, PEER_HEADS, head, 0)

    n_groups = PAIRS // 8
    sub = lax.broadcasted_iota(I32, (8, idx_scr.shape[1]), 0)
    keys = [idx_scr[8 * g:8 * g + 8, :] * PAIRS + (sub + 8 * g) for g in range(n_groups)]
    pays = [gate_scr[8 * g:8 * g + 8, :] for g in range(n_groups)]
    keys, pays = _sort_rows(keys, pays)
    eids = [jnp.right_shift(k, PAIRS.bit_length() - 1) for k in keys]
    row_hi, row_lo = WINDOW_SLOTS, PAIRS - WINDOW_SLOTS - 1
    ovf0 = jnp.where(eids[row_hi // 8][row_hi % 8:row_hi % 8 + 1, :] < block_rows, 1, 0)
    ovf1 = jnp.where(eids[row_lo // 8][row_lo % 8:row_lo % 8 + 1, :] >= block_rows, 1, 0)
    ovf_ref[...] = jnp.concatenate([ovf0, ovf1], axis=0).astype(I32)
    rot = (PAIRS - WINDOW_SLOTS) // 8
    for b in range(N_TABLE_BLOCKS):
        order = [(g + b * rot) % n_groups for g in range(n_groups)]
        e = jnp.concatenate([eids[g] for g in order], axis=0)
        inb = (e >= b * block_rows) & (e < (b + 1) * block_rows)
        local = jnp.where(inb, e - b * block_rows, 0)
        gate = jnp.where(inb, jnp.concatenate([pays[g] for g in order], axis=0), 0.0)
        idxl_ref[b] = local.astype(F32).T.astype(I32)
        gate_ref[b] = gate.T


def _route(st, block_rows):
    n_rows, T = st.shape
    tr = 128
    return pl.pallas_call(
        functools.partial(_route_kernel, block_rows),
        grid=(T // tr,),
        in_specs=[pl.BlockSpec((n_rows, tr), lambda i: (0, i))],
        out_specs=[
            pl.BlockSpec((N_TABLE_BLOCKS, tr, PAIRS), lambda i: (0, i, 0)),
            pl.BlockSpec((N_TABLE_BLOCKS, tr, PAIRS), lambda i: (0, i, 0)),
            pl.BlockSpec((N_TABLE_BLOCKS, tr), lambda i: (0, i)),
        ],
        out_shape=[
            jax.ShapeDtypeStruct((N_TABLE_BLOCKS, T, PAIRS), I32),
            jax.ShapeDtypeStruct((N_TABLE_BLOCKS, T, PAIRS), F32),
            jax.ShapeDtypeStruct((N_TABLE_BLOCKS, T), I32),
        ],
        scratch_shapes=[pltpu.VMEM((PAIRS, tr), I32), pltpu.VMEM((PAIRS, tr), F32)],
        compiler_params=_cparams(("parallel",)),
        name="peer_route",
    )(st)


def _diag_mask():
    r = lax.broadcasted_iota(I32, (SLAB_ROWS, PAIRS * SLAB_ROWS), 0)
    c = lax.broadcasted_iota(I32, (SLAB_ROWS, PAIRS * SLAB_ROWS), 1)
    return r == (c % SLAB_ROWS)


def _for_overflow_tokens(ovf_ref, b, tm, body):
    group = 8

    def step(g, carry):
        flags = [ovf_ref[b, g * group + i] for i in range(group)]
        any_set = functools.reduce(lambda x, y: x | y, flags)

        @pl.when(any_set != 0)
        def _():
            def one(i, c):
                t = g * group + i

                @pl.when(ovf_ref[b, t] != 0)
                def _():
                    body(t)
                return c

            lax.fori_loop(0, group, one, 0)
        return carry

    lax.fori_loop(0, tm // group, step, 0)


def _gather_slabs(tab_ref, idx_ref, t, slots):
    return jnp.concatenate([tab_ref[idx_ref[0, t, j]] for j in slots], axis=0)


def _peer_u_kernel(idx_ref, ovf_ref, h_ref, gsum_ref, u_ref, o_ref, diag_scr):
    tm = h_ref.shape[0]
    b = pl.program_id(0)
    eye = _diag_mask()
    wcols = WINDOW_SLOTS * SLAB_ROWS

    def diag(t, slots, lo, hi):
        usel = _gather_slabs(u_ref, idx_ref, t, slots)
        r = lax.dot_general(h_ref[t], usel, (((1,), (1,)), ((), ())), preferred_element_type=F32)
        diag_scr[pl.ds(t, 1), lo:hi] = jnp.sum(jnp.where(eye[:, lo:hi], r, 0.0), axis=0, keepdims=True)

    def tok(t, carry):
        diag(t, range(WINDOW_SLOTS), 0, wcols)
        return carry

    def rare(t):
        diag(t, range(WINDOW_SLOTS, PAIRS), wcols, PAIRS * SLAB_ROWS)

    diag_scr[:, wcols:] = jnp.zeros((tm, PAIRS * SLAB_ROWS - wcols), F32)
    lax.fori_loop(0, tm, tok, 0, unroll=TOKEN_UNROLL)
    _for_overflow_tokens(ovf_ref, b, tm, rare)
    d = diag_scr[...]
    hi = d.astype(BF16)
    lo = (d - hi.astype(F32)).astype(BF16)
    o_ref[0] = (jnp.dot(hi, gsum_ref[...], preferred_element_type=F32)
                + jnp.dot(lo, gsum_ref[...], preferred_element_type=F32))


def _peer_u(idxl, ovf, h_slab, group_sum, u_tab):
    T = h_slab.shape[0]
    block_rows = u_tab.shape[0] // N_TABLE_BLOCKS
    tm = 128
    return pl.pallas_call(
        _peer_u_kernel,
        grid=(N_TABLE_BLOCKS, T // tm),
        in_specs=[
            pl.BlockSpec((1, tm, PAIRS), lambda b, i: (b, i, 0), memory_space=pltpu.SMEM),
            pl.BlockSpec((N_TABLE_BLOCKS, tm), lambda b, i: (0, i), memory_space=pltpu.SMEM),
            pl.BlockSpec((tm, SLAB_ROWS, LANES), lambda b, i: (i, 0, 0)),
            _resident((PAIRS * SLAB_ROWS, PAIRS), lambda b, i: (0, 0)),
            _resident((block_rows, SLAB_ROWS, LANES), lambda b, i: (b, 0, 0)),
        ],
        out_specs=pl.BlockSpec((1, tm, PAIRS), lambda b, i: (b, i, 0)),
        out_shape=jax.ShapeDtypeStruct((N_TABLE_BLOCKS, T, PAIRS), F32),
        scratch_shapes=[pltpu.VMEM((tm, PAIRS * SLAB_ROWS), F32)],
        compiler_params=_cparams(("arbitrary", "arbitrary")),
        name="peer_expert_u",
    )(idxl, ovf, h_slab, group_sum, u_tab)


def _peer_v_kernel(idx_ref, ovf_ref, gate_ref, act_ref, ex_ref, v_ref, o_ref, wexp):
    tm = gate_ref.shape[1]
    b = pl.program_id(0)
    act = act_ref[0]
    gelu = 0.5 * act * (1.0 + lax.erf(act * np.float32(math.sqrt(0.5))))
    w = gate_ref[0] * gelu
    wexp[...] = jnp.dot(w.astype(BF16), ex_ref[...], preferred_element_type=F32)
    eye = _diag_mask()
    wcols = WINDOW_SLOTS * SLAB_ROWS

    def weighted(t, slots, lo, hi):
        vsel = _gather_slabs(v_ref, idx_ref, t, slots)
        wrow = jnp.broadcast_to(wexp[pl.ds(t, 1), lo:hi], (SLAB_ROWS, hi - lo))
        wbd = jnp.where(eye[:, lo:hi], wrow, 0.0).astype(BF16)
        return jnp.dot(wbd, vsel, preferred_element_type=F32)

    def tok(t, carry):
        o_ref[0, t] = weighted(t, range(WINDOW_SLOTS), 0, wcols)
        return carry

    def rare(t):
        o_ref[0, t] = o_ref[0, t] + weighted(t, range(WINDOW_SLOTS, PAIRS), wcols, PAIRS * SLAB_ROWS)

    lax.fori_loop(0, tm, tok, 0, unroll=TOKEN_UNROLL)
    _for_overflow_tokens(ovf_ref, b, tm, rare)


def _peer_v(idxl, ovf, gate, act, expand, v_tab):
    T = gate.shape[1]
    block_rows = v_tab.shape[0] // N_TABLE_BLOCKS
    tm = 128
    return pl.pallas_call(
        _peer_v_kernel,
        grid=(N_TABLE_BLOCKS, T // tm),
        in_specs=[
            pl.BlockSpec((1, tm, PAIRS), lambda b, i: (b, i, 0), memory_space=pltpu.SMEM),
            pl.BlockSpec((N_TABLE_BLOCKS, tm), lambda b, i: (0, i), memory_space=pltpu.SMEM),
            pl.BlockSpec((1, tm, PAIRS), lambda b, i: (b, i, 0)),
            pl.BlockSpec((1, tm, PAIRS), lambda b, i: (b, i, 0)),
            _resident((PAIRS, PAIRS * SLAB_ROWS), lambda b, i: (0, 0)),
            _resident((block_rows, SLAB_ROWS, LANES), lambda b, i: (b, 0, 0)),
        ],
        out_specs=pl.BlockSpec((1, tm, SLAB_ROWS, LANES), lambda b, i: (b, i, 0, 0)),
        out_shape=jax.ShapeDtypeStruct((N_TABLE_BLOCKS, T, SLAB_ROWS, LANES), F32),
        scratch_shapes=[pltpu.VMEM((tm, PAIRS * SLAB_ROWS), F32)],
        compiler_params=_cparams(("arbitrary", "arbitrary")),
        name="peer_expert_v",
    )(idxl, ovf, gate, act, expand, v_tab)


def _resid_kernel(final, x_ref, o_ref, g2_ref, fg_ref, y_ref):
    acc = o_ref[0]
    for k in range(1, N_TABLE_BLOCKS):
        acc = acc + o_ref[k]
    x = x_ref[...] + g2_ref[0] * acc
    if final:
        ms = jnp.mean(x * x, axis=-1, keepdims=True)
        x = (x * lax.rsqrt(ms + EPS)) * fg_ref[...]
    y_ref[...] = x


def _resid(x, peer_out, g2, final_g, seq, final):
    T, D = x.shape
    tm = 512
    per_seq = seq // tm
    return pl.pallas_call(
        functools.partial(_resid_kernel, final),
        grid=(T // tm,),
        in_specs=[
            pl.BlockSpec((tm, D), lambda i: (i, 0)),
            pl.BlockSpec((N_TABLE_BLOCKS, tm, D), lambda i: (0, i, 0)),
            pl.BlockSpec((1, 1, D), lambda i: (i // per_seq, 0, 0)),
            pl.BlockSpec((1, D), lambda i: (0, 0)),
        ],
        out_specs=pl.BlockSpec((tm, D), lambda i: (i, 0)),
        out_shape=jax.ShapeDtypeStruct((T, D), F32),
        compiler_params=_cparams(("parallel",)),
        name="peer_residual",
    )(x, peer_out, g2, final_g)


def _pair_expand_matrix():
    p = np.arange(PAIRS)[:, None]
    c = np.arange(PAIRS * SLAB_ROWS)[None, :]
    return (p == c // SLAB_ROWS).astype(np.float32)


def kernel(x, c, rel_bias, ada_w, ada_b, norm1_g, w_in, dw_w, dw_b, conv_ln_g, conv_ln_b, w_conv_out,
           attn_sinks, w_attn_out, w_out, norm2_g, w_pq, sub_keys, peer_u, peer_v, final_g):
    B, S, D = x.shape
    L = ada_w.shape[0]
    T = B * S
    C = dw_w.shape[2]
    A = N_HEADS * HEAD_DIM
    KV = N_KV_HEADS * HEAD_DIM
    E = peer_u.shape[1]
    assert S % 1024 == 0 and D == SLAB_ROWS * LANES and E % N_TABLE_BLOCKS == 0

    o_a, o_b, o_q, o_k, o_v, o_g = 0, C, 2 * C, 2 * C + A, 2 * C + A + KV, 2 * C + A + 2 * KV
    order = np.concatenate([np.arange(o_a, o_a + 2 * C), np.arange(o_g, o_g + 2 * D),
                            np.arange(o_q, o_q + A + 2 * KV)])
    n_a, n_b, n_gc, n_ga, n_q, n_k, n_v = 0, C, 2 * C, 2 * C + D, 2 * C + 2 * D, 2 * C + 2 * D + A, 2 * C + 2 * D + A + KV

    c_pad = jnp.pad(c, ((0, 8 - B), (0, 0)))
    mod = _ada(c_pad, ada_w, ada_b)
    bias_tab = _rel_bias_table(rel_bias)
    expand = jnp.asarray(_pair_expand_matrix(), dtype=BF16)
    group_sum = jnp.asarray(_pair_expand_matrix().T, dtype=BF16)
    final_g2 = final_g.reshape(1, D)

    xt = x.reshape(T, D)
    for l in range(L):
        m = mod[l, :B].reshape(B, 6, 1, D)
        sh1, sc1, g1, sh2, sc2, g2 = (m[:, k] for k in range(6))

        w_in_l = w_in[l][:, order].astype(BF16)
        proj = _inproj(xt, norm1_g[l].reshape(1, D), sc1, sh1, w_in_l, S)
        cact = _conv(proj, dw_w[l], dw_b[l].reshape(1, C), conv_ln_g[l].reshape(1, C),
                     conv_ln_b[l].reshape(1, C), S, n_a // C, n_b // C)
        attn = _attn(proj, bias_tab, attn_sinks[l], S, n_q // A, n_k // KV, n_v // KV)
        xt = _mix(cact, attn, proj, xt, g1, w_conv_out[l].astype(BF16), w_attn_out[l].astype(BF16),
                  w_out[l].astype(BF16), S, n_gc // D, n_ga // D)

        keys = sub_keys[l].reshape(2 * PEER_HEADS, N_KEYS, PEER_HALF).astype(BF16)
        h2, st = _peer_q(xt, norm2_g[l].reshape(1, D), sc2, sh2, w_pq[l].astype(BF16), keys, S)
        idxl, gate, ovf = _route(st, E // N_TABLE_BLOCKS)
        u_tab = peer_u[l].reshape(E, SLAB_ROWS, LANES).astype(BF16)
        v_tab = peer_v[l].reshape(E, SLAB_ROWS, LANES).astype(BF16)
        act = _peer_u(idxl, ovf, h2.reshape(T, SLAB_ROWS, LANES), group_sum, u_tab)
        pout = _peer_v(idxl, ovf, gate, act, expand, v_tab)
        xt = _resid(xt, pout.reshape(N_TABLE_BLOCKS, T, D), g2, final_g2, S, l == L - 1)
    return xt.reshape(B, S, D)
```

```python
import functools
import math

import numpy as np
import jax
import jax.numpy as jnp
from jax import lax
from jax.experimental import pallas as pl
from jax.experimental.pallas import tpu as pltpu

F32 = jnp.float32
BF16 = jnp.bfloat16
I32 = jnp.int32

EPS = 1e-6
CONV_WIDTH = 31
HALO = 32
N_HEADS = 16
N_KV_HEADS = 4
HEAD_DIM = 64
BLOCK = 128
NUM_BUCKETS = 32
MAX_DISTANCE = 128
N_KEYS = 128
PEER_HEADS = 8
PEER_TOPK = 16
PEER_HALF = 128
PAIRS = PEER_HEADS * PEER_TOPK
SLAB_ROWS = 16
LANES = 128
N_TABLE_BLOCKS = 2
TOKEN_UNROLL = 16
WINDOW_SLOTS = 80
assert PAIRS & (PAIRS - 1) == 0 and WINDOW_SLOTS % 8 == 0 and 2 * WINDOW_SLOTS >= PAIRS
VMEM_LIMIT = 56 * 1024 * 1024


def _cparams(sem, vmem=VMEM_LIMIT):
    return pltpu.CompilerParams(dimension_semantics=sem, vmem_limit_bytes=vmem)


def _resident(block_shape, index_map):
    return pl.BlockSpec(block_shape, index_map, pipeline_mode=pl.Buffered(1))


def _ada_kernel(c_ref, w_ref, b_ref, o_ref):
    c = c_ref[...]
    cs = c * jax.nn.sigmoid(c)
    o_ref[0] = jnp.dot(cs.astype(BF16), w_ref[0].astype(BF16), preferred_element_type=F32) + b_ref[0]


def _ada(c_pad, ada_w, ada_b):
    L, D, N = ada_w.shape
    tn = 1024
    return pl.pallas_call(
        _ada_kernel,
        grid=(L, N // tn),
        in_specs=[
            pl.BlockSpec((8, D), lambda l, j: (0, 0)),
            pl.BlockSpec((1, D, tn), lambda l, j: (l, 0, j)),
            pl.BlockSpec((1, 1, tn), lambda l, j: (l, 0, j)),
        ],
        out_specs=pl.BlockSpec((1, 8, tn), lambda l, j: (l, 0, j)),
        out_shape=jax.ShapeDtypeStruct((L, 8, N), F32),
        compiler_params=_cparams(("parallel", "parallel")),
        name="ada_mod",
    )(c_pad, ada_w, ada_b.reshape(L, 1, N))


def _bias_kernel(bucket_ref, rb_ref, o_ref):
    h = pl.program_id(0)
    bucket = bucket_ref[...]
    acc = jnp.zeros(bucket.shape, F32)
    for b in range(NUM_BUCKETS):
        acc = jnp.where(bucket == b, rb_ref[b, h], acc)
    o_ref[0] = acc


def _t5_bucket_table():
    qi = np.arange(BLOCK)[:, None] + BLOCK
    kj = np.arange(2 * BLOCK)[None, :]
    d = np.maximum(qi - kj, 0)
    max_exact = NUM_BUCKETS // 2
    ratio = (np.log(np.maximum(d, 1).astype(np.float64) / max_exact)
             / math.log(MAX_DISTANCE / max_exact) * (NUM_BUCKETS - max_exact))
    large = np.minimum(max_exact + ratio.astype(np.int64), NUM_BUCKETS - 1)
    return np.where(d < max_exact, d, large).astype(np.int32)


def _rel_bias_table(rel_bias):
    bucket = jnp.asarray(_t5_bucket_table())
    return pl.pallas_call(
        _bias_kernel,
        grid=(N_HEADS,),
        in_specs=[
            pl.BlockSpec((BLOCK, 2 * BLOCK), lambda h: (0, 0)),
            pl.BlockSpec(memory_space=pltpu.SMEM),
        ],
        out_specs=pl.BlockSpec((1, BLOCK, 2 * BLOCK), lambda h: (h, 0, 0)),
        out_shape=jax.ShapeDtypeStruct((N_HEADS, BLOCK, 2 * BLOCK), F32),
        compiler_params=_cparams(("arbitrary",)),
        name="rel_bias_table",
    )(bucket, rel_bias)


def _norm_mod(x, g, sc, sh):
    ms = jnp.mean(x * x, axis=-1, keepdims=True)
    y = x * lax.rsqrt(ms + EPS)
    return (y * g) * (1.0 + sc) + sh


def _inproj_kernel(x_ref, g_ref, sc_ref, sh_ref, w_ref, o_ref, h_scr):
    @pl.when(pl.program_id(1) == 0)
    def _():
        h_scr[...] = _norm_mod(x_ref[...], g_ref[...], sc_ref[0], sh_ref[0]).astype(BF16)

    o_ref[...] = jnp.dot(h_scr[...], w_ref[...], preferred_element_type=F32)


def _inproj(x, g, sc, sh, w, seq):
    T, D = x.shape
    N = w.shape[1]
    tm, tn = 1024, 512
    per_seq = seq // tm
    return pl.pallas_call(
        _inproj_kernel,
        grid=(T // tm, N // tn),
        in_specs=[
            pl.BlockSpec((tm, D), lambda i, j: (i, 0)),
            pl.BlockSpec((1, D), lambda i, j: (0, 0)),
            pl.BlockSpec((1, 1, D), lambda i, j: (i // per_seq, 0, 0)),
            pl.BlockSpec((1, 1, D), lambda i, j: (i // per_seq, 0, 0)),
            pl.BlockSpec((D, tn), lambda i, j: (0, j)),
        ],
        out_specs=pl.BlockSpec((tm, tn), lambda i, j: (i, j)),
        out_shape=jax.ShapeDtypeStruct((T, N), F32),
        scratch_shapes=[pltpu.VMEM((tm, D), BF16)],
        compiler_params=_cparams(("parallel", "arbitrary")),
        name="in_proj",
    )(x, g, sc, sh, w)


def _conv_kernel(per_seq, a_ref, b_ref, ha_ref, hb_ref, dw_ref, db_ref, lg_ref, lb_ref, o_ref, ext):
    ts = a_ref.shape[0]
    first = (pl.program_id(0) % per_seq) == 0
    hu = ha_ref[...] * jax.nn.sigmoid(hb_ref[...])
    ext[0:HALO, :] = jnp.where(first, 0.0, hu)
    ext[HALO:, :] = a_ref[...] * jax.nn.sigmoid(b_ref[...])
    acc = jnp.zeros(a_ref.shape, F32)
    base = HALO - (CONV_WIDTH - 1)
    for w in range(CONV_WIDTH):
        acc = acc + ext[base + w:base + w + ts, :] * dw_ref[w:w + 1, :]
    acc = acc + db_ref[...]
    mu = jnp.mean(acc, axis=-1, keepdims=True)
    cen = acc - mu
    var = jnp.mean(cen * cen, axis=-1, keepdims=True)
    y = cen * lax.rsqrt(var + EPS) * lg_ref[...] + lb_ref[...]
    o_ref[...] = (y * jax.nn.sigmoid(y)).astype(BF16)


def _conv(proj, dw_w, dw_b, ln_g, ln_b, seq, a_col, b_col):
    T = proj.shape[0]
    C = dw_w.shape[1]
    ts = 256
    per_seq = seq // ts
    rb = ts // HALO
    halo_map = lambda col: (lambda i: (jnp.maximum(i * rb - 1, 0), col))
    return pl.pallas_call(
        functools.partial(_conv_kernel, per_seq),
        grid=(T // ts,),
        in_specs=[
            pl.BlockSpec((ts, C), lambda i: (i, a_col)),
            pl.BlockSpec((ts, C), lambda i: (i, b_col)),
            pl.BlockSpec((HALO, C), halo_map(a_col)),
            pl.BlockSpec((HALO, C), halo_map(b_col)),
            pl.BlockSpec((CONV_WIDTH, C), lambda i: (0, 0)),
            pl.BlockSpec((1, C), lambda i: (0, 0)),
            pl.BlockSpec((1, C), lambda i: (0, 0)),
            pl.BlockSpec((1, C), lambda i: (0, 0)),
        ],
        out_specs=pl.BlockSpec((ts, C), lambda i: (i, 0)),
        out_shape=jax.ShapeDtypeStruct((T, C), BF16),
        scratch_shapes=[pltpu.VMEM((HALO + ts, C), F32)],
        compiler_params=_cparams(("parallel",)),
        name="conv_branch",
    )(proj, proj, proj, proj, dw_w, dw_b, ln_g, ln_b)


def _attn_kernel(nb, q_ref, kc_ref, kp_ref, vc_ref, vp_ref, bias_ref, sink_ref, o_ref):
    first = (pl.program_id(0) % nb) == 0
    kk = jnp.concatenate([kp_ref[...], kc_ref[...]], axis=0).astype(BF16)
    vv = jnp.concatenate([vp_ref[...], vc_ref[...]], axis=0).astype(BF16)
    row = lax.broadcasted_iota(I32, (BLOCK, 2 * BLOCK), 0)
    col = lax.broadcasted_iota(I32, (BLOCK, 2 * BLOCK), 1)
    dist = row + BLOCK - col
    valid = (dist >= 0) & (dist < BLOCK) & (jnp.logical_not(first) | (col >= BLOCK))
    group = N_HEADS // N_KV_HEADS
    for kvh in range(N_KV_HEADS):
        k_h = kk[:, kvh * HEAD_DIM:(kvh + 1) * HEAD_DIM]
        v_h = vv[:, kvh * HEAD_DIM:(kvh + 1) * HEAD_DIM]
        qg = q_ref[:, kvh * group * HEAD_DIM:(kvh + 1) * group * HEAD_DIM].astype(BF16)
        outs = []
        for g in range(group):
            h = kvh * group + g
            q_h = qg[:, g * HEAD_DIM:(g + 1) * HEAD_DIM]
            s = lax.dot_general(q_h, k_h, (((1,), (1,)), ((), ())), preferred_element_type=F32)
            s = s * (HEAD_DIM ** -0.5) + bias_ref[h]
            s = jnp.where(valid, s, -jnp.inf)
            sink = sink_ref[h]
            m = jnp.maximum(jnp.max(s, axis=-1, keepdims=True), sink)
            e = jnp.exp(s - m)
            denom = jnp.sum(e, axis=-1, keepdims=True) + jnp.exp(sink - m)
            p = e / denom
            outs.append(jnp.dot(p.astype(BF16), v_h, preferred_element_type=F32))
        o_ref[:, kvh * group * HEAD_DIM:(kvh + 1) * group * HEAD_DIM] = (
            jnp.concatenate(outs, axis=1).astype(BF16))


def _attn(proj, bias_tab, sinks, seq, q_col, k_col, v_col):
    T = proj.shape[0]
    nb = seq // BLOCK
    attn_dim = N_HEADS * HEAD_DIM
    kv_dim = N_KV_HEADS * HEAD_DIM
    prev = lambda col: (lambda i: (jnp.where(i % nb == 0, i, i - 1), col))
    return pl.pallas_call(
        functools.partial(_attn_kernel, nb),
        grid=(T // BLOCK,),
        in_specs=[
            pl.BlockSpec((BLOCK, attn_dim), lambda i: (i, q_col)),
            pl.BlockSpec((BLOCK, kv_dim), lambda i: (i, k_col)),
            pl.BlockSpec((BLOCK, kv_dim), prev(k_col)),
            pl.BlockSpec((BLOCK, kv_dim), lambda i: (i, v_col)),
            pl.BlockSpec((BLOCK, kv_dim), prev(v_col)),
            pl.BlockSpec((N_HEADS, BLOCK, 2 * BLOCK), lambda i: (0, 0, 0)),
            pl.BlockSpec(memory_space=pltpu.SMEM),
        ],
        out_specs=pl.BlockSpec((BLOCK, attn_dim), lambda i: (i, 0)),
        out_shape=jax.ShapeDtypeStruct((T, attn_dim), BF16),
        compiler_params=_cparams(("parallel",)),
        name="swa_attention",
    )(proj, proj, proj, proj, proj, bias_tab, sinks)


def _mix_kernel(ca_ref, at_ref, gc_ref, ga_ref, x_ref, g1_ref, wc_ref, wa_ref, wo_ref, o_ref):
    yc = jnp.dot(ca_ref[...], wc_ref[...], preferred_element_type=F32)
    ya = jnp.dot(at_ref[...], wa_ref[...], preferred_element_type=F32)
    mixed = jax.nn.sigmoid(gc_ref[...]) * yc + jax.nn.sigmoid(ga_ref[...]) * ya
    y = jnp.dot(mixed.astype(BF16), wo_ref[...], preferred_element_type=F32)
    o_ref[...] = x_ref[...] + g1_ref[0] * y


def _mix(cact, attn, proj, x, g1, wc, wa, wo, seq, gc_col, ga_col):
    T, D = x.shape
    C = cact.shape[1]
    A = attn.shape[1]
    tm = 256
    per_seq = seq // tm
    return pl.pallas_call(
        _mix_kernel,
        grid=(T // tm,),
        in_specs=[
            pl.BlockSpec((tm, C), lambda i: (i, 0)),
            pl.BlockSpec((tm, A), lambda i: (i, 0)),
            pl.BlockSpec((tm, D), lambda i: (i, gc_col)),
            pl.BlockSpec((tm, D), lambda i: (i, ga_col)),
            pl.BlockSpec((tm, D), lambda i: (i, 0)),
            pl.BlockSpec((1, 1, D), lambda i: (i // per_seq, 0, 0)),
            _resident((C, D), lambda i: (0, 0)),
            _resident((A, D), lambda i: (0, 0)),
            _resident((D, D), lambda i: (0, 0)),
        ],
        out_specs=pl.BlockSpec((tm, D), lambda i: (i, 0)),
        out_shape=jax.ShapeDtypeStruct((T, D), F32),
        compiler_params=_cparams(("parallel",)),
        name="merge_out_proj",
    )(cact, attn, proj, proj, x, g1, wc, wa, wo)


def _peer_q_kernel(x_ref, g_ref, sc_ref, sh_ref, w_ref, k_ref, h_ref, st_ref):
    hb = _norm_mod(x_ref[...], g_ref[...], sc_ref[0], sh_ref[0]).astype(BF16)
    h_ref[...] = hb
    qb = jnp.dot(hb, w_ref[...], preferred_element_type=F32).astype(BF16)
    for hp in range(2 * PEER_HEADS):
        q_hp = qb[:, hp * PEER_HALF:(hp + 1) * PEER_HALF]
        st_ref[hp * N_KEYS:(hp + 1) * N_KEYS, :] = lax.dot_general(
            k_ref[hp], q_hp, (((1,), (1,)), ((), ())), preferred_element_type=F32)


def _peer_q(x, g, sc, sh, w_pq, keys, seq):
    T, D = x.shape
    Q = w_pq.shape[1]
    tm = 256
    per_seq = seq // tm
    n_rows = 2 * PEER_HEADS * N_KEYS
    return pl.pallas_call(
        _peer_q_kernel,
        grid=(T // tm,),
        in_specs=[
            pl.BlockSpec((tm, D), lambda i: (i, 0)),
            pl.BlockSpec((1, D), lambda i: (0, 0)),
            pl.BlockSpec((1, 1, D), lambda i: (i // per_seq, 0, 0)),
            pl.BlockSpec((1, 1, D), lambda i: (i // per_seq, 0, 0)),
            _resident((D, Q), lambda i: (0, 0)),
            _resident((2 * PEER_HEADS, N_KEYS, PEER_HALF), lambda i: (0, 0, 0)),
        ],
        out_specs=[
            pl.BlockSpec((tm, D), lambda i: (i, 0)),
            pl.BlockSpec((n_rows, tm), lambda i: (0, i)),
        ],
        out_shape=[
            jax.ShapeDtypeStruct((T, D), BF16),
            jax.ShapeDtypeStruct((n_rows, T), F32),
        ],
        compiler_params=_cparams(("parallel",)),
        name="peer_query_scores",
    )(x, g, sc, sh, w_pq, keys)


def _top_rows(x, payload, k, rank=None):
    rows = lax.broadcasted_iota(I32, x.shape, 0) if rank is None else rank
    vals, pays = [], []
    for _ in range(k):
        m = jnp.max(x, axis=0, keepdims=True)
        first = jnp.min(jnp.where(x == m, rows, jnp.iinfo(jnp.int32).max), axis=0, keepdims=True)
        onehot = rows == first
        if payload is None:
            pays.append(first)
        else:
            pays.append(jnp.sum(jnp.where(onehot, payload, 0), axis=0, keepdims=True))
        vals.append(m)
        x = jnp.where(onehot, -jnp.inf, x)
    return jnp.concatenate(vals, axis=0), jnp.concatenate(pays, axis=0)


def _sort_rows(keys, pays):
    n_groups = len(keys)
    n_rows = 8 * n_groups
    sub = lax.broadcasted_iota(I32, keys[0].shape, 0)
    k = 2
    while k <= n_rows:
        j = k // 2
        while j >= 1:
            if j >= 8:
                gj = j // 8
                for lo in range(n_groups):
                    if lo & gj:
                        continue
                    hi = lo | gj
                    ascending = ((8 * lo) & k) == 0
                    a, b = keys[lo], keys[hi]
                    swap = (a > b) if ascending else (a < b)
                    keys[lo], keys[hi] = jnp.where(swap, b, a), jnp.where(swap, a, b)
                    pa, pb = pays[lo], pays[hi]
                    pays[lo], pays[hi] = jnp.where(swap, pb, pa), jnp.where(swap, pa, pb)
            else:
                lower = (sub & j) == 0
                upper = (sub & j) != 0
                for g in range(n_groups):
                    if k >= 8:
                        ascending = ((8 * g) & k) == 0
                        want_max = upper if ascending else lower
                    else:
                        want_max = jnp.logical_xor(upper, (sub & k) != 0)
                    x, p = keys[g], pays[g]
                    xp = jnp.where(lower, pltpu.roll(x, 8 - j, 0), pltpu.roll(x, j, 0))
                    pp = jnp.where(lower, pltpu.roll(p, 8 - j, 0), pltpu.roll(p, j, 0))
                    swap = jnp.logical_xor(xp < x, want_max)
                    keys[g] = jnp.where(swap, xp, x)
                    pays[g] = jnp.where(swap, pp, p)
            j //= 2
        k *= 2
    return keys, pays


def _route_kernel(block_rows, st_ref, idxl_ref, gate_ref, ovf_ref, idx_scr, gate_scr):
    def head(h, carry):
        base = pl.multiple_of(h * (2 * N_KEYS), 2 * N_KEYS)
        v0, i0 = _top_rows(st_ref[pl.ds(base, N_KEYS), :], None, PEER_TOPK)
        v1, i1 = _top_rows(st_ref[pl.ds(base + N_KEYS, N_KEYS), :], None, PEER_TOPK)
        half = PEER_TOPK // 2
        sub = lax.broadcasted_iota(I32, (half, v0.shape[1]), 0)
        cand = [v0[k:k + 1] + v1[:half] for k in range(half)]
        cidx = [i0[k:k + 1] * N_KEYS + i1[:half] for k in range(half)]
        flat = [sub + k * PEER_TOPK for k in range(half)]
        cand += [v0[:1] + v1[half:], v0[half:] + v1[:1]]
        cidx += [i0[:1] * N_KEYS + i1[half:], i0[half:] * N_KEYS + i1[:1]]
        flat += [sub + half, (sub + half) * PEER_TOPK]
        best, eid = _top_rows(jnp.concatenate(cand, axis=0), jnp.concatenate(cidx, axis=0), PEER_TOPK,
                              rank=jnp.concatenate(flat, axis=0))
        e = jnp.exp(best - jnp.max(best, axis=0, keepdims=True))
        gates = e / jnp.sum(e, axis=0, keepdims=True)
        off = pl.multiple_of(h * PEER_TOPK, PEER_TOPK)
        idx_scr[pl.ds(off, PEER_TOPK), :] = eid
        gate_scr[pl.ds(off, PEER_TOPK), :] = gates
        return carry

    lax.fori_loop(0, PEER_HEADS, head, 0)

    n_groups = PAIRS // 8
    sub = lax.broadcasted_iota(I32, (8, idx_scr.shape[1]), 0)
    keys = [idx_scr[8 * g:8 * g + 8, :] * PAIRS + (sub + 8 * g) for g in range(n_groups)]
    pays = [gate_scr[8 * g:8 * g + 8, :] for g in range(n_groups)]
    keys, pays = _sort_rows(keys, pays)
    eids = [jnp.right_shift(k, PAIRS.bit_length() - 1) for k in keys]
    row_hi, row_lo = WINDOW_SLOTS, PAIRS - WINDOW_SLOTS - 1
    ovf0 = jnp.where(eids[row_hi // 8][row_hi % 8:row_hi % 8 + 1, :] < block_rows, 1, 0)
    ovf1 = jnp.where(eids[row_lo // 8][row_lo % 8:row_lo % 8 + 1, :] >= block_rows, 1, 0)
    ovf_ref[...] = jnp.concatenate([ovf0, ovf1], axis=0).astype(I32)
    rot = (PAIRS - WINDOW_SLOTS) // 8
    for b in range(N_TABLE_BLOCKS):
        order = [(g + b * rot) % n_groups for g in range(n_groups)]
        e = jnp.concatenate([eids[g] for g in order], axis=0)
        inb = (e >= b * block_rows) & (e < (b + 1) * block_rows)
        local = jnp.where(inb, e - b * block_rows, 0)
        gate = jnp.where(inb, jnp.concatenate([pays[g] for g in order], axis=0), 0.0)
        idxl_ref[b] = local.astype(F32).T.astype(I32)
        gate_ref[b] = gate.T


def _route(st, block_rows):
    n_rows, T = st.shape
    tr = 128
    return pl.pallas_call(
        functools.partial(_route_kernel, block_rows),
        grid=(T // tr,),
        in_specs=[pl.BlockSpec((n_rows, tr), lambda i: (0, i))],
        out_specs=[
            pl.BlockSpec((N_TABLE_BLOCKS, tr, PAIRS), lambda i: (0, i, 0)),
            pl.BlockSpec((N_TABLE_BLOCKS, tr, PAIRS), lambda i: (0, i, 0)),
            pl.BlockSpec((N_TABLE_BLOCKS, tr), lambda i: (0, i)),
        ],
        out_shape=[
            jax.ShapeDtypeStruct((N_TABLE_BLOCKS, T, PAIRS), I32),
            jax.ShapeDtypeStruct((N_TABLE_BLOCKS, T, PAIRS), F32),
            jax.ShapeDtypeStruct((N_TABLE_BLOCKS, T), I32),
        ],
        scratch_shapes=[pltpu.VMEM((PAIRS, tr), I32), pltpu.VMEM((PAIRS, tr), F32)],
        compiler_params=_cparams(("parallel",)),
        name="peer_route",
    )(st)


def _diag_mask():
    r = lax.broadcasted_iota(I32, (SLAB_ROWS, PAIRS * SLAB_ROWS), 0)
    c = lax.broadcasted_iota(I32, (SLAB_ROWS, PAIRS * SLAB_ROWS), 1)
    return r == (c % SLAB_ROWS)


def _for_overflow_tokens(ovf_ref, b, tm, body):
    group = 8

    def step(g, carry):
        flags = [ovf_ref[b, g * group + i] for i in range(group)]
        any_set = functools.reduce(lambda x, y: x | y, flags)

        @pl.when(any_set != 0)
        def _():
            def one(i, c):
                t = g * group + i

                @pl.when(ovf_ref[b, t] != 0)
                def _():
                    body(t)
                return c

            lax.fori_loop(0, group, one, 0)
        return carry

    lax.fori_loop(0, tm // group, step, 0)


def _gather_slabs(tab_ref, idx_ref, t, slots):
    return jnp.concatenate([tab_ref[idx_ref[0, t, j]] for j in slots], axis=0)


def _peer_u_kernel(idx_ref, ovf_ref, h_ref, gsum_ref, u_ref, o_ref, diag_scr):
    tm = h_ref.shape[0]
    b = pl.program_id(0)
    eye = _diag_mask()
    wcols = WINDOW_SLOTS * SLAB_ROWS

    def diag(t, slots, lo, hi):
        usel = _gather_slabs(u_ref, idx_ref, t, slots)
        r = lax.dot_general(h_ref[t], usel, (((1,), (1,)), ((), ())), preferred_element_type=F32)
        diag_scr[pl.ds(t, 1), lo:hi] = jnp.sum(jnp.where(eye[:, lo:hi], r, 0.0), axis=0, keepdims=True)

    def tok(t, carry):
        diag(t, range(WINDOW_SLOTS), 0, wcols)
        return carry

    def rare(t):
        diag(t, range(WINDOW_SLOTS, PAIRS), wcols, PAIRS * SLAB_ROWS)

    diag_scr[:, wcols:] = jnp.zeros((tm, PAIRS * SLAB_ROWS - wcols), F32)
    lax.fori_loop(0, tm, tok, 0, unroll=TOKEN_UNROLL)
    _for_overflow_tokens(ovf_ref, b, tm, rare)
    d = diag_scr[...]
    hi = d.astype(BF16)
    lo = (d - hi.astype(F32)).astype(BF16)
    o_ref[0] = (jnp.dot(hi, gsum_ref[...], preferred_element_type=F32)
                + jnp.dot(lo, gsum_ref[...], preferred_element_type=F32))


def _peer_u(idxl, ovf, h_slab, group_sum, u_tab):
    T = h_slab.shape[0]
    block_rows = u_tab.shape[0] // N_TABLE_BLOCKS
    tm = 128
    return pl.pallas_call(
        _peer_u_kernel,
        grid=(N_TABLE_BLOCKS, T // tm),
        in_specs=[
            pl.BlockSpec((1, tm, PAIRS), lambda b, i: (b, i, 0), memory_space=pltpu.SMEM),
            pl.BlockSpec((N_TABLE_BLOCKS, tm), lambda b, i: (0, i), memory_space=pltpu.SMEM),
            pl.BlockSpec((tm, SLAB_ROWS, LANES), lambda b, i: (i, 0, 0)),
            _resident((PAIRS * SLAB_ROWS, PAIRS), lambda b, i: (0, 0)),
            _resident((block_rows, SLAB_ROWS, LANES), lambda b, i: (b, 0, 0)),
        ],
        out_specs=pl.BlockSpec((1, tm, PAIRS), lambda b, i: (b, i, 0)),
        out_shape=jax.ShapeDtypeStruct((N_TABLE_BLOCKS, T, PAIRS), F32),
        scratch_shapes=[pltpu.VMEM((tm, PAIRS * SLAB_ROWS), F32)],
        compiler_params=_cparams(("arbitrary", "arbitrary")),
        name="peer_expert_u",
    )(idxl, ovf, h_slab, group_sum, u_tab)


def _peer_v_kernel(idx_ref, ovf_ref, gate_ref, act_ref, ex_ref, v_ref, o_ref, wexp):
    tm = gate_ref.shape[1]
    b = pl.program_id(0)
    act = act_ref[0]
    gelu = 0.5 * act * (1.0 + lax.erf(act * np.float32(math.sqrt(0.5))))
    w = gate_ref[0] * gelu
    wexp[...] = jnp.dot(w.astype(BF16), ex_ref[...], preferred_element_type=F32)
    eye = _diag_mask()
    wcols = WINDOW_SLOTS * SLAB_ROWS

    def weighted(t, slots, lo, hi):
        vsel = _gather_slabs(v_ref, idx_ref, t, slots)
        wrow = jnp.broadcast_to(wexp[pl.ds(t, 1), lo:hi], (SLAB_ROWS, hi - lo))
        wbd = jnp.where(eye[:, lo:hi], wrow, 0.0).astype(BF16)
        return jnp.dot(wbd, vsel, preferred_element_type=F32)

    def tok(t, carry):
        o_ref[0, t] = weighted(t, range(WINDOW_SLOTS), 0, wcols)
        return carry

    def rare(t):
        o_ref[0, t] = o_ref[0, t] + weighted(t, range(WINDOW_SLOTS, PAIRS), wcols, PAIRS * SLAB_ROWS)

    lax.fori_loop(0, tm, tok, 0, unroll=TOKEN_UNROLL)
    _for_overflow_tokens(ovf_ref, b, tm, rare)


def _peer_v(idxl, ovf, gate, act, expand, v_tab):
    T = gate.shape[1]
    block_rows = v_tab.shape[0] // N_TABLE_BLOCKS
    tm = 128
    return pl.pallas_call(
        _peer_v_kernel,
        grid=(N_TABLE_BLOCKS, T // tm),
        in_specs=[
            pl.BlockSpec((1, tm, PAIRS), lambda b, i: (b, i, 0), memory_space=pltpu.SMEM),
            pl.BlockSpec((N_TABLE_BLOCKS, tm), lambda b, i: (0, i), memory_space=pltpu.SMEM),
            pl.BlockSpec((1, tm, PAIRS), lambda b, i: (b, i, 0)),
            pl.BlockSpec((1, tm, PAIRS), lambda b, i: (b, i, 0)),
            _resident((PAIRS, PAIRS * SLAB_ROWS), lambda b, i: (0, 0)),
            _resident((block_rows, SLAB_ROWS, LANES), lambda b, i: (b, 0, 0)),
        ],
        out_specs=pl.BlockSpec((1, tm, SLAB_ROWS, LANES), lambda b, i: (b, i, 0, 0)),
        out_shape=jax.ShapeDtypeStruct((N_TABLE_BLOCKS, T, SLAB_ROWS, LANES), F32),
        scratch_shapes=[pltpu.VMEM((tm, PAIRS * SLAB_ROWS), F32)],
        compiler_params=_cparams(("arbitrary", "arbitrary")),
        name="peer_expert_v",
    )(idxl, ovf, gate, act, expand, v_tab)


def _resid_kernel(final, x_ref, o_ref, g2_ref, fg_ref, y_ref):
    acc = o_ref[0]
    for k in range(1, N_TABLE_BLOCKS):
        acc = acc + o_ref[k]
    x = x_ref[...] + g2_ref[0] * acc
    if final:
        ms = jnp.mean(x * x, axis=-1, keepdims=True)
        x = (x * lax.rsqrt(ms + EPS)) * fg_ref[...]
    y_ref[...] = x


def _resid(x, peer_out, g2, final_g, seq, final):
    T, D = x.shape
    tm = 512
    per_seq = seq // tm
    return pl.pallas_call(
        functools.partial(_resid_kernel, final),
        grid=(T // tm,),
        in_specs=[
            pl.BlockSpec((tm, D), lambda i: (i, 0)),
            pl.BlockSpec((N_TABLE_BLOCKS, tm, D), lambda i: (0, i, 0)),
            pl.BlockSpec((1, 1, D), lambda i: (i // per_seq, 0, 0)),
            pl.BlockSpec((1, D), lambda i: (0, 0)),
        ],
        out_specs=pl.BlockSpec((tm, D), lambda i: (i, 0)),
        out_shape=jax.ShapeDtypeStruct((T, D), F32),
        compiler_params=_cparams(("parallel",)),
        name="peer_residual",
    )(x, peer_out, g2, final_g)


def _pair_expand_matrix():
    p = np.arange(PAIRS)[:, None]
    c = np.arange(PAIRS * SLAB_ROWS)[None, :]
    return (p == c // SLAB_ROWS).astype(np.float32)


def kernel(x, c, rel_bias, ada_w, ada_b, norm1_g, w_in, dw_w, dw_b, conv_ln_g, conv_ln_b, w_conv_out,
           attn_sinks, w_attn_out, w_out, norm2_g, w_pq, sub_keys, peer_u, peer_v, final_g):
    B, S, D = x.shape
    L = ada_w.shape[0]
    T = B * S
    C = dw_w.shape[2]
    A = N_HEADS * HEAD_DIM
    KV = N_KV_HEADS * HEAD_DIM
    E = peer_u.shape[1]
    assert S % 1024 == 0 and D == SLAB_ROWS * LANES and E % N_TABLE_BLOCKS == 0

    o_a, o_b, o_q, o_k, o_v, o_g = 0, C, 2 * C, 2 * C + A, 2 * C + A + KV, 2 * C + A + 2 * KV
    order = np.concatenate([np.arange(o_a, o_a + 2 * C), np.arange(o_g, o_g + 2 * D),
                            np.arange(o_q, o_q + A + 2 * KV)])
    n_a, n_b, n_gc, n_ga, n_q, n_k, n_v = 0, C, 2 * C, 2 * C + D, 2 * C + 2 * D, 2 * C + 2 * D + A, 2 * C + 2 * D + A + KV

    c_pad = jnp.pad(c, ((0, 8 - B), (0, 0)))
    mod = _ada(c_pad, ada_w, ada_b)
    bias_tab = _rel_bias_table(rel_bias)
    expand = jnp.asarray(_pair_expand_matrix(), dtype=BF16)
    group_sum = jnp.asarray(_pair_expand_matrix().T, dtype=BF16)
    final_g2 = final_g.reshape(1, D)

    xt = x.reshape(T, D)
    for l in range(L):
        m = mod[l, :B].reshape(B, 6, 1, D)
        sh1, sc1, g1, sh2, sc2, g2 = (m[:, k] for k in range(6))

        w_in_l = w_in[l][:, order].astype(BF16)
        proj = _inproj(xt, norm1_g[l].reshape(1, D), sc1, sh1, w_in_l, S)
        cact = _conv(proj, dw_w[l], dw_b[l].reshape(1, C), conv_ln_g[l].reshape(1, C),
                     conv_ln_b[l].reshape(1, C), S, n_a // C, n_b // C)
        attn = _attn(proj, bias_tab, attn_sinks[l], S, n_q // A, n_k // KV, n_v // KV)
        xt = _mix(cact, attn, proj, xt, g1, w_conv_out[l].astype(BF16), w_attn_out[l].astype(BF16),
                  w_out[l].astype(BF16), S, n_gc // D, n_ga // D)

        keys = sub_keys[l].reshape(2 * PEER_HEADS, N_KEYS, PEER_HALF).astype(BF16)
        h2, st = _peer_q(xt, norm2_g[l].reshape(1, D), sc2, sh2, w_pq[l].astype(BF16), keys, S)
        idxl, gate, ovf = _route(st, E // N_TABLE_BLOCKS)
        u_tab = peer_u[l].reshape(E, SLAB_ROWS, LANES).astype(BF16)
        v_tab = peer_v[l].reshape(E, SLAB_ROWS, LANES).astype(BF16)
        act = _peer_u(idxl, ovf, h2.reshape(T, SLAB_ROWS, LANES), group_sum, u_tab)
        pout = _peer_v(idxl, ovf, gate, act, expand, v_tab)
        xt = _resid(xt, pout.reshape(N_TABLE_BLOCKS, T, D), g2, final_g2, S, l == L - 1)
    return xt.reshape(B, S, D)
```

```python
import functools
import math

import numpy as np
import jax
import jax.numpy as jnp
from jax import lax
from jax.experimental import pallas as pl
from jax.experimental.pallas import tpu as pltpu

F32 = jnp.float32
BF16 = jnp.bfloat16
I32 = jnp.int32

EPS = 1e-6
CONV_WIDTH = 31
HALO = 32
N_HEADS = 16
N_KV_HEADS = 4
HEAD_DIM = 64
BLOCK = 128
NUM_BUCKETS = 32
MAX_DISTANCE = 128
N_KEYS = 128
PEER_HEADS = 8
PEER_TOPK = 16
PEER_HALF = 128
PAIRS = PEER_HEADS * PEER_TOPK
SLAB_ROWS = 16
LANES = 128
N_TABLE_BLOCKS = 2
TOKEN_UNROLL = 16
SLAB_WORDS = 8
WINDOW_SLOTS = 80
assert PAIRS & (PAIRS - 1) == 0 and WINDOW_SLOTS % 8 == 0 and 2 * WINDOW_SLOTS >= PAIRS
VMEM_LIMIT = 56 * 1024 * 1024


def _cparams(sem, vmem=VMEM_LIMIT):
    return pltpu.CompilerParams(dimension_semantics=sem, vmem_limit_bytes=vmem)


def _resident(block_shape, index_map):
    return pl.BlockSpec(block_shape, index_map, pipeline_mode=pl.Buffered(1))


def _ada_kernel(c_ref, w_ref, b_ref, o_ref):
    c = c_ref[...]
    cs = c * jax.nn.sigmoid(c)
    o_ref[0] = jnp.dot(cs.astype(BF16), w_ref[0].astype(BF16), preferred_element_type=F32) + b_ref[0]


def _ada(c_pad, ada_w, ada_b):
    L, D, N = ada_w.shape
    tn = 1024
    return pl.pallas_call(
        _ada_kernel,
        grid=(L, N // tn),
        in_specs=[
            pl.BlockSpec((8, D), lambda l, j: (0, 0)),
            pl.BlockSpec((1, D, tn), lambda l, j: (l, 0, j)),
            pl.BlockSpec((1, 1, tn), lambda l, j: (l, 0, j)),
        ],
        out_specs=pl.BlockSpec((1, 8, tn), lambda l, j: (l, 0, j)),
        out_shape=jax.ShapeDtypeStruct((L, 8, N), F32),
        compiler_params=_cparams(("parallel", "parallel")),
        name="ada_mod",
    )(c_pad, ada_w, ada_b.reshape(L, 1, N))


def _bias_kernel(bucket_ref, rb_ref, o_ref):
    h = pl.program_id(0)
    bucket = bucket_ref[...]
    acc = jnp.zeros(bucket.shape, F32)
    for b in range(NUM_BUCKETS):
        acc = jnp.where(bucket == b, rb_ref[b, h], acc)
    o_ref[0] = acc


def _t5_bucket_table():
    qi = np.arange(BLOCK)[:, None] + BLOCK
    kj = np.arange(2 * BLOCK)[None, :]
    d = np.maximum(qi - kj, 0)
    max_exact = NUM_BUCKETS // 2
    ratio = (np.log(np.maximum(d, 1).astype(np.float64) / max_exact)
             / math.log(MAX_DISTANCE / max_exact) * (NUM_BUCKETS - max_exact))
    large = np.minimum(max_exact + ratio.astype(np.int64), NUM_BUCKETS - 1)
    return np.where(d < max_exact, d, large).astype(np.int32)


def _rel_bias_table(rel_bias):
    bucket = jnp.asarray(_t5_bucket_table())
    return pl.pallas_call(
        _bias_kernel,
        grid=(N_HEADS,),
        in_specs=[
            pl.BlockSpec((BLOCK, 2 * BLOCK), lambda h: (0, 0)),
            pl.BlockSpec(memory_space=pltpu.SMEM),
        ],
        out_specs=pl.BlockSpec((1, BLOCK, 2 * BLOCK), lambda h: (h, 0, 0)),
        out_shape=jax.ShapeDtypeStruct((N_HEADS, BLOCK, 2 * BLOCK), F32),
        compiler_params=_cparams(("arbitrary",)),
        name="rel_bias_table",
    )(bucket, rel_bias)


def _norm_mod(x, g, sc, sh):
    ms = jnp.mean(x * x, axis=-1, keepdims=True)
    y = x * lax.rsqrt(ms + EPS)
    return (y * g) * (1.0 + sc) + sh


def _inproj_kernel(x_ref, g_ref, sc_ref, sh_ref, w_ref, o_ref, h_scr):
    @pl.when(pl.program_id(1) == 0)
    def _():
        h_scr[...] = _norm_mod(x_ref[...], g_ref[...], sc_ref[0], sh_ref[0]).astype(BF16)

    o_ref[...] = jnp.dot(h_scr[...], w_ref[...], preferred_element_type=F32)


def _inproj(x, g, sc, sh, w, seq):
    T, D = x.shape
    N = w.shape[1]
    tm, tn = 1024, 512
    per_seq = seq // tm
    return pl.pallas_call(
        _inproj_kernel,
        grid=(T // tm, N // tn),
        in_specs=[
            pl.BlockSpec((tm, D), lambda i, j: (i, 0)),
            pl.BlockSpec((1, D), lambda i, j: (0, 0)),
            pl.BlockSpec((1, 1, D), lambda i, j: (i // per_seq, 0, 0)),
            pl.BlockSpec((1, 1, D), lambda i, j: (i // per_seq, 0, 0)),
            pl.BlockSpec((D, tn), lambda i, j: (0, j)),
        ],
        out_specs=pl.BlockSpec((tm, tn), lambda i, j: (i, j)),
        out_shape=jax.ShapeDtypeStruct((T, N), F32),
        scratch_shapes=[pltpu.VMEM((tm, D), BF16)],
        compiler_params=_cparams(("parallel", "arbitrary")),
        name="in_proj",
    )(x, g, sc, sh, w)


def _conv_kernel(per_seq, a_ref, b_ref, ha_ref, hb_ref, dw_ref, db_ref, lg_ref, lb_ref, o_ref, ext):
    ts = a_ref.shape[0]
    first = (pl.program_id(0) % per_seq) == 0
    hu = ha_ref[...] * jax.nn.sigmoid(hb_ref[...])
    ext[0:HALO, :] = jnp.where(first, 0.0, hu)
    ext[HALO:, :] = a_ref[...] * jax.nn.sigmoid(b_ref[...])
    acc = jnp.zeros(a_ref.shape, F32)
    base = HALO - (CONV_WIDTH - 1)
    for w in range(CONV_WIDTH):
        acc = acc + ext[base + w:base + w + ts, :] * dw_ref[w:w + 1, :]
    acc = acc + db_ref[...]
    mu = jnp.mean(acc, axis=-1, keepdims=True)
    cen = acc - mu
    var = jnp.mean(cen * cen, axis=-1, keepdims=True)
    y = cen * lax.rsqrt(var + EPS) * lg_ref[...] + lb_ref[...]
    o_ref[...] = (y * jax.nn.sigmoid(y)).astype(BF16)


def _conv(proj, dw_w, dw_b, ln_g, ln_b, seq, a_col, b_col):
    T = proj.shape[0]
    C = dw_w.shape[1]
    ts = 256
    per_seq = seq // ts
    rb = ts // HALO
    halo_map = lambda col: (lambda i: (jnp.maximum(i * rb - 1, 0), col))
    return pl.pallas_call(
        functools.partial(_conv_kernel, per_seq),
        grid=(T // ts,),
        in_specs=[
            pl.BlockSpec((ts, C), lambda i: (i, a_col)),
            pl.BlockSpec((ts, C), lambda i: (i, b_col)),
            pl.BlockSpec((HALO, C), halo_map(a_col)),
            pl.BlockSpec((HALO, C), halo_map(b_col)),
            pl.BlockSpec((CONV_WIDTH, C), lambda i: (0, 0)),
            pl.BlockSpec((1, C), lambda i: (0, 0)),
            pl.BlockSpec((1, C), lambda i: (0, 0)),
            pl.BlockSpec((1, C), lambda i: (0, 0)),
        ],
        out_specs=pl.BlockSpec((ts, C), lambda i: (i, 0)),
        out_shape=jax.ShapeDtypeStruct((T, C), BF16),
        scratch_shapes=[pltpu.VMEM((HALO + ts, C), F32)],
        compiler_params=_cparams(("parallel",)),
        name="conv_branch",
    )(proj, proj, proj, proj, dw_w, dw_b, ln_g, ln_b)


def _attn_kernel(nb, q_ref, kc_ref, kp_ref, vc_ref, vp_ref, bias_ref, sink_ref, o_ref):
    first = (pl.program_id(0) % nb) == 0
    kk = jnp.concatenate([kp_ref[...], kc_ref[...]], axis=0).astype(BF16)
    vv = jnp.concatenate([vp_ref[...], vc_ref[...]], axis=0).astype(BF16)
    row = lax.broadcasted_iota(I32, (BLOCK, 2 * BLOCK), 0)
    col = lax.broadcasted_iota(I32, (BLOCK, 2 * BLOCK), 1)
    dist = row + BLOCK - col
    valid = (dist >= 0) & (dist < BLOCK) & (jnp.logical_not(first) | (col >= BLOCK))
    group = N_HEADS // N_KV_HEADS
    for kvh in range(N_KV_HEADS):
        k_h = kk[:, kvh * HEAD_DIM:(kvh + 1) * HEAD_DIM]
        v_h = vv[:, kvh * HEAD_DIM:(kvh + 1) * HEAD_DIM]
        qg = q_ref[:, kvh * group * HEAD_DIM:(kvh + 1) * group * HEAD_DIM].astype(BF16)
        outs = []
        for g in range(group):
            h = kvh * group + g
            q_h = qg[:, g * HEAD_DIM:(g + 1) * HEAD_DIM]
            s = lax.dot_general(q_h, k_h, (((1,), (1,)), ((), ())), preferred_element_type=F32)
            s = s * (HEAD_DIM ** -0.5) + bias_ref[h]
            s = jnp.where(valid, s, -jnp.inf)
            sink = sink_ref[h]
            m = jnp.maximum(jnp.max(s, axis=-1, keepdims=True), sink)
            e = jnp.exp(s - m)
            denom = jnp.sum(e, axis=-1, keepdims=True) + jnp.exp(sink - m)
            p = e / denom
            outs.append(jnp.dot(p.astype(BF16), v_h, preferred_element_type=F32))
        o_ref[:, kvh * group * HEAD_DIM:(kvh + 1) * group * HEAD_DIM] = (
            jnp.concatenate(outs, axis=1).astype(BF16))


def _attn(proj, bias_tab, sinks, seq, q_col, k_col, v_col):
    T = proj.shape[0]
    nb = seq // BLOCK
    attn_dim = N_HEADS * HEAD_DIM
    kv_dim = N_KV_HEADS * HEAD_DIM
    prev = lambda col: (lambda i: (jnp.where(i % nb == 0, i, i - 1), col))
    return pl.pallas_call(
        functools.partial(_attn_kernel, nb),
        grid=(T // BLOCK,),
        in_specs=[
            pl.BlockSpec((BLOCK, attn_dim), lambda i: (i, q_col)),
            pl.BlockSpec((BLOCK, kv_dim), lambda i: (i, k_col)),
            pl.BlockSpec((BLOCK, kv_dim), prev(k_col)),
            pl.BlockSpec((BLOCK, kv_dim), lambda i: (i, v_col)),
            pl.BlockSpec((BLOCK, kv_dim), prev(v_col)),
            pl.BlockSpec((N_HEADS, BLOCK, 2 * BLOCK), lambda i: (0, 0, 0)),
            pl.BlockSpec(memory_space=pltpu.SMEM),
        ],
        out_specs=pl.BlockSpec((BLOCK, attn_dim), lambda i: (i, 0)),
        out_shape=jax.ShapeDtypeStruct((T, attn_dim), BF16),
        compiler_params=_cparams(("parallel",)),
        name="swa_attention",
    )(proj, proj, proj, proj, proj, bias_tab, sinks)


def _mix_kernel(ca_ref, at_ref, gc_ref, ga_ref, x_ref, g1_ref, wc_ref, wa_ref, wo_ref, o_ref):
    yc = jnp.dot(ca_ref[...], wc_ref[...], preferred_element_type=F32)
    ya = jnp.dot(at_ref[...], wa_ref[...], preferred_element_type=F32)
    mixed = jax.nn.sigmoid(gc_ref[...]) * yc + jax.nn.sigmoid(ga_ref[...]) * ya
    y = jnp.dot(mixed.astype(BF16), wo_ref[...], preferred_element_type=F32)
    o_ref[...] = x_ref[...] + g1_ref[0] * y


def _mix(cact, attn, proj, x, g1, wc, wa, wo, seq, gc_col, ga_col):
    T, D = x.shape
    C = cact.shape[1]
    A = attn.shape[1]
    tm = 256
    per_seq = seq // tm
    return pl.pallas_call(
        _mix_kernel,
        grid=(T // tm,),
        in_specs=[
            pl.BlockSpec((tm, C), lambda i: (i, 0)),
            pl.BlockSpec((tm, A), lambda i: (i, 0)),
            pl.BlockSpec((tm, D), lambda i: (i, gc_col)),
            pl.BlockSpec((tm, D), lambda i: (i, ga_col)),
            pl.BlockSpec((tm, D), lambda i: (i, 0)),
            pl.BlockSpec((1, 1, D), lambda i: (i // per_seq, 0, 0)),
            _resident((C, D), lambda i: (0, 0)),
            _resident((A, D), lambda i: (0, 0)),
            _resident((D, D), lambda i: (0, 0)),
        ],
        out_specs=pl.BlockSpec((tm, D), lambda i: (i, 0)),
        out_shape=jax.ShapeDtypeStruct((T, D), F32),
        compiler_params=_cparams(("parallel",)),
        name="merge_out_proj",
    )(cact, attn, proj, proj, x, g1, wc, wa, wo)


def _peer_q_kernel(x_ref, g_ref, sc_ref, sh_ref, w_ref, k_ref, h_ref, st_ref):
    hb = _norm_mod(x_ref[...], g_ref[...], sc_ref[0], sh_ref[0]).astype(BF16)
    h_ref[...] = hb
    qb = jnp.dot(hb, w_ref[...], preferred_element_type=F32).astype(BF16)
    for hp in range(2 * PEER_HEADS):
        q_hp = qb[:, hp * PEER_HALF:(hp + 1) * PEER_HALF]
        st_ref[hp * N_KEYS:(hp + 1) * N_KEYS, :] = lax.dot_general(
            k_ref[hp], q_hp, (((1,), (1,)), ((), ())), preferred_element_type=F32)


def _peer_q(x, g, sc, sh, w_pq, keys, seq):
    T, D = x.shape
    Q = w_pq.shape[1]
    tm = 256
    per_seq = seq // tm
    n_rows = 2 * PEER_HEADS * N_KEYS
    return pl.pallas_call(
        _peer_q_kernel,
        grid=(T // tm,),
        in_specs=[
            pl.BlockSpec((tm, D), lambda i: (i, 0)),
            pl.BlockSpec((1, D), lambda i: (0, 0)),
            pl.BlockSpec((1, 1, D), lambda i: (i // per_seq, 0, 0)),
            pl.BlockSpec((1, 1, D), lambda i: (i // per_seq, 0, 0)),
            _resident((D, Q), lambda i: (0, 0)),
            _resident((2 * PEER_HEADS, N_KEYS, PEER_HALF), lambda i: (0, 0, 0)),
        ],
        out_specs=[
            pl.BlockSpec((tm, D), lambda i: (i, 0)),
            pl.BlockSpec((n_rows, tm), lambda i: (0, i)),
        ],
        out_shape=[
            jax.ShapeDtypeStruct((T, D), BF16),
            jax.ShapeDtypeStruct((n_rows, T), F32),
        ],
        compiler_params=_cparams(("parallel",)),
        name="peer_query_scores",
    )(x, g, sc, sh, w_pq, keys)


def _top_rows(x, payload, k, rank=None):
    rows = lax.broadcasted_iota(I32, x.shape, 0) if rank is None else rank
    vals, pays = [], []
    for _ in range(k):
        m = jnp.max(x, axis=0, keepdims=True)
        first = jnp.min(jnp.where(x == m, rows, jnp.iinfo(jnp.int32).max), axis=0, keepdims=True)
        onehot = rows == first
        if payload is None:
            pays.append(first)
        else:
            pays.append(jnp.sum(jnp.where(onehot, payload, 0), axis=0, keepdims=True))
        vals.append(m)
        x = jnp.where(onehot, -jnp.inf, x)
    return jnp.concatenate(vals, axis=0), jnp.concatenate(pays, axis=0)


def _sort_rows(keys, pays):
    n_groups = len(keys)
    n_rows = 8 * n_groups
    sub = lax.broadcasted_iota(I32, keys[0].shape, 0)
    k = 2
    while k <= n_rows:
        j = k // 2
        while j >= 1:
            if j >= 8:
                gj = j // 8
                for lo in range(n_groups):
                    if lo & gj:
                        continue
                    hi = lo | gj
                    ascending = ((8 * lo) & k) == 0
                    a, b = keys[lo], keys[hi]
                    swap = (a > b) if ascending else (a < b)
                    keys[lo], keys[hi] = jnp.where(swap, b, a), jnp.where(swap, a, b)
                    pa, pb = pays[lo], pays[hi]
                    pays[lo], pays[hi] = jnp.where(swap, pb, pa), jnp.where(swap, pa, pb)
            else:
                lower = (sub & j) == 0
                upper = (sub & j) != 0
                for g in range(n_groups):
                    if k >= 8:
                        ascending = ((8 * g) & k) == 0
                        want_max = upper if ascending else lower
                    else:
                        want_max = jnp.logical_xor(upper, (sub & k) != 0)
                    x, p = keys[g], pays[g]
                    xp = jnp.where(lower, pltpu.roll(x, 8 - j, 0), pltpu.roll(x, j, 0))
                    pp = jnp.where(lower, pltpu.roll(p, 8 - j, 0), pltpu.roll(p, j, 0))
                    swap = jnp.logical_xor(xp < x, want_max)
                    keys[g] = jnp.where(swap, xp, x)
                    pays[g] = jnp.where(swap, pp, p)
            j //= 2
        k *= 2
    return keys, pays


def _route_kernel(block_rows, st_ref, idxl_ref, gate_ref, ovf_ref, idx_scr, gate_scr):
    def head(h, carry):
        base = pl.multiple_of(h * (2 * N_KEYS), 2 * N_KEYS)
        v0, i0 = _top_rows(st_ref[pl.ds(base, N_KEYS), :], None, PEER_TOPK)
        v1, i1 = _top_rows(st_ref[pl.ds(base + N_KEYS, N_KEYS), :], None, PEER_TOPK)
        half = PEER_TOPK // 2
        sub = lax.broadcasted_iota(I32, (half, v0.shape[1]), 0)
        cand = [v0[k:k + 1] + v1[:half] for k in range(half)]
        cidx = [i0[k:k + 1] * N_KEYS + i1[:half] for k in range(half)]
        flat = [sub + k * PEER_TOPK for k in range(half)]
        cand += [v0[:1] + v1[half:], v0[half:] + v1[:1]]
        cidx += [i0[:1] * N_KEYS + i1[half:], i0[half:] * N_KEYS + i1[:1]]
        flat += [sub + half, (sub + half) * PEER_TOPK]
        best, eid = _top_rows(jnp.concatenate(cand, axis=0), jnp.concatenate(cidx, axis=0), PEER_TOPK,
                              rank=jnp.concatenate(flat, axis=0))
        e = jnp.exp(best - jnp.max(best, axis=0, keepdims=True))
        gates = e / jnp.sum(e, axis=0, keepdims=True)
        off = pl.multiple_of(h * PEER_TOPK, PEER_TOPK)
        idx_scr[pl.ds(off, PEER_TOPK), :] = eid
        gate_scr[pl.ds(off, PEER_TOPK), :] = gates
        return carry

    lax.fori_loop(0, PEER_HEADS, head, 0)

    n_groups = PAIRS // 8
    sub = lax.broadcasted_iota(I32, (8, idx_scr.shape[1]), 0)
    keys = [idx_scr[8 * g:8 * g + 8, :] * PAIRS + (sub + 8 * g) for g in range(n_groups)]
    pays = [gate_scr[8 * g:8 * g + 8, :] for g in range(n_groups)]
    keys, pays = _sort_rows(keys, pays)
    eids = [jnp.right_shift(k, PAIRS.bit_length() - 1) for k in keys]
    row_hi, row_lo = WINDOW_SLOTS, PAIRS - WINDOW_SLOTS - 1
    ovf0 = jnp.where(eids[row_hi // 8][row_hi % 8:row_hi % 8 + 1, :] < block_rows, 1, 0)
    ovf1 = jnp.where(eids[row_lo // 8][row_lo % 8:row_lo % 8 + 1, :] >= block_rows, 1, 0)
    ovf_ref[...] = jnp.concatenate([ovf0, ovf1], axis=0).astype(I32)
    rot = (PAIRS - WINDOW_SLOTS) // 8
    for b in range(N_TABLE_BLOCKS):
        order = [(g + b * rot) % n_groups for g in range(n_groups)]
        e = jnp.concatenate([eids[g] for g in order], axis=0)
        inb = (e >= b * block_rows) & (e < (b + 1) * block_rows)
        local = jnp.where(inb, (e - b * block_rows) * SLAB_WORDS, 0)
        gate = jnp.where(inb, jnp.concatenate([pays[g] for g in order], axis=0), 0.0)
        idxl_ref[b] = local.astype(F32).T.astype(I32)
        gate_ref[b] = gate.T


def _route(st, block_rows):
    n_rows, T = st.shape
    tr = 128
    return pl.pallas_call(
        functools.partial(_route_kernel, block_rows),
        grid=(T // tr,),
        in_specs=[pl.BlockSpec((n_rows, tr), lambda i: (0, i))],
        out_specs=[
            pl.BlockSpec((N_TABLE_BLOCKS, tr, PAIRS), lambda i: (0, i, 0)),
            pl.BlockSpec((N_TABLE_BLOCKS, tr, PAIRS), lambda i: (0, i, 0)),
            pl.BlockSpec((N_TABLE_BLOCKS, tr), lambda i: (0, i)),
        ],
        out_shape=[
            jax.ShapeDtypeStruct((N_TABLE_BLOCKS, T, PAIRS), I32),
            jax.ShapeDtypeStruct((N_TABLE_BLOCKS, T, PAIRS), F32),
            jax.ShapeDtypeStruct((N_TABLE_BLOCKS, T), I32),
        ],
        scratch_shapes=[pltpu.VMEM((PAIRS, tr), I32), pltpu.VMEM((PAIRS, tr), F32)],
        compiler_params=_cparams(("parallel",)),
        name="peer_route",
    )(st)


def _diag_mask():
    r = lax.broadcasted_iota(I32, (SLAB_ROWS, PAIRS * SLAB_ROWS), 0)
    c = lax.broadcasted_iota(I32, (SLAB_ROWS, PAIRS * SLAB_ROWS), 1)
    return r == (c % SLAB_ROWS)


def _for_overflow_tokens(ovf_ref, b, tm, body):
    group = 8

    def step(g, carry):
        flags = [ovf_ref[b, g * group + i] for i in range(group)]
        any_set = functools.reduce(lambda x, y: x | y, flags)

        @pl.when(any_set != 0)
        def _():
            def one(i, c):
                t = g * group + i

                @pl.when(ovf_ref[b, t] != 0)
                def _():
                    body(t)
                return c

            lax.fori_loop(0, group, one, 0)
        return carry

    lax.fori_loop(0, tm // group, step, 0)


def _gather_slabs(tab_ref, idx_ref, t, slots):
    words = [tab_ref[pl.ds(pl.multiple_of(idx_ref[0, t, j], SLAB_WORDS), SLAB_WORDS), :] for j in slots]
    return pltpu.bitcast(jnp.concatenate(words, axis=0), BF16)


def _pack_table(tab):
    n = tab.shape[0]
    t = tab.reshape(n, SLAB_WORDS, 2, LANES).astype(BF16)
    return lax.bitcast_convert_type(jnp.swapaxes(t, -1, -2), jnp.uint32).reshape(n * SLAB_WORDS, LANES)


def _peer_u_kernel(idx_ref, ovf_ref, h_ref, gsum_ref, u_ref, o_ref, diag_scr):
    tm = h_ref.shape[0]
    b = pl.program_id(0)
    eye = _diag_mask()
    wcols = WINDOW_SLOTS * SLAB_ROWS

    def diag(t, slots, lo, hi):
        usel = _gather_slabs(u_ref, idx_ref, t, slots)
        r = lax.dot_general(h_ref[t], usel, (((1,), (1,)), ((), ())), preferred_element_type=F32)
        diag_scr[pl.ds(t, 1), lo:hi] = jnp.sum(jnp.where(eye[:, lo:hi], r, 0.0), axis=0, keepdims=True)

    def tok(t, carry):
        diag(t, range(WINDOW_SLOTS), 0, wcols)
        return carry

    def rare(t):
        diag(t, range(WINDOW_SLOTS, PAIRS), wcols, PAIRS * SLAB_ROWS)

    diag_scr[:, wcols:] = jnp.zeros((tm, PAIRS * SLAB_ROWS - wcols), F32)
    lax.fori_loop(0, tm, tok, 0, unroll=TOKEN_UNROLL)
    _for_overflow_tokens(ovf_ref, b, tm, rare)
    d = diag_scr[...]
    hi = d.astype(BF16)
    lo = (d - hi.astype(F32)).astype(BF16)
    o_ref[0] = (jnp.dot(hi, gsum_ref[...], preferred_element_type=F32)
                + jnp.dot(lo, gsum_ref[...], preferred_element_type=F32))


def _peer_u(idxl, ovf, h_slab, group_sum, u_tab):
    T = h_slab.shape[0]
    block_words = u_tab.shape[0] // N_TABLE_BLOCKS
    tm = 128
    return pl.pallas_call(
        _peer_u_kernel,
        grid=(N_TABLE_BLOCKS, T // tm),
        in_specs=[
            pl.BlockSpec((1, tm, PAIRS), lambda b, i: (b, i, 0), memory_space=pltpu.SMEM),
            pl.BlockSpec((N_TABLE_BLOCKS, tm), lambda b, i: (0, i), memory_space=pltpu.SMEM),
            pl.BlockSpec((tm, SLAB_ROWS, LANES), lambda b, i: (i, 0, 0)),
            _resident((PAIRS * SLAB_ROWS, PAIRS), lambda b, i: (0, 0)),
            _resident((block_words, LANES), lambda b, i: (b, 0)),
        ],
        out_specs=pl.BlockSpec((1, tm, PAIRS), lambda b, i: (b, i, 0)),
        out_shape=jax.ShapeDtypeStruct((N_TABLE_BLOCKS, T, PAIRS), F32),
        scratch_shapes=[pltpu.VMEM((tm, PAIRS * SLAB_ROWS), F32)],
        compiler_params=_cparams(("arbitrary", "arbitrary")),
        name="peer_expert_u",
    )(idxl, ovf, h_slab, group_sum, u_tab)


def _peer_v_kernel(idx_ref, ovf_ref, gate_ref, act_ref, ex_ref, v_ref, o_ref, wexp):
    tm = gate_ref.shape[1]
    b = pl.program_id(0)
    act = act_ref[0]
    gelu = 0.5 * act * (1.0 + lax.erf(act * np.float32(math.sqrt(0.5))))
    w = gate_ref[0] * gelu
    wexp[...] = jnp.dot(w.astype(BF16), ex_ref[...], preferred_element_type=F32)
    eye = _diag_mask()
    wcols = WINDOW_SLOTS * SLAB_ROWS

    def weighted(t, slots, lo, hi):
        vsel = _gather_slabs(v_ref, idx_ref, t, slots)
        wrow = jnp.broadcast_to(wexp[pl.ds(t, 1), lo:hi], (SLAB_ROWS, hi - lo))
        wbd = jnp.where(eye[:, lo:hi], wrow, 0.0).astype(BF16)
        return jnp.dot(wbd, vsel, preferred_element_type=F32)

    def tok(t, carry):
        o_ref[0, t] = weighted(t, range(WINDOW_SLOTS), 0, wcols)
        return carry

    def rare(t):
        o_ref[0, t] = o_ref[0, t] + weighted(t, range(WINDOW_SLOTS, PAIRS), wcols, PAIRS * SLAB_ROWS)

    lax.fori_loop(0, tm, tok, 0, unroll=TOKEN_UNROLL)
    _for_overflow_tokens(ovf_ref, b, tm, rare)


def _peer_v(idxl, ovf, gate, act, expand, v_tab):
    T = gate.shape[1]
    block_words = v_tab.shape[0] // N_TABLE_BLOCKS
    tm = 128
    return pl.pallas_call(
        _peer_v_kernel,
        grid=(N_TABLE_BLOCKS, T // tm),
        in_specs=[
            pl.BlockSpec((1, tm, PAIRS), lambda b, i: (b, i, 0), memory_space=pltpu.SMEM),
            pl.BlockSpec((N_TABLE_BLOCKS, tm), lambda b, i: (0, i), memory_space=pltpu.SMEM),
            pl.BlockSpec((1, tm, PAIRS), lambda b, i: (b, i, 0)),
            pl.BlockSpec((1, tm, PAIRS), lambda b, i: (b, i, 0)),
            _resident((PAIRS, PAIRS * SLAB_ROWS), lambda b, i: (0, 0)),
            _resident((block_words, LANES), lambda b, i: (b, 0)),
        ],
        out_specs=pl.BlockSpec((1, tm, SLAB_ROWS, LANES), lambda b, i: (b, i, 0, 0)),
        out_shape=jax.ShapeDtypeStruct((N_TABLE_BLOCKS, T, SLAB_ROWS, LANES), F32),
        scratch_shapes=[pltpu.VMEM((tm, PAIRS * SLAB_ROWS), F32)],
        compiler_params=_cparams(("arbitrary", "arbitrary")),
        name="peer_expert_v",
    )(idxl, ovf, gate, act, expand, v_tab)


def _resid_kernel(final, x_ref, o_ref, g2_ref, fg_ref, y_ref):
    acc = o_ref[0]
    for k in range(1, N_TABLE_BLOCKS):
        acc = acc + o_ref[k]
    x = x_ref[...] + g2_ref[0] * acc
    if final:
        ms = jnp.mean(x * x, axis=-1, keepdims=True)
        x = (x * lax.rsqrt(ms + EPS)) * fg_ref[...]
    y_ref[...] = x


def _resid(x, peer_out, g2, final_g, seq, final):
    T, D = x.shape
    tm = 512
    per_seq = seq // tm
    return pl.pallas_call(
        functools.partial(_resid_kernel, final),
        grid=(T // tm,),
        in_specs=[
            pl.BlockSpec((tm, D), lambda i: (i, 0)),
            pl.BlockSpec((N_TABLE_BLOCKS, tm, D), lambda i: (0, i, 0)),
            pl.BlockSpec((1, 1, D), lambda i: (i // per_seq, 0, 0)),
            pl.BlockSpec((1, D), lambda i: (0, 0)),
        ],
        out_specs=pl.BlockSpec((tm, D), lambda i: (i, 0)),
        out_shape=jax.ShapeDtypeStruct((T, D), F32),
        compiler_params=_cparams(("parallel",)),
        name="peer_residual",
    )(x, peer_out, g2, final_g)


def _pair_expand_matrix():
    p = np.arange(PAIRS)[:, None]
    c = np.arange(PAIRS * SLAB_ROWS)[None, :]
    return (p == c // SLAB_ROWS).astype(np.float32)


def kernel(x, c, rel_bias, ada_w, ada_b, norm1_g, w_in, dw_w, dw_b, conv_ln_g, conv_ln_b, w_conv_out,
           attn_sinks, w_attn_out, w_out, norm2_g, w_pq, sub_keys, peer_u, peer_v, final_g):
    B, S, D = x.shape
    L = ada_w.shape[0]
    T = B * S
    C = dw_w.shape[2]
    A = N_HEADS * HEAD_DIM
    KV = N_KV_HEADS * HEAD_DIM
    E = peer_u.shape[1]
    assert S % 1024 == 0 and D == SLAB_ROWS * LANES and E % N_TABLE_BLOCKS == 0

    o_a, o_b, o_q, o_k, o_v, o_g = 0, C, 2 * C, 2 * C + A, 2 * C + A + KV, 2 * C + A + 2 * KV
    order = np.concatenate([np.arange(o_a, o_a + 2 * C), np.arange(o_g, o_g + 2 * D),
                            np.arange(o_q, o_q + A + 2 * KV)])
    n_a, n_b, n_gc, n_ga, n_q, n_k, n_v = 0, C, 2 * C, 2 * C + D, 2 * C + 2 * D, 2 * C + 2 * D + A, 2 * C + 2 * D + A + KV

    c_pad = jnp.pad(c, ((0, 8 - B), (0, 0)))
    mod = _ada(c_pad, ada_w, ada_b)
    bias_tab = _rel_bias_table(rel_bias)
    expand = jnp.asarray(_pair_expand_matrix(), dtype=BF16)
    group_sum = jnp.asarray(_pair_expand_matrix().T, dtype=BF16)
    final_g2 = final_g.reshape(1, D)

    xt = x.reshape(T, D)
    for l in range(L):
        m = mod[l, :B].reshape(B, 6, 1, D)
        sh1, sc1, g1, sh2, sc2, g2 = (m[:, k] for k in range(6))

        w_in_l = w_in[l][:, order].astype(BF16)
        proj = _inproj(xt, norm1_g[l].reshape(1, D), sc1, sh1, w_in_l, S)
        cact = _conv(proj, dw_w[l], dw_b[l].reshape(1, C), conv_ln_g[l].reshape(1, C),
                     conv_ln_b[l].reshape(1, C), S, n_a // C, n_b // C)
        attn = _attn(proj, bias_tab, attn_sinks[l], S, n_q // A, n_k // KV, n_v // KV)
        xt = _mix(cact, attn, proj, xt, g1, w_conv_out[l].astype(BF16), w_attn_out[l].astype(BF16),
                  w_out[l].astype(BF16), S, n_gc // D, n_ga // D)

        keys = sub_keys[l].reshape(2 * PEER_HEADS, N_KEYS, PEER_HALF).astype(BF16)
        h2, st = _peer_q(xt, norm2_g[l].reshape(1, D), sc2, sh2, w_pq[l].astype(BF16), keys, S)
        idxl, gate, ovf = _route(st, E // N_TABLE_BLOCKS)
        u_tab = _pack_table(peer_u[l])
        v_tab = _pack_table(peer_v[l])
        act = _peer_u(idxl, ovf, h2.reshape(T, SLAB_ROWS, LANES), group_sum, u_tab)
        pout = _peer_v(idxl, ovf, gate, act, expand, v_tab)
        xt = _resid(xt, pout.reshape(N_TABLE_BLOCKS, T, D), g2, final_g2, S, l == L - 1)
    return xt.reshape(B, S, D)
```

```python
import functools
import math

import numpy as np
import jax
import jax.numpy as jnp
from jax import lax
from jax.experimental import pallas as pl
from jax.experimental.pallas import tpu as pltpu

F32 = jnp.float32
BF16 = jnp.bfloat16
I32 = jnp.int32

EPS = 1e-6
CONV_WIDTH = 31
HALO = 32
N_HEADS = 16
N_KV_HEADS = 4
HEAD_DIM = 64
BLOCK = 128
NUM_BUCKETS = 32
MAX_DISTANCE = 128
N_KEYS = 128
PEER_HEADS = 8
PEER_TOPK = 16
PEER_HALF = 128
PAIRS = PEER_HEADS * PEER_TOPK
SLAB_ROWS = 16
LANES = 128
N_TABLE_BLOCKS = 2
TOKEN_UNROLL = 16
SLAB_WORDS = 8
WINDOW_SLOTS = 80
assert PAIRS & (PAIRS - 1) == 0 and WINDOW_SLOTS % 8 == 0 and 2 * WINDOW_SLOTS >= PAIRS
VMEM_LIMIT = 56 * 1024 * 1024


def _cparams(sem, vmem=VMEM_LIMIT):
    return pltpu.CompilerParams(dimension_semantics=sem, vmem_limit_bytes=vmem)


def _resident(block_shape, index_map):
    return pl.BlockSpec(block_shape, index_map, pipeline_mode=pl.Buffered(1))


def _ada_kernel(c_ref, w_ref, b_ref, o_ref):
    c = c_ref[...]
    cs = c * jax.nn.sigmoid(c)
    o_ref[0] = jnp.dot(cs.astype(BF16), w_ref[0].astype(BF16), preferred_element_type=F32) + b_ref[0]


def _ada(c_pad, ada_w, ada_b):
    L, D, N = ada_w.shape
    tn = 1024
    return pl.pallas_call(
        _ada_kernel,
        grid=(L, N // tn),
        in_specs=[
            pl.BlockSpec((8, D), lambda l, j: (0, 0)),
            pl.BlockSpec((1, D, tn), lambda l, j: (l, 0, j)),
            pl.BlockSpec((1, 1, tn), lambda l, j: (l, 0, j)),
        ],
        out_specs=pl.BlockSpec((1, 8, tn), lambda l, j: (l, 0, j)),
        out_shape=jax.ShapeDtypeStruct((L, 8, N), F32),
        compiler_params=_cparams(("parallel", "parallel")),
        name="ada_mod",
    )(c_pad, ada_w, ada_b.reshape(L, 1, N))


def _bias_kernel(bucket_ref, rb_ref, o_ref):
    h = pl.program_id(0)
    bucket = bucket_ref[...]
    acc = jnp.zeros(bucket.shape, F32)
    for b in range(NUM_BUCKETS):
        acc = jnp.where(bucket == b, rb_ref[b, h], acc)
    o_ref[0] = acc


def _t5_bucket_table():
    qi = np.arange(BLOCK)[:, None] + BLOCK
    kj = np.arange(2 * BLOCK)[None, :]
    d = np.maximum(qi - kj, 0)
    max_exact = NUM_BUCKETS // 2
    ratio = (np.log(np.maximum(d, 1).astype(np.float64) / max_exact)
             / math.log(MAX_DISTANCE / max_exact) * (NUM_BUCKETS - max_exact))
    large = np.minimum(max_exact + ratio.astype(np.int64), NUM_BUCKETS - 1)
    return np.where(d < max_exact, d, large).astype(np.int32)


def _rel_bias_table(rel_bias):
    bucket = jnp.asarray(_t5_bucket_table())
    return pl.pallas_call(
        _bias_kernel,
        grid=(N_HEADS,),
        in_specs=[
            pl.BlockSpec((BLOCK, 2 * BLOCK), lambda h: (0, 0)),
            pl.BlockSpec(memory_space=pltpu.SMEM),
        ],
        out_specs=pl.BlockSpec((1, BLOCK, 2 * BLOCK), lambda h: (h, 0, 0)),
        out_shape=jax.ShapeDtypeStruct((N_HEADS, BLOCK, 2 * BLOCK), F32),
        compiler_params=_cparams(("arbitrary",)),
        name="rel_bias_table",
    )(bucket, rel_bias)


def _norm_mod(x, g, sc, sh):
    ms = jnp.mean(x * x, axis=-1, keepdims=True)
    y = x * lax.rsqrt(ms + EPS)
    return (y * g) * (1.0 + sc) + sh


def _inproj_kernel(x_ref, g_ref, sc_ref, sh_ref, w_ref, o_ref, h_scr):
    @pl.when(pl.program_id(1) == 0)
    def _():
        h_scr[...] = _norm_mod(x_ref[...], g_ref[...], sc_ref[0], sh_ref[0]).astype(BF16)

    o_ref[...] = jnp.dot(h_scr[...], w_ref[...], preferred_element_type=F32)


def _inproj(x, g, sc, sh, w, seq):
    T, D = x.shape
    N = w.shape[1]
    tm, tn = 1024, 512
    per_seq = seq // tm
    return pl.pallas_call(
        _inproj_kernel,
        grid=(T // tm, N // tn),
        in_specs=[
            pl.BlockSpec((tm, D), lambda i, j: (i, 0)),
            pl.BlockSpec((1, D), lambda i, j: (0, 0)),
            pl.BlockSpec((1, 1, D), lambda i, j: (i // per_seq, 0, 0)),
            pl.BlockSpec((1, 1, D), lambda i, j: (i // per_seq, 0, 0)),
            pl.BlockSpec((D, tn), lambda i, j: (0, j)),
        ],
        out_specs=pl.BlockSpec((tm, tn), lambda i, j: (i, j)),
        out_shape=jax.ShapeDtypeStruct((T, N), F32),
        scratch_shapes=[pltpu.VMEM((tm, D), BF16)],
        compiler_params=_cparams(("parallel", "arbitrary")),
        name="in_proj",
    )(x, g, sc, sh, w)


def _conv_kernel(per_seq, a_ref, b_ref, ha_ref, hb_ref, dw_ref, db_ref, lg_ref, lb_ref, o_ref, ext):
    ts = a_ref.shape[0]
    first = (pl.program_id(0) % per_seq) == 0
    hu = ha_ref[...] * jax.nn.sigmoid(hb_ref[...])
    ext[0:HALO, :] = jnp.where(first, 0.0, hu)
    ext[HALO:, :] = a_ref[...] * jax.nn.sigmoid(b_ref[...])
    acc = jnp.zeros(a_ref.shape, F32)
    base = HALO - (CONV_WIDTH - 1)
    for w in range(CONV_WIDTH):
        acc = acc + ext[base + w:base + w + ts, :] * dw_ref[w:w + 1, :]
    acc = acc + db_ref[...]
    mu = jnp.mean(acc, axis=-1, keepdims=True)
    cen = acc - mu
    var = jnp.mean(cen * cen, axis=-1, keepdims=True)
    y = cen * lax.rsqrt(var + EPS) * lg_ref[...] + lb_ref[...]
    o_ref[...] = (y * jax.nn.sigmoid(y)).astype(BF16)


def _conv(proj, dw_w, dw_b, ln_g, ln_b, seq, a_col, b_col):
    T = proj.shape[0]
    C = dw_w.shape[1]
    ts = 256
    per_seq = seq // ts
    rb = ts // HALO
    halo_map = lambda col: (lambda i: (jnp.maximum(i * rb - 1, 0), col))
    return pl.pallas_call(
        functools.partial(_conv_kernel, per_seq),
        grid=(T // ts,),
        in_specs=[
            pl.BlockSpec((ts, C), lambda i: (i, a_col)),
            pl.BlockSpec((ts, C), lambda i: (i, b_col)),
            pl.BlockSpec((HALO, C), halo_map(a_col)),
            pl.BlockSpec((HALO, C), halo_map(b_col)),
            pl.BlockSpec((CONV_WIDTH, C), lambda i: (0, 0)),
            pl.BlockSpec((1, C), lambda i: (0, 0)),
            pl.BlockSpec((1, C), lambda i: (0, 0)),
            pl.BlockSpec((1, C), lambda i: (0, 0)),
        ],
        out_specs=pl.BlockSpec((ts, C), lambda i: (i, 0)),
        out_shape=jax.ShapeDtypeStruct((T, C), BF16),
        scratch_shapes=[pltpu.VMEM((HALO + ts, C), F32)],
        compiler_params=_cparams(("parallel",)),
        name="conv_branch",
    )(proj, proj, proj, proj, dw_w, dw_b, ln_g, ln_b)


def _attn_kernel(nb, q_ref, kc_ref, kp_ref, vc_ref, vp_ref, bias_ref, sink_ref, o_ref):
    first = (pl.program_id(0) % nb) == 0
    kk = jnp.concatenate([kp_ref[...], kc_ref[...]], axis=0).astype(BF16)
    vv = jnp.concatenate([vp_ref[...], vc_ref[...]], axis=0).astype(BF16)
    row = lax.broadcasted_iota(I32, (BLOCK, 2 * BLOCK), 0)
    col = lax.broadcasted_iota(I32, (BLOCK, 2 * BLOCK), 1)
    dist = row + BLOCK - col
    valid = (dist >= 0) & (dist < BLOCK) & (jnp.logical_not(first) | (col >= BLOCK))
    group = N_HEADS // N_KV_HEADS
    for kvh in range(N_KV_HEADS):
        k_h = kk[:, kvh * HEAD_DIM:(kvh + 1) * HEAD_DIM]
        v_h = vv[:, kvh * HEAD_DIM:(kvh + 1) * HEAD_DIM]
        qg = q_ref[:, kvh * group * HEAD_DIM:(kvh + 1) * group * HEAD_DIM].astype(BF16)
        outs = []
        for g in range(group):
            h = kvh * group + g
            q_h = qg[:, g * HEAD_DIM:(g + 1) * HEAD_DIM]
            s = lax.dot_general(q_h, k_h, (((1,), (1,)), ((), ())), preferred_element_type=F32)
            s = s * (HEAD_DIM ** -0.5) + bias_ref[h]
            s = jnp.where(valid, s, -jnp.inf)
            sink = sink_ref[h]
            m = jnp.maximum(jnp.max(s, axis=-1, keepdims=True), sink)
            e = jnp.exp(s - m)
            denom = jnp.sum(e, axis=-1, keepdims=True) + jnp.exp(sink - m)
            p = e / denom
            outs.append(jnp.dot(p.astype(BF16), v_h, preferred_element_type=F32))
        o_ref[:, kvh * group * HEAD_DIM:(kvh + 1) * group * HEAD_DIM] = (
            jnp.concatenate(outs, axis=1).astype(BF16))


def _attn(proj, bias_tab, sinks, seq, q_col, k_col, v_col):
    T = proj.shape[0]
    nb = seq // BLOCK
    attn_dim = N_HEADS * HEAD_DIM
    kv_dim = N_KV_HEADS * HEAD_DIM
    prev = lambda col: (lambda i: (jnp.where(i % nb == 0, i, i - 1), col))
    return pl.pallas_call(
        functools.partial(_attn_kernel, nb),
        grid=(T // BLOCK,),
        in_specs=[
            pl.BlockSpec((BLOCK, attn_dim), lambda i: (i, q_col)),
            pl.BlockSpec((BLOCK, kv_dim), lambda i: (i, k_col)),
            pl.BlockSpec((BLOCK, kv_dim), prev(k_col)),
            pl.BlockSpec((BLOCK, kv_dim), lambda i: (i, v_col)),
            pl.BlockSpec((BLOCK, kv_dim), prev(v_col)),
            pl.BlockSpec((N_HEADS, BLOCK, 2 * BLOCK), lambda i: (0, 0, 0)),
            pl.BlockSpec(memory_space=pltpu.SMEM),
        ],
        out_specs=pl.BlockSpec((BLOCK, attn_dim), lambda i: (i, 0)),
        out_shape=jax.ShapeDtypeStruct((T, attn_dim), BF16),
        compiler_params=_cparams(("parallel",)),
        name="swa_attention",
    )(proj, proj, proj, proj, proj, bias_tab, sinks)


def _mix_kernel(ca_ref, at_ref, gc_ref, ga_ref, x_ref, g1_ref, wc_ref, wa_ref, wo_ref, o_ref):
    yc = jnp.dot(ca_ref[...], wc_ref[...], preferred_element_type=F32)
    ya = jnp.dot(at_ref[...], wa_ref[...], preferred_element_type=F32)
    mixed = jax.nn.sigmoid(gc_ref[...]) * yc + jax.nn.sigmoid(ga_ref[...]) * ya
    y = jnp.dot(mixed.astype(BF16), wo_ref[...], preferred_element_type=F32)
    o_ref[...] = x_ref[...] + g1_ref[0] * y


def _mix(cact, attn, proj, x, g1, wc, wa, wo, seq, gc_col, ga_col):
    T, D = x.shape
    C = cact.shape[1]
    A = attn.shape[1]
    tm = 256
    per_seq = seq // tm
    return pl.pallas_call(
        _mix_kernel,
        grid=(T // tm,),
        in_specs=[
            pl.BlockSpec((tm, C), lambda i: (i, 0)),
            pl.BlockSpec((tm, A), lambda i: (i, 0)),
            pl.BlockSpec((tm, D), lambda i: (i, gc_col)),
            pl.BlockSpec((tm, D), lambda i: (i, ga_col)),
            pl.BlockSpec((tm, D), lambda i: (i, 0)),
            pl.BlockSpec((1, 1, D), lambda i: (i // per_seq, 0, 0)),
            _resident((C, D), lambda i: (0, 0)),
            _resident((A, D), lambda i: (0, 0)),
            _resident((D, D), lambda i: (0, 0)),
        ],
        out_specs=pl.BlockSpec((tm, D), lambda i: (i, 0)),
        out_shape=jax.ShapeDtypeStruct((T, D), F32),
        compiler_params=_cparams(("parallel",)),
        name="merge_out_proj",
    )(cact, attn, proj, proj, x, g1, wc, wa, wo)


def _store_packed_slabs(o_ref, xb):
    w = lax.bitcast_convert_type(xb.astype(F32), jnp.uint32)
    for i in range(SLAB_WORDS):
        lo = w[:, (2 * i) * LANES:(2 * i + 1) * LANES]
        hi = w[:, (2 * i + 1) * LANES:(2 * i + 2) * LANES]
        o_ref[:, i, :] = hi | (lo >> 16)


def _pack_kernel(x_ref, o_ref):
    _store_packed_slabs(o_ref, x_ref[...].astype(BF16))


def _pack_table(tab):
    n, d = tab.shape
    te = 256
    packed = pl.pallas_call(
        _pack_kernel,
        grid=(n // te,),
        in_specs=[pl.BlockSpec((te, d), lambda i: (i, 0))],
        out_specs=pl.BlockSpec((te, SLAB_WORDS, LANES), lambda i: (i, 0, 0)),
        out_shape=jax.ShapeDtypeStruct((n, SLAB_WORDS, LANES), jnp.uint32),
        compiler_params=_cparams(("parallel",)),
        name="pack_expert_table",
    )(tab)
    return packed.reshape(n * SLAB_WORDS, LANES)


def _peer_q_kernel(x_ref, g_ref, sc_ref, sh_ref, w_ref, k_ref, h_ref, st_ref):
    hb = _norm_mod(x_ref[...], g_ref[...], sc_ref[0], sh_ref[0]).astype(BF16)
    _store_packed_slabs(h_ref, hb)
    qb = jnp.dot(hb, w_ref[...], preferred_element_type=F32).astype(BF16)
    for hp in range(2 * PEER_HEADS):
        q_hp = qb[:, hp * PEER_HALF:(hp + 1) * PEER_HALF]
        st_ref[hp * N_KEYS:(hp + 1) * N_KEYS, :] = lax.dot_general(
            k_ref[hp], q_hp, (((1,), (1,)), ((), ())), preferred_element_type=F32)


def _peer_q(x, g, sc, sh, w_pq, keys, seq):
    T, D = x.shape
    Q = w_pq.shape[1]
    tm = 256
    per_seq = seq // tm
    n_rows = 2 * PEER_HEADS * N_KEYS
    return pl.pallas_call(
        _peer_q_kernel,
        grid=(T // tm,),
        in_specs=[
            pl.BlockSpec((tm, D), lambda i: (i, 0)),
            pl.BlockSpec((1, D), lambda i: (0, 0)),
            pl.BlockSpec((1, 1, D), lambda i: (i // per_seq, 0, 0)),
            pl.BlockSpec((1, 1, D), lambda i: (i // per_seq, 0, 0)),
            _resident((D, Q), lambda i: (0, 0)),
            _resident((2 * PEER_HEADS, N_KEYS, PEER_HALF), lambda i: (0, 0, 0)),
        ],
        out_specs=[
            pl.BlockSpec((tm, SLAB_WORDS, LANES), lambda i: (i, 0, 0)),
            pl.BlockSpec((n_rows, tm), lambda i: (0, i)),
        ],
        out_shape=[
            jax.ShapeDtypeStruct((T, SLAB_WORDS, LANES), jnp.uint32),
            jax.ShapeDtypeStruct((n_rows, T), F32),
        ],
        compiler_params=_cparams(("parallel",)),
        name="peer_query_scores",
    )(x, g, sc, sh, w_pq, keys)


def _top_rows(x, payload, k, rank=None):
    rows = lax.broadcasted_iota(I32, x.shape, 0) if rank is None else rank
    vals, pays = [], []
    for _ in range(k):
        m = jnp.max(x, axis=0, keepdims=True)
        first = jnp.min(jnp.where(x == m, rows, jnp.iinfo(jnp.int32).max), axis=0, keepdims=True)
        onehot = rows == first
        if payload is None:
            pays.append(first)
        else:
            pays.append(jnp.sum(jnp.where(onehot, payload, 0), axis=0, keepdims=True))
        vals.append(m)
        x = jnp.where(onehot, -jnp.inf, x)
    return jnp.concatenate(vals, axis=0), jnp.concatenate(pays, axis=0)


def _sort_rows(keys, pays):
    n_groups = len(keys)
    n_rows = 8 * n_groups
    sub = lax.broadcasted_iota(I32, keys[0].shape, 0)
    k = 2
    while k <= n_rows:
        j = k // 2
        while j >= 1:
            if j >= 8:
                gj = j // 8
                for lo in range(n_groups):
                    if lo & gj:
                        continue
                    hi = lo | gj
                    ascending = ((8 * lo) & k) == 0
                    a, b = keys[lo], keys[hi]
                    swap = (a > b) if ascending else (a < b)
                    keys[lo], keys[hi] = jnp.where(swap, b, a), jnp.where(swap, a, b)
                    pa, pb = pays[lo], pays[hi]
                    pays[lo], pays[hi] = jnp.where(swap, pb, pa), jnp.where(swap, pa, pb)
            else:
                lower = (sub & j) == 0
                upper = (sub & j) != 0
                for g in range(n_groups):
                    if k >= 8:
                        ascending = ((8 * g) & k) == 0
                        want_max = upper if ascending else lower
                    else:
                        want_max = jnp.logical_xor(upper, (sub & k) != 0)
                    x, p = keys[g], pays[g]
                    xp = jnp.where(lower, pltpu.roll(x, 8 - j, 0), pltpu.roll(x, j, 0))
                    pp = jnp.where(lower, pltpu.roll(p, 8 - j, 0), pltpu.roll(p, j, 0))
                    swap = jnp.logical_xor(xp < x, want_max)
                    keys[g] = jnp.where(swap, xp, x)
                    pays[g] = jnp.where(swap, pp, p)
            j //= 2
        k *= 2
    return keys, pays


def _route_kernel(block_rows, st_ref, idxl_ref, gate_ref, ovf_ref, idx_scr, gate_scr):
    def head(h, carry):
        base = pl.multiple_of(h * (2 * N_KEYS), 2 * N_KEYS)
        v0, i0 = _top_rows(st_ref[pl.ds(base, N_KEYS), :], None, PEER_TOPK)
        v1, i1 = _top_rows(st_ref[pl.ds(base + N_KEYS, N_KEYS), :], None, PEER_TOPK)
        half = PEER_TOPK // 2
        sub = lax.broadcasted_iota(I32, (half, v0.shape[1]), 0)
        cand = [v0[k:k + 1] + v1[:half] for k in range(half)]
        cidx = [i0[k:k + 1] * N_KEYS + i1[:half] for k in range(half)]
        flat = [sub + k * PEER_TOPK for k in range(half)]
        cand += [v0[:1] + v1[half:], v0[half:] + v1[:1]]
        cidx += [i0[:1] * N_KEYS + i1[half:], i0[half:] * N_KEYS + i1[:1]]
        flat += [sub + half, (sub + half) * PEER_TOPK]
        best, eid = _top_rows(jnp.concatenate(cand, axis=0), jnp.concatenate(cidx, axis=0), PEER_TOPK,
                              rank=jnp.concatenate(flat, axis=0))
        e = jnp.exp(best - jnp.max(best, axis=0, keepdims=True))
        gates = e / jnp.sum(e, axis=0, keepdims=True)
        off = pl.multiple_of(h * PEER_TOPK, PEER_TOPK)
        idx_scr[pl.ds(off, PEER_TOPK), :] = eid
        gate_scr[pl.ds(off, PEER_TOPK), :] = gates
        return carry

    lax.fori_loop(0, PEER_HEADS, head, 0)

    n_groups = PAIRS // 8
    sub = lax.broadcasted_iota(I32, (8, idx_scr.shape[1]), 0)
    keys = [idx_scr[8 * g:8 * g + 8, :] * PAIRS + (sub + 8 * g) for g in range(n_groups)]
    pays = [gate_scr[8 * g:8 * g + 8, :] for g in range(n_groups)]
    keys, pays = _sort_rows(keys, pays)
    eids = [jnp.right_shift(k, PAIRS.bit_length() - 1) for k in keys]
    row_hi, row_lo = WINDOW_SLOTS, PAIRS - WINDOW_SLOTS - 1
    ovf0 = jnp.where(eids[row_hi // 8][row_hi % 8:row_hi % 8 + 1, :] < block_rows, 1, 0)
    ovf1 = jnp.where(eids[row_lo // 8][row_lo % 8:row_lo % 8 + 1, :] >= block_rows, 1, 0)
    ovf_ref[...] = jnp.concatenate([ovf0, ovf1], axis=0).astype(I32)
    rot = (PAIRS - WINDOW_SLOTS) // 8
    for b in range(N_TABLE_BLOCKS):
        order = [(g + b * rot) % n_groups for g in range(n_groups)]
        e = jnp.concatenate([eids[g] for g in order], axis=0)
        inb = (e >= b * block_rows) & (e < (b + 1) * block_rows)
        local = jnp.where(inb, (e - b * block_rows) * SLAB_WORDS, 0)
        gate = jnp.where(inb, jnp.concatenate([pays[g] for g in order], axis=0), 0.0)
        idxl_ref[b] = local.astype(F32).T.astype(I32)
        gate_ref[b] = gate.T


def _route(st, block_rows):
    n_rows, T = st.shape
    tr = 128
    return pl.pallas_call(
        functools.partial(_route_kernel, block_rows),
        grid=(T // tr,),
        in_specs=[pl.BlockSpec((n_rows, tr), lambda i: (0, i))],
        out_specs=[
            pl.BlockSpec((N_TABLE_BLOCKS, tr, PAIRS), lambda i: (0, i, 0)),
            pl.BlockSpec((N_TABLE_BLOCKS, tr, PAIRS), lambda i: (0, i, 0)),
            pl.BlockSpec((N_TABLE_BLOCKS, tr), lambda i: (0, i)),
        ],
        out_shape=[
            jax.ShapeDtypeStruct((N_TABLE_BLOCKS, T, PAIRS), I32),
            jax.ShapeDtypeStruct((N_TABLE_BLOCKS, T, PAIRS), F32),
            jax.ShapeDtypeStruct((N_TABLE_BLOCKS, T), I32),
        ],
        scratch_shapes=[pltpu.VMEM((PAIRS, tr), I32), pltpu.VMEM((PAIRS, tr), F32)],
        compiler_params=_cparams(("parallel",)),
        name="peer_route",
    )(st)


def _diag_mask():
    r = lax.broadcasted_iota(I32, (SLAB_ROWS, PAIRS * SLAB_ROWS), 0)
    c = lax.broadcasted_iota(I32, (SLAB_ROWS, PAIRS * SLAB_ROWS), 1)
    return r == (c % SLAB_ROWS)


def _for_overflow_tokens(ovf_ref, b, tm, body):
    group = 8

    def step(g, carry):
        flags = [ovf_ref[b, g * group + i] for i in range(group)]
        any_set = functools.reduce(lambda x, y: x | y, flags)

        @pl.when(any_set != 0)
        def _():
            def one(i, c):
                t = g * group + i

                @pl.when(ovf_ref[b, t] != 0)
                def _():
                    body(t)
                return c

            lax.fori_loop(0, group, one, 0)
        return carry

    lax.fori_loop(0, tm // group, step, 0)


def _gather_slabs(tab_ref, idx_ref, t, slots):
    words = [tab_ref[pl.ds(pl.multiple_of(idx_ref[0, t, j], SLAB_WORDS), SLAB_WORDS), :] for j in slots]
    return pltpu.bitcast(jnp.concatenate(words, axis=0), BF16)


def _peer_u_kernel(idx_ref, ovf_ref, h_ref, gsum_ref, u_ref, o_ref, diag_scr):
    tm = h_ref.shape[0]
    b = pl.program_id(0)
    eye = _diag_mask()
    wcols = WINDOW_SLOTS * SLAB_ROWS

    def diag(t, slots, lo, hi):
        usel = _gather_slabs(u_ref, idx_ref, t, slots)
        r = lax.dot_general(pltpu.bitcast(h_ref[t], BF16), usel, (((1,), (1,)), ((), ())),
                            preferred_element_type=F32)
        diag_scr[pl.ds(t, 1), lo:hi] = jnp.sum(jnp.where(eye[:, lo:hi], r, 0.0), axis=0, keepdims=True)

    def tok(t, carry):
        diag(t, range(WINDOW_SLOTS), 0, wcols)
        return carry

    def rare(t):
        diag(t, range(WINDOW_SLOTS, PAIRS), wcols, PAIRS * SLAB_ROWS)

    diag_scr[:, wcols:] = jnp.zeros((tm, PAIRS * SLAB_ROWS - wcols), F32)
    lax.fori_loop(0, tm, tok, 0, unroll=TOKEN_UNROLL)
    _for_overflow_tokens(ovf_ref, b, tm, rare)
    d = diag_scr[...]
    hi = d.astype(BF16)
    lo = (d - hi.astype(F32)).astype(BF16)
    o_ref[0] = (jnp.dot(hi, gsum_ref[...], preferred_element_type=F32)
                + jnp.dot(lo, gsum_ref[...], preferred_element_type=F32))


def _peer_u(idxl, ovf, h_slab, group_sum, u_tab):
    T = h_slab.shape[0]
    block_words = u_tab.shape[0] // N_TABLE_BLOCKS
    tm = 128
    return pl.pallas_call(
        _peer_u_kernel,
        grid=(N_TABLE_BLOCKS, T // tm),
        in_specs=[
            pl.BlockSpec((1, tm, PAIRS), lambda b, i: (b, i, 0), memory_space=pltpu.SMEM),
            pl.BlockSpec((N_TABLE_BLOCKS, tm), lambda b, i: (0, i), memory_space=pltpu.SMEM),
            pl.BlockSpec((tm, SLAB_WORDS, LANES), lambda b, i: (i, 0, 0)),
            _resident((PAIRS * SLAB_ROWS, PAIRS), lambda b, i: (0, 0)),
            _resident((block_words, LANES), lambda b, i: (b, 0)),
        ],
        out_specs=pl.BlockSpec((1, tm, PAIRS), lambda b, i: (b, i, 0)),
        out_shape=jax.ShapeDtypeStruct((N_TABLE_BLOCKS, T, PAIRS), F32),
        scratch_shapes=[pltpu.VMEM((tm, PAIRS * SLAB_ROWS), F32)],
        compiler_params=_cparams(("arbitrary", "arbitrary")),
        name="peer_expert_u",
    )(idxl, ovf, h_slab, group_sum, u_tab)


def _peer_v_kernel(idx_ref, ovf_ref, gate_ref, act_ref, ex_ref, v_ref, o_ref, wexp):
    tm = gate_ref.shape[1]
    b = pl.program_id(0)
    act = act_ref[0]
    gelu = 0.5 * act * (1.0 + lax.erf(act * np.float32(math.sqrt(0.5))))
    w = gate_ref[0] * gelu
    wexp[...] = jnp.dot(w.astype(BF16), ex_ref[...], preferred_element_type=F32)
    eye = _diag_mask()
    wcols = WINDOW_SLOTS * SLAB_ROWS

    def weighted(t, slots, lo, hi):
        vsel = _gather_slabs(v_ref, idx_ref, t, slots)
        wrow = jnp.broadcast_to(wexp[pl.ds(t, 1), lo:hi], (SLAB_ROWS, hi - lo))
        wbd = jnp.where(eye[:, lo:hi], wrow, 0.0).astype(BF16)
        return jnp.dot(wbd, vsel, preferred_element_type=F32)

    def tok(t, carry):
        o_ref[0, t] = weighted(t, range(WINDOW_SLOTS), 0, wcols)
        return carry

    def rare(t):
        o_ref[0, t] = o_ref[0, t] + weighted(t, range(WINDOW_SLOTS, PAIRS), wcols, PAIRS * SLAB_ROWS)

    lax.fori_loop(0, tm, tok, 0, unroll=TOKEN_UNROLL)
    _for_overflow_tokens(ovf_ref, b, tm, rare)


def _peer_v(idxl, ovf, gate, act, expand, v_tab):
    T = gate.shape[1]
    block_words = v_tab.shape[0] // N_TABLE_BLOCKS
    tm = 128
    return pl.pallas_call(
        _peer_v_kernel,
        grid=(N_TABLE_BLOCKS, T // tm),
        in_specs=[
            pl.BlockSpec((1, tm, PAIRS), lambda b, i: (b, i, 0), memory_space=pltpu.SMEM),
            pl.BlockSpec((N_TABLE_BLOCKS, tm), lambda b, i: (0, i), memory_space=pltpu.SMEM),
            pl.BlockSpec((1, tm, PAIRS), lambda b, i: (b, i, 0)),
            pl.BlockSpec((1, tm, PAIRS), lambda b, i: (b, i, 0)),
            _resident((PAIRS, PAIRS * SLAB_ROWS), lambda b, i: (0, 0)),
            _resident((block_words, LANES), lambda b, i: (b, 0)),
        ],
        out_specs=pl.BlockSpec((1, tm, SLAB_ROWS, LANES), lambda b, i: (b, i, 0, 0)),
        out_shape=jax.ShapeDtypeStruct((N_TABLE_BLOCKS, T, SLAB_ROWS, LANES), F32),
        scratch_shapes=[pltpu.VMEM((tm, PAIRS * SLAB_ROWS), F32)],
        compiler_params=_cparams(("arbitrary", "arbitrary")),
        name="peer_expert_v",
    )(idxl, ovf, gate, act, expand, v_tab)


def _resid_kernel(final, x_ref, o_ref, g2_ref, fg_ref, y_ref):
    cols = []
    for s in range(SLAB_ROWS):
        part = o_ref[0, :, s, :]
        for k in range(1, N_TABLE_BLOCKS):
            part = part + o_ref[k, :, s, :]
        cols.append(part)
    x = x_ref[...] + g2_ref[0] * jnp.concatenate(cols, axis=1)
    if final:
        ms = jnp.mean(x * x, axis=-1, keepdims=True)
        x = (x * lax.rsqrt(ms + EPS)) * fg_ref[...]
    y_ref[...] = x


def _resid(x, peer_out, g2, final_g, seq, final):
    T, D = x.shape
    tm = 512
    per_seq = seq // tm
    return pl.pallas_call(
        functools.partial(_resid_kernel, final),
        grid=(T // tm,),
        in_specs=[
            pl.BlockSpec((tm, D), lambda i: (i, 0)),
            pl.BlockSpec((N_TABLE_BLOCKS, tm, SLAB_ROWS, LANES), lambda i: (0, i, 0, 0)),
            pl.BlockSpec((1, 1, D), lambda i: (i // per_seq, 0, 0)),
            pl.BlockSpec((1, D), lambda i: (0, 0)),
        ],
        out_specs=pl.BlockSpec((tm, D), lambda i: (i, 0)),
        out_shape=jax.ShapeDtypeStruct((T, D), F32),
        compiler_params=_cparams(("parallel",)),
        name="peer_residual",
    )(x, peer_out, g2, final_g)


def _pair_expand_matrix():
    p = np.arange(PAIRS)[:, None]
    c = np.arange(PAIRS * SLAB_ROWS)[None, :]
    return (p == c // SLAB_ROWS).astype(np.float32)


def kernel(x, c, rel_bias, ada_w, ada_b, norm1_g, w_in, dw_w, dw_b, conv_ln_g, conv_ln_b, w_conv_out,
           attn_sinks, w_attn_out, w_out, norm2_g, w_pq, sub_keys, peer_u, peer_v, final_g):
    B, S, D = x.shape
    L = ada_w.shape[0]
    T = B * S
    C = dw_w.shape[2]
    A = N_HEADS * HEAD_DIM
    KV = N_KV_HEADS * HEAD_DIM
    E = peer_u.shape[1]
    assert S % 1024 == 0 and D == SLAB_ROWS * LANES and E % N_TABLE_BLOCKS == 0

    o_a, o_b, o_q, o_k, o_v, o_g = 0, C, 2 * C, 2 * C + A, 2 * C + A + KV, 2 * C + A + 2 * KV
    order = np.concatenate([np.arange(o_a, o_a + 2 * C), np.arange(o_g, o_g + 2 * D),
                            np.arange(o_q, o_q + A + 2 * KV)])
    n_a, n_b, n_gc, n_ga, n_q, n_k, n_v = 0, C, 2 * C, 2 * C + D, 2 * C + 2 * D, 2 * C + 2 * D + A, 2 * C + 2 * D + A + KV

    c_pad = jnp.pad(c, ((0, 8 - B), (0, 0)))
    mod = _ada(c_pad, ada_w, ada_b)
    bias_tab = _rel_bias_table(rel_bias)
    expand = jnp.asarray(_pair_expand_matrix(), dtype=BF16)
    group_sum = jnp.asarray(_pair_expand_matrix().T, dtype=BF16)
    final_g2 = final_g.reshape(1, D)

    xt = x.reshape(T, D)
    for l in range(L):
        m = mod[l, :B].reshape(B, 6, 1, D)
        sh1, sc1, g1, sh2, sc2, g2 = (m[:, k] for k in range(6))

        w_in_l = w_in[l][:, order].astype(BF16)
        proj = _inproj(xt, norm1_g[l].reshape(1, D), sc1, sh1, w_in_l, S)
        cact = _conv(proj, dw_w[l], dw_b[l].reshape(1, C), conv_ln_g[l].reshape(1, C),
                     conv_ln_b[l].reshape(1, C), S, n_a // C, n_b // C)
        attn = _attn(proj, bias_tab, attn_sinks[l], S, n_q // A, n_k // KV, n_v // KV)
        xt = _mix(cact, attn, proj, xt, g1, w_conv_out[l].astype(BF16), w_attn_out[l].astype(BF16),
                  w_out[l].astype(BF16), S, n_gc // D, n_ga // D)

        keys = sub_keys[l].reshape(2 * PEER_HEADS, N_KEYS, PEER_HALF).astype(BF16)
        h2, st = _peer_q(xt, norm2_g[l].reshape(1, D), sc2, sh2, w_pq[l].astype(BF16), keys, S)
        idxl, gate, ovf = _route(st, E // N_TABLE_BLOCKS)
        u_tab = _pack_table(peer_u[l])
        v_tab = _pack_table(peer_v[l])
        act = _peer_u(idxl, ovf, h2, group_sum, u_tab)
        pout = _peer_v(idxl, ovf, gate, act, expand, v_tab)
        xt = _resid(xt, pout, g2, final_g2, S, l == L - 1)
    return xt.reshape(B, S, D)
```

```python
import functools
import math

import numpy as np
import jax
import jax.numpy as jnp
from jax import lax
from jax.experimental import pallas as pl
from jax.experimental.pallas import tpu as pltpu

F32 = jnp.float32
BF16 = jnp.bfloat16
I32 = jnp.int32

EPS = 1e-6
CONV_WIDTH = 31
HALO = 32
N_HEADS = 16
N_KV_HEADS = 4
HEAD_DIM = 64
BLOCK = 128
NUM_BUCKETS = 32
MAX_DISTANCE = 128
N_KEYS = 128
PEER_HEADS = 8
PEER_TOPK = 16
PEER_HALF = 128
PAIRS = PEER_HEADS * PEER_TOPK
SLAB_ROWS = 16
LANES = 128
N_TABLE_BLOCKS = 2
TOKEN_UNROLL = 16
MXU_DEPTH = 256
SLAB_WORDS = 8
WINDOW_SLOTS = 80
assert PAIRS & (PAIRS - 1) == 0 and WINDOW_SLOTS % 8 == 0 and 2 * WINDOW_SLOTS >= PAIRS
VMEM_LIMIT = 56 * 1024 * 1024


def _cparams(sem, vmem=VMEM_LIMIT):
    return pltpu.CompilerParams(dimension_semantics=sem, vmem_limit_bytes=vmem)


def _resident(block_shape, index_map):
    return pl.BlockSpec(block_shape, index_map, pipeline_mode=pl.Buffered(1))


def _ada_kernel(c_ref, w_ref, b_ref, o_ref):
    c = c_ref[...]
    cs = c * jax.nn.sigmoid(c)
    o_ref[0] = jnp.dot(cs.astype(BF16), w_ref[0].astype(BF16), preferred_element_type=F32) + b_ref[0]


def _ada(c_pad, ada_w, ada_b):
    L, D, N = ada_w.shape
    tn = 1024
    return pl.pallas_call(
        _ada_kernel,
        grid=(L, N // tn),
        in_specs=[
            pl.BlockSpec((8, D), lambda l, j: (0, 0)),
            pl.BlockSpec((1, D, tn), lambda l, j: (l, 0, j)),
            pl.BlockSpec((1, 1, tn), lambda l, j: (l, 0, j)),
        ],
        out_specs=pl.BlockSpec((1, 8, tn), lambda l, j: (l, 0, j)),
        out_shape=jax.ShapeDtypeStruct((L, 8, N), F32),
        compiler_params=_cparams(("parallel", "parallel")),
        name="ada_mod",
    )(c_pad, ada_w, ada_b.reshape(L, 1, N))


def _bias_kernel(bucket_ref, rb_ref, o_ref):
    h = pl.program_id(0)
    bucket = bucket_ref[...]
    acc = jnp.zeros(bucket.shape, F32)
    for b in range(NUM_BUCKETS):
        acc = jnp.where(bucket == b, rb_ref[b, h], acc)
    o_ref[0] = acc


def _t5_bucket_table():
    qi = np.arange(BLOCK)[:, None] + BLOCK
    kj = np.arange(2 * BLOCK)[None, :]
    d = np.maximum(qi - kj, 0)
    max_exact = NUM_BUCKETS // 2
    ratio = (np.log(np.maximum(d, 1).astype(np.float64) / max_exact)
             / math.log(MAX_DISTANCE / max_exact) * (NUM_BUCKETS - max_exact))
    large = np.minimum(max_exact + ratio.astype(np.int64), NUM_BUCKETS - 1)
    return np.where(d < max_exact, d, large).astype(np.int32)


def _rel_bias_table(rel_bias):
    bucket = jnp.asarray(_t5_bucket_table())
    return pl.pallas_call(
        _bias_kernel,
        grid=(N_HEADS,),
        in_specs=[
            pl.BlockSpec((BLOCK, 2 * BLOCK), lambda h: (0, 0)),
            pl.BlockSpec(memory_space=pltpu.SMEM),
        ],
        out_specs=pl.BlockSpec((1, BLOCK, 2 * BLOCK), lambda h: (h, 0, 0)),
        out_shape=jax.ShapeDtypeStruct((N_HEADS, BLOCK, 2 * BLOCK), F32),
        compiler_params=_cparams(("arbitrary",)),
        name="rel_bias_table",
    )(bucket, rel_bias)


def _norm_mod(x, g, sc, sh):
    ms = jnp.mean(x * x, axis=-1, keepdims=True)
    y = x * lax.rsqrt(ms + EPS)
    return (y * g) * (1.0 + sc) + sh


def _inproj_kernel(x_ref, g_ref, sc_ref, sh_ref, w_ref, o_ref, h_scr):
    @pl.when(pl.program_id(1) == 0)
    def _():
        h_scr[...] = _norm_mod(x_ref[...], g_ref[...], sc_ref[0], sh_ref[0]).astype(BF16)

    o_ref[...] = jnp.dot(h_scr[...], w_ref[...], preferred_element_type=F32)


def _inproj(x, g, sc, sh, w, seq):
    T, D = x.shape
    N = w.shape[1]
    tm, tn = 1024, 512
    per_seq = seq // tm
    return pl.pallas_call(
        _inproj_kernel,
        grid=(T // tm, N // tn),
        in_specs=[
            pl.BlockSpec((tm, D), lambda i, j: (i, 0)),
            pl.BlockSpec((1, D), lambda i, j: (0, 0)),
            pl.BlockSpec((1, 1, D), lambda i, j: (i // per_seq, 0, 0)),
            pl.BlockSpec((1, 1, D), lambda i, j: (i // per_seq, 0, 0)),
            pl.BlockSpec((D, tn), lambda i, j: (0, j)),
        ],
        out_specs=pl.BlockSpec((tm, tn), lambda i, j: (i, j)),
        out_shape=jax.ShapeDtypeStruct((T, N), F32),
        scratch_shapes=[pltpu.VMEM((tm, D), BF16)],
        compiler_params=_cparams(("parallel", "arbitrary")),
        name="in_proj",
    )(x, g, sc, sh, w)


def _conv_kernel(per_seq, a_ref, b_ref, ha_ref, hb_ref, dw_ref, db_ref, lg_ref, lb_ref, o_ref, ext):
    ts = a_ref.shape[0]
    first = (pl.program_id(0) % per_seq) == 0
    hu = ha_ref[...] * jax.nn.sigmoid(hb_ref[...])
    ext[0:HALO, :] = jnp.where(first, 0.0, hu)
    ext[HALO:, :] = a_ref[...] * jax.nn.sigmoid(b_ref[...])
    acc = jnp.zeros(a_ref.shape, F32)
    base = HALO - (CONV_WIDTH - 1)
    for w in range(CONV_WIDTH):
        acc = acc + ext[base + w:base + w + ts, :] * dw_ref[w:w + 1, :]
    acc = acc + db_ref[...]
    mu = jnp.mean(acc, axis=-1, keepdims=True)
    cen = acc - mu
    var = jnp.mean(cen * cen, axis=-1, keepdims=True)
    y = cen * lax.rsqrt(var + EPS) * lg_ref[...] + lb_ref[...]
    o_ref[...] = (y * jax.nn.sigmoid(y)).astype(BF16)


def _conv(proj, dw_w, dw_b, ln_g, ln_b, seq, a_col, b_col):
    T = proj.shape[0]
    C = dw_w.shape[1]
    ts = 256
    per_seq = seq // ts
    rb = ts // HALO
    halo_map = lambda col: (lambda i: (jnp.maximum(i * rb - 1, 0), col))
    return pl.pallas_call(
        functools.partial(_conv_kernel, per_seq),
        grid=(T // ts,),
        in_specs=[
            pl.BlockSpec((ts, C), lambda i: (i, a_col)),
            pl.BlockSpec((ts, C), lambda i: (i, b_col)),
            pl.BlockSpec((HALO, C), halo_map(a_col)),
            pl.BlockSpec((HALO, C), halo_map(b_col)),
            pl.BlockSpec((CONV_WIDTH, C), lambda i: (0, 0)),
            pl.BlockSpec((1, C), lambda i: (0, 0)),
            pl.BlockSpec((1, C), lambda i: (0, 0)),
            pl.BlockSpec((1, C), lambda i: (0, 0)),
        ],
        out_specs=pl.BlockSpec((ts, C), lambda i: (i, 0)),
        out_shape=jax.ShapeDtypeStruct((T, C), BF16),
        scratch_shapes=[pltpu.VMEM((HALO + ts, C), F32)],
        compiler_params=_cparams(("parallel",)),
        name="conv_branch",
    )(proj, proj, proj, proj, dw_w, dw_b, ln_g, ln_b)


def _attn_kernel(nb, q_ref, kc_ref, kp_ref, vc_ref, vp_ref, bias_ref, sink_ref, o_ref):
    first = (pl.program_id(0) % nb) == 0
    kk = jnp.concatenate([kp_ref[...], kc_ref[...]], axis=0).astype(BF16)
    vv = jnp.concatenate([vp_ref[...], vc_ref[...]], axis=0).astype(BF16)
    row = lax.broadcasted_iota(I32, (BLOCK, 2 * BLOCK), 0)
    col = lax.broadcasted_iota(I32, (BLOCK, 2 * BLOCK), 1)
    dist = row + BLOCK - col
    valid = (dist >= 0) & (dist < BLOCK) & (jnp.logical_not(first) | (col >= BLOCK))
    group = N_HEADS // N_KV_HEADS
    for kvh in range(N_KV_HEADS):
        k_h = kk[:, kvh * HEAD_DIM:(kvh + 1) * HEAD_DIM]
        v_h = vv[:, kvh * HEAD_DIM:(kvh + 1) * HEAD_DIM]
        qg = q_ref[:, kvh * group * HEAD_DIM:(kvh + 1) * group * HEAD_DIM].astype(BF16)
        outs = []
        for g in range(group):
            h = kvh * group + g
            q_h = qg[:, g * HEAD_DIM:(g + 1) * HEAD_DIM]
            s = lax.dot_general(q_h, k_h, (((1,), (1,)), ((), ())), preferred_element_type=F32)
            s = s * (HEAD_DIM ** -0.5) + bias_ref[h]
            s = jnp.where(valid, s, -jnp.inf)
            sink = sink_ref[h]
            m = jnp.maximum(jnp.max(s, axis=-1, keepdims=True), sink)
            e = jnp.exp(s - m)
            denom = jnp.sum(e, axis=-1, keepdims=True) + jnp.exp(sink - m)
            p = e / denom
            outs.append(jnp.dot(p.astype(BF16), v_h, preferred_element_type=F32))
        o_ref[:, kvh * group * HEAD_DIM:(kvh + 1) * group * HEAD_DIM] = (
            jnp.concatenate(outs, axis=1).astype(BF16))


def _attn(proj, bias_tab, sinks, seq, q_col, k_col, v_col):
    T = proj.shape[0]
    nb = seq // BLOCK
    attn_dim = N_HEADS * HEAD_DIM
    kv_dim = N_KV_HEADS * HEAD_DIM
    prev = lambda col: (lambda i: (jnp.where(i % nb == 0, i, i - 1), col))
    return pl.pallas_call(
        functools.partial(_attn_kernel, nb),
        grid=(T // BLOCK,),
        in_specs=[
            pl.BlockSpec((BLOCK, attn_dim), lambda i: (i, q_col)),
            pl.BlockSpec((BLOCK, kv_dim), lambda i: (i, k_col)),
            pl.BlockSpec((BLOCK, kv_dim), prev(k_col)),
            pl.BlockSpec((BLOCK, kv_dim), lambda i: (i, v_col)),
            pl.BlockSpec((BLOCK, kv_dim), prev(v_col)),
            pl.BlockSpec((N_HEADS, BLOCK, 2 * BLOCK), lambda i: (0, 0, 0)),
            pl.BlockSpec(memory_space=pltpu.SMEM),
        ],
        out_specs=pl.BlockSpec((BLOCK, attn_dim), lambda i: (i, 0)),
        out_shape=jax.ShapeDtypeStruct((T, attn_dim), BF16),
        compiler_params=_cparams(("parallel",)),
        name="swa_attention",
    )(proj, proj, proj, proj, proj, bias_tab, sinks)


def _mix_kernel(ca_ref, at_ref, gc_ref, ga_ref, x_ref, g1_ref, wc_ref, wa_ref, wo_ref, o_ref):
    yc = jnp.dot(ca_ref[...], wc_ref[...], preferred_element_type=F32)
    ya = jnp.dot(at_ref[...], wa_ref[...], preferred_element_type=F32)
    mixed = jax.nn.sigmoid(gc_ref[...]) * yc + jax.nn.sigmoid(ga_ref[...]) * ya
    y = jnp.dot(mixed.astype(BF16), wo_ref[...], preferred_element_type=F32)
    o_ref[...] = x_ref[...] + g1_ref[0] * y


def _mix(cact, attn, proj, x, g1, wc, wa, wo, seq, gc_col, ga_col):
    T, D = x.shape
    C = cact.shape[1]
    A = attn.shape[1]
    tm = 256
    per_seq = seq // tm
    return pl.pallas_call(
        _mix_kernel,
        grid=(T // tm,),
        in_specs=[
            pl.BlockSpec((tm, C), lambda i: (i, 0)),
            pl.BlockSpec((tm, A), lambda i: (i, 0)),
            pl.BlockSpec((tm, D), lambda i: (i, gc_col)),
            pl.BlockSpec((tm, D), lambda i: (i, ga_col)),
            pl.BlockSpec((tm, D), lambda i: (i, 0)),
            pl.BlockSpec((1, 1, D), lambda i: (i // per_seq, 0, 0)),
            _resident((C, D), lambda i: (0, 0)),
            _resident((A, D), lambda i: (0, 0)),
            _resident((D, D), lambda i: (0, 0)),
        ],
        out_specs=pl.BlockSpec((tm, D), lambda i: (i, 0)),
        out_shape=jax.ShapeDtypeStruct((T, D), F32),
        compiler_params=_cparams(("parallel",)),
        name="merge_out_proj",
    )(cact, attn, proj, proj, x, g1, wc, wa, wo)


def _store_packed_slabs(o_ref, xb):
    w = lax.bitcast_convert_type(xb.astype(F32), jnp.uint32)
    for i in range(SLAB_WORDS):
        lo = w[:, (2 * i) * LANES:(2 * i + 1) * LANES]
        hi = w[:, (2 * i + 1) * LANES:(2 * i + 2) * LANES]
        o_ref[:, i, :] = hi | (lo >> 16)


def _pack_kernel(x_ref, o_ref):
    _store_packed_slabs(o_ref, x_ref[0].astype(BF16))


def _pack_table(tabs, layer):
    _, n, d = tabs.shape
    te = 256
    packed = pl.pallas_call(
        _pack_kernel,
        grid=(n // te,),
        in_specs=[pl.BlockSpec((1, te, d), lambda i: (layer, i, 0))],
        out_specs=pl.BlockSpec((te, SLAB_WORDS, LANES), lambda i: (i, 0, 0)),
        out_shape=jax.ShapeDtypeStruct((n, SLAB_WORDS, LANES), jnp.uint32),
        compiler_params=_cparams(("parallel",)),
        name="pack_expert_table",
    )(tabs)
    return packed.reshape(n * SLAB_WORDS, LANES)


def _peer_q_kernel(x_ref, g_ref, sc_ref, sh_ref, w_ref, k_ref, h_ref, st_ref):
    hb = _norm_mod(x_ref[...], g_ref[...], sc_ref[0], sh_ref[0]).astype(BF16)
    _store_packed_slabs(h_ref, hb)
    qb = jnp.dot(hb, w_ref[...], preferred_element_type=F32).astype(BF16)
    for hp in range(2 * PEER_HEADS):
        q_hp = qb[:, hp * PEER_HALF:(hp + 1) * PEER_HALF]
        st_ref[hp * N_KEYS:(hp + 1) * N_KEYS, :] = lax.dot_general(
            k_ref[hp], q_hp, (((1,), (1,)), ((), ())), preferred_element_type=F32)


def _peer_q(x, g, sc, sh, w_pq, keys, seq):
    T, D = x.shape
    Q = w_pq.shape[1]
    tm = 256
    per_seq = seq // tm
    n_rows = 2 * PEER_HEADS * N_KEYS
    return pl.pallas_call(
        _peer_q_kernel,
        grid=(T // tm,),
        in_specs=[
            pl.BlockSpec((tm, D), lambda i: (i, 0)),
            pl.BlockSpec((1, D), lambda i: (0, 0)),
            pl.BlockSpec((1, 1, D), lambda i: (i // per_seq, 0, 0)),
            pl.BlockSpec((1, 1, D), lambda i: (i // per_seq, 0, 0)),
            _resident((D, Q), lambda i: (0, 0)),
            _resident((2 * PEER_HEADS, N_KEYS, PEER_HALF), lambda i: (0, 0, 0)),
        ],
        out_specs=[
            pl.BlockSpec((tm, SLAB_WORDS, LANES), lambda i: (i, 0, 0)),
            pl.BlockSpec((n_rows, tm), lambda i: (0, i)),
        ],
        out_shape=[
            jax.ShapeDtypeStruct((T, SLAB_WORDS, LANES), jnp.uint32),
            jax.ShapeDtypeStruct((n_rows, T), F32),
        ],
        compiler_params=_cparams(("parallel",)),
        name="peer_query_scores",
    )(x, g, sc, sh, w_pq, keys)


def _top_rows(x, payload, k, rank=None):
    rows = lax.broadcasted_iota(I32, x.shape, 0) if rank is None else rank
    vals, pays = [], []
    for _ in range(k):
        m = jnp.max(x, axis=0, keepdims=True)
        first = jnp.min(jnp.where(x == m, rows, jnp.iinfo(jnp.int32).max), axis=0, keepdims=True)
        onehot = rows == first
        if payload is None:
            pays.append(first)
        else:
            pays.append(jnp.sum(jnp.where(onehot, payload, 0), axis=0, keepdims=True))
        vals.append(m)
        x = jnp.where(onehot, -jnp.inf, x)
    return jnp.concatenate(vals, axis=0), jnp.concatenate(pays, axis=0)


def _sort_rows(keys, pays):
    n_groups = len(keys)
    n_rows = 8 * n_groups
    sub = lax.broadcasted_iota(I32, keys[0].shape, 0)
    k = 2
    while k <= n_rows:
        j = k // 2
        while j >= 1:
            if j >= 8:
                gj = j // 8
                for lo in range(n_groups):
                    if lo & gj:
                        continue
                    hi = lo | gj
                    ascending = ((8 * lo) & k) == 0
                    a, b = keys[lo], keys[hi]
                    swap = (a > b) if ascending else (a < b)
                    keys[lo], keys[hi] = jnp.where(swap, b, a), jnp.where(swap, a, b)
                    pa, pb = pays[lo], pays[hi]
                    pays[lo], pays[hi] = jnp.where(swap, pb, pa), jnp.where(swap, pa, pb)
            else:
                lower = (sub & j) == 0
                upper = (sub & j) != 0
                for g in range(n_groups):
                    if k >= 8:
                        ascending = ((8 * g) & k) == 0
                        want_max = upper if ascending else lower
                    else:
                        want_max = jnp.logical_xor(upper, (sub & k) != 0)
                    x, p = keys[g], pays[g]
                    xp = jnp.where(lower, pltpu.roll(x, 8 - j, 0), pltpu.roll(x, j, 0))
                    pp = jnp.where(lower, pltpu.roll(p, 8 - j, 0), pltpu.roll(p, j, 0))
                    swap = jnp.logical_xor(xp < x, want_max)
                    keys[g] = jnp.where(swap, xp, x)
                    pays[g] = jnp.where(swap, pp, p)
            j //= 2
        k *= 2
    return keys, pays


def _route_kernel(block_rows, st_ref, idxl_ref, gate_ref, ovf_ref, idx_scr, gate_scr):
    def head(h, carry):
        base = pl.multiple_of(h * (2 * N_KEYS), 2 * N_KEYS)
        v0, i0 = _top_rows(st_ref[pl.ds(base, N_KEYS), :], None, PEER_TOPK)
        v1, i1 = _top_rows(st_ref[pl.ds(base + N_KEYS, N_KEYS), :], None, PEER_TOPK)
        half = PEER_TOPK // 2
        sub = lax.broadcasted_iota(I32, (half, v0.shape[1]), 0)
        cand = [v0[k:k + 1] + v1[:half] for k in range(half)]
        cidx = [i0[k:k + 1] * N_KEYS + i1[:half] for k in range(half)]
        flat = [sub + k * PEER_TOPK for k in range(half)]
        cand += [v0[:1] + v1[half:], v0[half:] + v1[:1]]
        cidx += [i0[:1] * N_KEYS + i1[half:], i0[half:] * N_KEYS + i1[:1]]
        flat += [sub + half, (sub + half) * PEER_TOPK]
        best, eid = _top_rows(jnp.concatenate(cand, axis=0), jnp.concatenate(cidx, axis=0), PEER_TOPK,
                              rank=jnp.concatenate(flat, axis=0))
        e = jnp.exp(best - jnp.max(best, axis=0, keepdims=True))
        gates = e / jnp.sum(e, axis=0, keepdims=True)
        off = pl.multiple_of(h * PEER_TOPK, PEER_TOPK)
        idx_scr[pl.ds(off, PEER_TOPK), :] = eid
        gate_scr[pl.ds(off, PEER_TOPK), :] = gates
        return carry

    lax.fori_loop(0, PEER_HEADS, head, 0)

    n_groups = PAIRS // 8
    sub = lax.broadcasted_iota(I32, (8, idx_scr.shape[1]), 0)
    keys = [idx_scr[8 * g:8 * g + 8, :] * PAIRS + (sub + 8 * g) for g in range(n_groups)]
    pays = [gate_scr[8 * g:8 * g + 8, :] for g in range(n_groups)]
    keys, pays = _sort_rows(keys, pays)
    eids = [jnp.right_shift(k, PAIRS.bit_length() - 1) for k in keys]
    row_hi, row_lo = WINDOW_SLOTS, PAIRS - WINDOW_SLOTS - 1
    ovf0 = jnp.where(eids[row_hi // 8][row_hi % 8:row_hi % 8 + 1, :] < block_rows, 1, 0)
    ovf1 = jnp.where(eids[row_lo // 8][row_lo % 8:row_lo % 8 + 1, :] >= block_rows, 1, 0)
    ovf_ref[...] = jnp.concatenate([ovf0, ovf1], axis=0).astype(I32)
    rot = (PAIRS - WINDOW_SLOTS) // 8
    for b in range(N_TABLE_BLOCKS):
        order = [(g + b * rot) % n_groups for g in range(n_groups)]
        e = jnp.concatenate([eids[g] for g in order], axis=0)
        inb = (e >= b * block_rows) & (e < (b + 1) * block_rows)
        local = jnp.where(inb, (e - b * block_rows) * SLAB_WORDS, 0)
        gate = jnp.where(inb, jnp.concatenate([pays[g] for g in order], axis=0), 0.0)
        idxl_ref[b] = local.astype(F32).T.astype(I32)
        gate_ref[b] = gate.T


def _route(st, block_rows):
    n_rows, T = st.shape
    tr = 128
    return pl.pallas_call(
        functools.partial(_route_kernel, block_rows),
        grid=(T // tr,),
        in_specs=[pl.BlockSpec((n_rows, tr), lambda i: (0, i))],
        out_specs=[
            pl.BlockSpec((N_TABLE_BLOCKS, tr, PAIRS), lambda i: (0, i, 0)),
            pl.BlockSpec((N_TABLE_BLOCKS, tr, PAIRS), lambda i: (0, i, 0)),
            pl.BlockSpec((N_TABLE_BLOCKS, tr), lambda i: (0, i)),
        ],
        out_shape=[
            jax.ShapeDtypeStruct((N_TABLE_BLOCKS, T, PAIRS), I32),
            jax.ShapeDtypeStruct((N_TABLE_BLOCKS, T, PAIRS), F32),
            jax.ShapeDtypeStruct((N_TABLE_BLOCKS, T), I32),
        ],
        scratch_shapes=[pltpu.VMEM((PAIRS, tr), I32), pltpu.VMEM((PAIRS, tr), F32)],
        compiler_params=_cparams(("parallel",)),
        name="peer_route",
    )(st)


def _diag_mask():
    r = lax.broadcasted_iota(I32, (SLAB_ROWS, PAIRS * SLAB_ROWS), 0)
    c = lax.broadcasted_iota(I32, (SLAB_ROWS, PAIRS * SLAB_ROWS), 1)
    return r == (c % SLAB_ROWS)


def _for_overflow_tokens(ovf_ref, b, tm, body):
    group = 8

    def step(g, carry):
        flags = [ovf_ref[b, g * group + i] for i in range(group)]
        any_set = functools.reduce(lambda x, y: x | y, flags)

        @pl.when(any_set != 0)
        def _():
            def one(i, c):
                t = g * group + i

                @pl.when(ovf_ref[b, t] != 0)
                def _():
                    body(t)
                return c

            lax.fori_loop(0, group, one, 0)
        return carry

    lax.fori_loop(0, tm // group, step, 0)


def _gather_slabs(tab_ref, idx_ref, t, slots):
    words = [tab_ref[pl.ds(pl.multiple_of(idx_ref[0, t, j], SLAB_WORDS), SLAB_WORDS), :] for j in slots]
    return pltpu.bitcast(jnp.concatenate(words, axis=0), BF16)


def _peer_u_kernel(idx_ref, ovf_ref, h_ref, sel_ref, u_ref, o_ref, part_scr):
    tm = h_ref.shape[0]
    b = pl.program_id(0)
    sel = sel_ref[...]
    per_tile = sel.shape[1] // SLAB_ROWS

    def lane_partials(t, slots):
        h = pltpu.bitcast(h_ref[t], BF16)
        prods = [pltpu.bitcast(
            u_ref[pl.ds(pl.multiple_of(idx_ref[0, t, j], SLAB_WORDS), SLAB_WORDS), :], BF16) * h for j in slots]
        for k in range(0, len(prods), per_tile):
            part_scr[t, slots[k]:slots[k] + per_tile, :] = jnp.dot(
                sel, jnp.concatenate(prods[k:k + per_tile], axis=0), preferred_element_type=F32)

    def tok(t, carry):
        lane_partials(t, range(WINDOW_SLOTS))
        return carry

    def rare(t):
        lane_partials(t, range(WINDOW_SLOTS, PAIRS))

    part_scr[:, WINDOW_SLOTS:, :] = jnp.zeros((tm, PAIRS - WINDOW_SLOTS, LANES), F32)
    lax.fori_loop(0, tm, tok, 0, unroll=TOKEN_UNROLL)
    _for_overflow_tokens(ovf_ref, b, tm, rare)

    chunk = 8

    def finish(c, carry):
        r = pl.multiple_of(c * chunk, chunk)
        o_ref[0, pl.ds(r, chunk), :] = jnp.sum(part_scr[pl.ds(r, chunk)], axis=-1)
        return carry

    lax.fori_loop(0, tm // chunk, finish, 0)


def _peer_u(idxl, ovf, h_slab, row_sum, u_tab):
    T = h_slab.shape[0]
    block_words = u_tab.shape[0] // N_TABLE_BLOCKS
    tm = 128
    return pl.pallas_call(
        _peer_u_kernel,
        grid=(N_TABLE_BLOCKS, T // tm),
        in_specs=[
            pl.BlockSpec((1, tm, PAIRS), lambda b, i: (b, i, 0), memory_space=pltpu.SMEM),
            pl.BlockSpec((N_TABLE_BLOCKS, tm), lambda b, i: (0, i), memory_space=pltpu.SMEM),
            pl.BlockSpec((tm, SLAB_WORDS, LANES), lambda b, i: (i, 0, 0)),
            _resident(row_sum.shape, lambda b, i: (0, 0)),
            _resident((block_words, LANES), lambda b, i: (b, 0)),
        ],
        out_specs=pl.BlockSpec((1, tm, PAIRS), lambda b, i: (b, i, 0)),
        out_shape=jax.ShapeDtypeStruct((N_TABLE_BLOCKS, T, PAIRS), F32),
        scratch_shapes=[pltpu.VMEM((tm, PAIRS, LANES), F32)],
        compiler_params=_cparams(("arbitrary", "arbitrary")),
        name="peer_expert_u",
    )(idxl, ovf, h_slab, row_sum, u_tab)


def _peer_v_kernel(idx_ref, ovf_ref, gate_ref, act_ref, ex_ref, v_ref, o_ref, wexp):
    tm = gate_ref.shape[1]
    b = pl.program_id(0)
    act = act_ref[0]
    gelu = 0.5 * act * (1.0 + lax.erf(act * np.float32(math.sqrt(0.5))))
    w = gate_ref[0] * gelu
    wexp[...] = jnp.dot(w.astype(BF16), ex_ref[...], preferred_element_type=F32)
    eye = _diag_mask()
    wcols = WINDOW_SLOTS * SLAB_ROWS

    def weighted(t, slots, lo, hi):
        vsel = _gather_slabs(v_ref, idx_ref, t, slots)
        wrow = jnp.broadcast_to(wexp[pl.ds(t, 1), lo:hi], (SLAB_ROWS, hi - lo))
        wbd = jnp.where(eye[:, lo:hi], wrow, 0.0).astype(BF16)
        return jnp.dot(wbd, vsel, preferred_element_type=F32)

    def tok(t, carry):
        o_ref[0, t] = weighted(t, range(WINDOW_SLOTS), 0, wcols)
        return carry

    def rare(t):
        o_ref[0, t] = o_ref[0, t] + weighted(t, range(WINDOW_SLOTS, PAIRS), wcols, PAIRS * SLAB_ROWS)

    lax.fori_loop(0, tm, tok, 0, unroll=TOKEN_UNROLL)
    _for_overflow_tokens(ovf_ref, b, tm, rare)


def _peer_v(idxl, ovf, gate, act, expand, v_tab):
    T = gate.shape[1]
    block_words = v_tab.shape[0] // N_TABLE_BLOCKS
    tm = 128
    return pl.pallas_call(
        _peer_v_kernel,
        grid=(N_TABLE_BLOCKS, T // tm),
        in_specs=[
            pl.BlockSpec((1, tm, PAIRS), lambda b, i: (b, i, 0), memory_space=pltpu.SMEM),
            pl.BlockSpec((N_TABLE_BLOCKS, tm), lambda b, i: (0, i), memory_space=pltpu.SMEM),
            pl.BlockSpec((1, tm, PAIRS), lambda b, i: (b, i, 0)),
            pl.BlockSpec((1, tm, PAIRS), lambda b, i: (b, i, 0)),
            _resident((PAIRS, PAIRS * SLAB_ROWS), lambda b, i: (0, 0)),
            _resident((block_words, LANES), lambda b, i: (b, 0)),
        ],
        out_specs=pl.BlockSpec((1, tm, SLAB_ROWS, LANES), lambda b, i: (b, i, 0, 0)),
        out_shape=jax.ShapeDtypeStruct((N_TABLE_BLOCKS, T, SLAB_ROWS, LANES), F32),
        scratch_shapes=[pltpu.VMEM((tm, PAIRS * SLAB_ROWS), F32)],
        compiler_params=_cparams(("arbitrary", "arbitrary")),
        name="peer_expert_v",
    )(idxl, ovf, gate, act, expand, v_tab)


def _resid_kernel(final, x_ref, o_ref, g2_ref, fg_ref, y_ref):
    cols = []
    for s in range(SLAB_ROWS):
        part = o_ref[0, :, s, :]
        for k in range(1, N_TABLE_BLOCKS):
            part = part + o_ref[k, :, s, :]
        cols.append(part)
    x = x_ref[...] + g2_ref[0] * jnp.concatenate(cols, axis=1)
    if final:
        ms = jnp.mean(x * x, axis=-1, keepdims=True)
        x = (x * lax.rsqrt(ms + EPS)) * fg_ref[...]
    y_ref[...] = x


def _resid(x, peer_out, g2, final_g, seq, final):
    T, D = x.shape
    tm = 512
    per_seq = seq // tm
    return pl.pallas_call(
        functools.partial(_resid_kernel, final),
        grid=(T // tm,),
        in_specs=[
            pl.BlockSpec((tm, D), lambda i: (i, 0)),
            pl.BlockSpec((N_TABLE_BLOCKS, tm, SLAB_ROWS, LANES), lambda i: (0, i, 0, 0)),
            pl.BlockSpec((1, 1, D), lambda i: (i // per_seq, 0, 0)),
            pl.BlockSpec((1, D), lambda i: (0, 0)),
        ],
        out_specs=pl.BlockSpec((tm, D), lambda i: (i, 0)),
        out_shape=jax.ShapeDtypeStruct((T, D), F32),
        compiler_params=_cparams(("parallel",)),
        name="peer_residual",
    )(x, peer_out, g2, final_g)


def _pair_expand_matrix():
    p = np.arange(PAIRS)[:, None]
    c = np.arange(PAIRS * SLAB_ROWS)[None, :]
    return (p == c // SLAB_ROWS).astype(np.float32)


def kernel(x, c, rel_bias, ada_w, ada_b, norm1_g, w_in, dw_w, dw_b, conv_ln_g, conv_ln_b, w_conv_out,
           attn_sinks, w_attn_out, w_out, norm2_g, w_pq, sub_keys, peer_u, peer_v, final_g):
    B, S, D = x.shape
    L = ada_w.shape[0]
    T = B * S
    C = dw_w.shape[2]
    A = N_HEADS * HEAD_DIM
    KV = N_KV_HEADS * HEAD_DIM
    E = peer_u.shape[1]
    assert S % 1024 == 0 and D == SLAB_ROWS * LANES and E % N_TABLE_BLOCKS == 0

    o_a, o_b, o_q, o_k, o_v, o_g = 0, C, 2 * C, 2 * C + A, 2 * C + A + KV, 2 * C + A + 2 * KV
    order = np.concatenate([np.arange(o_a, o_a + 2 * C), np.arange(o_g, o_g + 2 * D),
                            np.arange(o_q, o_q + A + 2 * KV)])
    n_a, n_b, n_gc, n_ga, n_q, n_k, n_v = 0, C, 2 * C, 2 * C + D, 2 * C + 2 * D, 2 * C + 2 * D + A, 2 * C + 2 * D + A + KV

    c_pad = jnp.pad(c, ((0, 8 - B), (0, 0)))
    mod = _ada(c_pad, ada_w, ada_b)
    bias_tab = _rel_bias_table(rel_bias)
    expand = jnp.asarray(_pair_expand_matrix(), dtype=BF16)
    row_sum = jnp.asarray(_pair_expand_matrix()[:MXU_DEPTH // SLAB_ROWS, :MXU_DEPTH], dtype=BF16)
    final_g2 = final_g.reshape(1, D)

    xt = x.reshape(T, D)
    for l in range(L):
        m = mod[l, :B].reshape(B, 6, 1, D)
        sh1, sc1, g1, sh2, sc2, g2 = (m[:, k] for k in range(6))

        w_in_l = w_in[l][:, order].astype(BF16)
        proj = _inproj(xt, norm1_g[l].reshape(1, D), sc1, sh1, w_in_l, S)
        cact = _conv(proj, dw_w[l], dw_b[l].reshape(1, C), conv_ln_g[l].reshape(1, C),
                     conv_ln_b[l].reshape(1, C), S, n_a // C, n_b // C)
        attn = _attn(proj, bias_tab, attn_sinks[l], S, n_q // A, n_k // KV, n_v // KV)
        xt = _mix(cact, attn, proj, xt, g1, w_conv_out[l].astype(BF16), w_attn_out[l].astype(BF16),
                  w_out[l].astype(BF16), S, n_gc // D, n_ga // D)

        keys = sub_keys[l].reshape(2 * PEER_HEADS, N_KEYS, PEER_HALF).astype(BF16)
        h2, st = _peer_q(xt, norm2_g[l].reshape(1, D), sc2, sh2, w_pq[l].astype(BF16), keys, S)
        idxl, gate, ovf = _route(st, E // N_TABLE_BLOCKS)
        u_tab = _pack_table(peer_u, l)
        v_tab = _pack_table(peer_v, l)
        act = _peer_u(idxl, ovf, h2, row_sum, u_tab)
        pout = _peer_v(idxl, ovf, gate, act, expand, v_tab)
        xt = _resid(xt, pout, g2, final_g2, S, l == L - 1)
    return xt.reshape(B, S, D)
```

```python
import functools
import math

import numpy as np
import jax
import jax.numpy as jnp
from jax import lax
from jax.experimental import pallas as pl
from jax.experimental.pallas import tpu as pltpu

F32 = jnp.float32
BF16 = jnp.bfloat16
I32 = jnp.int32

EPS = 1e-6
CONV_WIDTH = 31
HALO = 32
N_HEADS = 16
N_KV_HEADS = 4
HEAD_DIM = 64
BLOCK = 128
NUM_BUCKETS = 32
MAX_DISTANCE = 128
N_KEYS = 128
PEER_HEADS = 8
PEER_TOPK = 16
PEER_HALF = 128
PAIRS = PEER_HEADS * PEER_TOPK
SLAB_ROWS = 16
LANES = 128
N_TABLE_BLOCKS = 2
TOKEN_UNROLL = 16
HEAD_UNROLL = 4
MXU_DEPTH = 256
SLAB_WORDS = 8
WINDOW_SLOTS = 80
assert PAIRS & (PAIRS - 1) == 0 and WINDOW_SLOTS % 8 == 0 and 2 * WINDOW_SLOTS >= PAIRS
VMEM_LIMIT = 56 * 1024 * 1024


def _cparams(sem, vmem=VMEM_LIMIT):
    return pltpu.CompilerParams(dimension_semantics=sem, vmem_limit_bytes=vmem)


def _resident(block_shape, index_map):
    return pl.BlockSpec(block_shape, index_map, pipeline_mode=pl.Buffered(1))


def _ada_kernel(c_ref, w_ref, b_ref, o_ref):
    c = c_ref[...]
    cs = c * jax.nn.sigmoid(c)
    o_ref[0] = jnp.dot(cs.astype(BF16), w_ref[0].astype(BF16), preferred_element_type=F32) + b_ref[0]


def _ada(c_pad, ada_w, ada_b):
    L, D, N = ada_w.shape
    tn = 1024
    return pl.pallas_call(
        _ada_kernel,
        grid=(L, N // tn),
        in_specs=[
            pl.BlockSpec((8, D), lambda l, j: (0, 0)),
            pl.BlockSpec((1, D, tn), lambda l, j: (l, 0, j)),
            pl.BlockSpec((1, 1, tn), lambda l, j: (l, 0, j)),
        ],
        out_specs=pl.BlockSpec((1, 8, tn), lambda l, j: (l, 0, j)),
        out_shape=jax.ShapeDtypeStruct((L, 8, N), F32),
        compiler_params=_cparams(("parallel", "parallel")),
        name="ada_mod",
    )(c_pad, ada_w, ada_b.reshape(L, 1, N))


def _bias_kernel(bucket_ref, rb_ref, o_ref):
    h = pl.program_id(0)
    bucket = bucket_ref[...]
    acc = jnp.zeros(bucket.shape, F32)
    for b in range(NUM_BUCKETS):
        acc = jnp.where(bucket == b, rb_ref[b, h], acc)
    o_ref[0] = acc


def _t5_bucket_table():
    qi = np.arange(BLOCK)[:, None] + BLOCK
    kj = np.arange(2 * BLOCK)[None, :]
    d = np.maximum(qi - kj, 0)
    max_exact = NUM_BUCKETS // 2
    ratio = (np.log(np.maximum(d, 1).astype(np.float64) / max_exact)
             / math.log(MAX_DISTANCE / max_exact) * (NUM_BUCKETS - max_exact))
    large = np.minimum(max_exact + ratio.astype(np.int64), NUM_BUCKETS - 1)
    return np.where(d < max_exact, d, large).astype(np.int32)


def _rel_bias_table(rel_bias):
    bucket = jnp.asarray(_t5_bucket_table())
    return pl.pallas_call(
        _bias_kernel,
        grid=(N_HEADS,),
        in_specs=[
            pl.BlockSpec((BLOCK, 2 * BLOCK), lambda h: (0, 0)),
            pl.BlockSpec(memory_space=pltpu.SMEM),
        ],
        out_specs=pl.BlockSpec((1, BLOCK, 2 * BLOCK), lambda h: (h, 0, 0)),
        out_shape=jax.ShapeDtypeStruct((N_HEADS, BLOCK, 2 * BLOCK), F32),
        compiler_params=_cparams(("arbitrary",)),
        name="rel_bias_table",
    )(bucket, rel_bias)


def _norm_mod(x, g, sc, sh):
    ms = jnp.mean(x * x, axis=-1, keepdims=True)
    y = x * lax.rsqrt(ms + EPS)
    return (y * g) * (1.0 + sc) + sh


def _inproj_kernel(x_ref, g_ref, sc_ref, sh_ref, w_ref, o_ref, h_scr):
    @pl.when(pl.program_id(1) == 0)
    def _():
        h_scr[...] = _norm_mod(x_ref[...], g_ref[...], sc_ref[0], sh_ref[0]).astype(BF16)

    o_ref[...] = jnp.dot(h_scr[...], w_ref[...], preferred_element_type=F32)


def _inproj(x, g, sc, sh, w, seq):
    T, D = x.shape
    N = w.shape[1]
    tm, tn = 1024, 512
    per_seq = seq // tm
    return pl.pallas_call(
        _inproj_kernel,
        grid=(T // tm, N // tn),
        in_specs=[
            pl.BlockSpec((tm, D), lambda i, j: (i, 0)),
            pl.BlockSpec((1, D), lambda i, j: (0, 0)),
            pl.BlockSpec((1, 1, D), lambda i, j: (i // per_seq, 0, 0)),
            pl.BlockSpec((1, 1, D), lambda i, j: (i // per_seq, 0, 0)),
            pl.BlockSpec((D, tn), lambda i, j: (0, j)),
        ],
        out_specs=pl.BlockSpec((tm, tn), lambda i, j: (i, j)),
        out_shape=jax.ShapeDtypeStruct((T, N), F32),
        scratch_shapes=[pltpu.VMEM((tm, D), BF16)],
        compiler_params=_cparams(("parallel", "arbitrary")),
        name="in_proj",
    )(x, g, sc, sh, w)


def _conv_kernel(per_seq, a_ref, b_ref, ha_ref, hb_ref, dw_ref, db_ref, lg_ref, lb_ref, o_ref, ext):
    ts = a_ref.shape[0]
    first = (pl.program_id(0) % per_seq) == 0
    hu = ha_ref[...] * jax.nn.sigmoid(hb_ref[...])
    ext[0:HALO, :] = jnp.where(first, 0.0, hu)
    ext[HALO:, :] = a_ref[...] * jax.nn.sigmoid(b_ref[...])
    acc = jnp.zeros(a_ref.shape, F32)
    base = HALO - (CONV_WIDTH - 1)
    for w in range(CONV_WIDTH):
        acc = acc + ext[base + w:base + w + ts, :] * dw_ref[w:w + 1, :]
    acc = acc + db_ref[...]
    mu = jnp.mean(acc, axis=-1, keepdims=True)
    cen = acc - mu
    var = jnp.mean(cen * cen, axis=-1, keepdims=True)
    y = cen * lax.rsqrt(var + EPS) * lg_ref[...] + lb_ref[...]
    o_ref[...] = (y * jax.nn.sigmoid(y)).astype(BF16)


def _conv(proj, dw_w, dw_b, ln_g, ln_b, seq, a_col, b_col):
    T = proj.shape[0]
    C = dw_w.shape[1]
    ts = 256
    per_seq = seq // ts
    rb = ts // HALO
    halo_map = lambda col: (lambda i: (jnp.maximum(i * rb - 1, 0), col))
    return pl.pallas_call(
        functools.partial(_conv_kernel, per_seq),
        grid=(T // ts,),
        in_specs=[
            pl.BlockSpec((ts, C), lambda i: (i, a_col)),
            pl.BlockSpec((ts, C), lambda i: (i, b_col)),
            pl.BlockSpec((HALO, C), halo_map(a_col)),
            pl.BlockSpec((HALO, C), halo_map(b_col)),
            pl.BlockSpec((CONV_WIDTH, C), lambda i: (0, 0)),
            pl.BlockSpec((1, C), lambda i: (0, 0)),
            pl.BlockSpec((1, C), lambda i: (0, 0)),
            pl.BlockSpec((1, C), lambda i: (0, 0)),
        ],
        out_specs=pl.BlockSpec((ts, C), lambda i: (i, 0)),
        out_shape=jax.ShapeDtypeStruct((T, C), BF16),
        scratch_shapes=[pltpu.VMEM((HALO + ts, C), F32)],
        compiler_params=_cparams(("parallel",)),
        name="conv_branch",
    )(proj, proj, proj, proj, dw_w, dw_b, ln_g, ln_b)


def _attn_kernel(nb, q_ref, kc_ref, kp_ref, vc_ref, vp_ref, bias_ref, sink_ref, o_ref):
    first = (pl.program_id(0) % nb) == 0
    kk = jnp.concatenate([kp_ref[...], kc_ref[...]], axis=0).astype(BF16)
    vv = jnp.concatenate([vp_ref[...], vc_ref[...]], axis=0).astype(BF16)
    row = lax.broadcasted_iota(I32, (BLOCK, 2 * BLOCK), 0)
    col = lax.broadcasted_iota(I32, (BLOCK, 2 * BLOCK), 1)
    dist = row + BLOCK - col
    valid = (dist >= 0) & (dist < BLOCK) & (jnp.logical_not(first) | (col >= BLOCK))
    group = N_HEADS // N_KV_HEADS
    for kvh in range(N_KV_HEADS):
        k_h = kk[:, kvh * HEAD_DIM:(kvh + 1) * HEAD_DIM]
        v_h = vv[:, kvh * HEAD_DIM:(kvh + 1) * HEAD_DIM]
        qg = q_ref[:, kvh * group * HEAD_DIM:(kvh + 1) * group * HEAD_DIM].astype(BF16)
        outs = []
        for g in range(group):
            h = kvh * group + g
            q_h = qg[:, g * HEAD_DIM:(g + 1) * HEAD_DIM]
            s = lax.dot_general(q_h, k_h, (((1,), (1,)), ((), ())), preferred_element_type=F32)
            s = s * (HEAD_DIM ** -0.5) + bias_ref[h]
            s = jnp.where(valid, s, -jnp.inf)
            sink = sink_ref[h]
            m = jnp.maximum(jnp.max(s, axis=-1, keepdims=True), sink)
            e = jnp.exp(s - m)
            denom = jnp.sum(e, axis=-1, keepdims=True) + jnp.exp(sink - m)
            p = e / denom
            outs.append(jnp.dot(p.astype(BF16), v_h, preferred_element_type=F32))
        o_ref[:, kvh * group * HEAD_DIM:(kvh + 1) * group * HEAD_DIM] = (
            jnp.concatenate(outs, axis=1).astype(BF16))


def _attn(proj, bias_tab, sinks, seq, q_col, k_col, v_col):
    T = proj.shape[0]
    nb = seq // BLOCK
    attn_dim = N_HEADS * HEAD_DIM
    kv_dim = N_KV_HEADS * HEAD_DIM
    prev = lambda col: (lambda i: (jnp.where(i % nb == 0, i, i - 1), col))
    return pl.pallas_call(
        functools.partial(_attn_kernel, nb),
        grid=(T // BLOCK,),
        in_specs=[
            pl.BlockSpec((BLOCK, attn_dim), lambda i: (i, q_col)),
            pl.BlockSpec((BLOCK, kv_dim), lambda i: (i, k_col)),
            pl.BlockSpec((BLOCK, kv_dim), prev(k_col)),
            pl.BlockSpec((BLOCK, kv_dim), lambda i: (i, v_col)),
            pl.BlockSpec((BLOCK, kv_dim), prev(v_col)),
            pl.BlockSpec((N_HEADS, BLOCK, 2 * BLOCK), lambda i: (0, 0, 0)),
            pl.BlockSpec(memory_space=pltpu.SMEM),
        ],
        out_specs=pl.BlockSpec((BLOCK, attn_dim), lambda i: (i, 0)),
        out_shape=jax.ShapeDtypeStruct((T, attn_dim), BF16),
        compiler_params=_cparams(("parallel",)),
        name="swa_attention",
    )(proj, proj, proj, proj, proj, bias_tab, sinks)


def _mix_kernel(ca_ref, at_ref, gc_ref, ga_ref, x_ref, g1_ref, wc_ref, wa_ref, wo_ref, o_ref):
    yc = jnp.dot(ca_ref[...], wc_ref[...], preferred_element_type=F32)
    ya = jnp.dot(at_ref[...], wa_ref[...], preferred_element_type=F32)
    mixed = jax.nn.sigmoid(gc_ref[...]) * yc + jax.nn.sigmoid(ga_ref[...]) * ya
    y = jnp.dot(mixed.astype(BF16), wo_ref[...], preferred_element_type=F32)
    o_ref[...] = x_ref[...] + g1_ref[0] * y


def _mix(cact, attn, proj, x, g1, wc, wa, wo, seq, gc_col, ga_col):
    T, D = x.shape
    C = cact.shape[1]
    A = attn.shape[1]
    tm = 256
    per_seq = seq // tm
    return pl.pallas_call(
        _mix_kernel,
        grid=(T // tm,),
        in_specs=[
            pl.BlockSpec((tm, C), lambda i: (i, 0)),
            pl.BlockSpec((tm, A), lambda i: (i, 0)),
            pl.BlockSpec((tm, D), lambda i: (i, gc_col)),
            pl.BlockSpec((tm, D), lambda i: (i, ga_col)),
            pl.BlockSpec((tm, D), lambda i: (i, 0)),
            pl.BlockSpec((1, 1, D), lambda i: (i // per_seq, 0, 0)),
            _resident((C, D), lambda i: (0, 0)),
            _resident((A, D), lambda i: (0, 0)),
            _resident((D, D), lambda i: (0, 0)),
        ],
        out_specs=pl.BlockSpec((tm, D), lambda i: (i, 0)),
        out_shape=jax.ShapeDtypeStruct((T, D), F32),
        compiler_params=_cparams(("parallel",)),
        name="merge_out_proj",
    )(cact, attn, proj, proj, x, g1, wc, wa, wo)


def _store_packed_slabs(o_ref, xb):
    w = lax.bitcast_convert_type(xb.astype(F32), jnp.uint32)
    for i in range(SLAB_WORDS):
        lo = w[:, (2 * i) * LANES:(2 * i + 1) * LANES]
        hi = w[:, (2 * i + 1) * LANES:(2 * i + 2) * LANES]
        o_ref[:, i, :] = hi | (lo >> 16)


def _pack_kernel(x_ref, o_ref):
    _store_packed_slabs(o_ref, x_ref[0].astype(BF16))


def _pack_table(tabs, layer):
    _, n, d = tabs.shape
    te = 256
    packed = pl.pallas_call(
        _pack_kernel,
        grid=(n // te,),
        in_specs=[pl.BlockSpec((1, te, d), lambda i: (layer, i, 0))],
        out_specs=pl.BlockSpec((te, SLAB_WORDS, LANES), lambda i: (i, 0, 0)),
        out_shape=jax.ShapeDtypeStruct((n, SLAB_WORDS, LANES), jnp.uint32),
        compiler_params=_cparams(("parallel",)),
        name="pack_expert_table",
    )(tabs)
    return packed.reshape(n * SLAB_WORDS, LANES)


def _peer_q_kernel(x_ref, g_ref, sc_ref, sh_ref, w_ref, k_ref, h_ref, st_ref):
    hb = _norm_mod(x_ref[...], g_ref[...], sc_ref[0], sh_ref[0]).astype(BF16)
    _store_packed_slabs(h_ref, hb)
    qb = jnp.dot(hb, w_ref[...], preferred_element_type=F32).astype(BF16)
    for hp in range(2 * PEER_HEADS):
        q_hp = qb[:, hp * PEER_HALF:(hp + 1) * PEER_HALF]
        st_ref[hp * N_KEYS:(hp + 1) * N_KEYS, :] = lax.dot_general(
            k_ref[hp], q_hp, (((1,), (1,)), ((), ())), preferred_element_type=F32)


def _peer_q(x, g, sc, sh, w_pq, keys, seq):
    T, D = x.shape
    Q = w_pq.shape[1]
    tm = 256
    per_seq = seq // tm
    n_rows = 2 * PEER_HEADS * N_KEYS
    return pl.pallas_call(
        _peer_q_kernel,
        grid=(T // tm,),
        in_specs=[
            pl.BlockSpec((tm, D), lambda i: (i, 0)),
            pl.BlockSpec((1, D), lambda i: (0, 0)),
            pl.BlockSpec((1, 1, D), lambda i: (i // per_seq, 0, 0)),
            pl.BlockSpec((1, 1, D), lambda i: (i // per_seq, 0, 0)),
            _resident((D, Q), lambda i: (0, 0)),
            _resident((2 * PEER_HEADS, N_KEYS, PEER_HALF), lambda i: (0, 0, 0)),
        ],
        out_specs=[
            pl.BlockSpec((tm, SLAB_WORDS, LANES), lambda i: (i, 0, 0)),
            pl.BlockSpec((n_rows, tm), lambda i: (0, i)),
        ],
        out_shape=[
            jax.ShapeDtypeStruct((T, SLAB_WORDS, LANES), jnp.uint32),
            jax.ShapeDtypeStruct((n_rows, T), F32),
        ],
        compiler_params=_cparams(("parallel",)),
        name="peer_query_scores",
    )(x, g, sc, sh, w_pq, keys)


def _top_rows(x, payload, k, rank=None):
    rows = (lax.broadcasted_iota(I32, x.shape, 0) if rank is None else rank).astype(F32)
    vals, pays = [], []
    for _ in range(k):
        m = jnp.max(x, axis=0, keepdims=True)
        first = jnp.min(jnp.where(x == m, rows, jnp.float32(1 << 24)), axis=0, keepdims=True)
        onehot = rows == first
        if payload is None:
            pays.append(first.astype(I32))
        else:
            pays.append(jnp.sum(jnp.where(onehot, payload, 0), axis=0, keepdims=True))
        vals.append(m)
        x = jnp.where(onehot, -jnp.inf, x)
    return jnp.concatenate(vals, axis=0), jnp.concatenate(pays, axis=0)


def _sort_rows(keys, pays):
    n_groups = len(keys)
    n_rows = 8 * n_groups
    sub = lax.broadcasted_iota(I32, keys[0].shape, 0)
    k = 2
    while k <= n_rows:
        j = k // 2
        while j >= 1:
            if j >= 8:
                gj = j // 8
                for lo in range(n_groups):
                    if lo & gj:
                        continue
                    hi = lo | gj
                    ascending = ((8 * lo) & k) == 0
                    a, b = keys[lo], keys[hi]
                    swap = (a > b) if ascending else (a < b)
                    keys[lo], keys[hi] = jnp.where(swap, b, a), jnp.where(swap, a, b)
                    pa, pb = pays[lo], pays[hi]
                    pays[lo], pays[hi] = jnp.where(swap, pb, pa), jnp.where(swap, pa, pb)
            else:
                lower = (sub & j) == 0
                upper = (sub & j) != 0
                for g in range(n_groups):
                    if k >= 8:
                        ascending = ((8 * g) & k) == 0
                        want_max = upper if ascending else lower
                    else:
                        want_max = jnp.logical_xor(upper, (sub & k) != 0)
                    x, p = keys[g], pays[g]
                    xp = jnp.where(lower, pltpu.roll(x, 8 - j, 0), pltpu.roll(x, j, 0))
                    pp = jnp.where(lower, pltpu.roll(p, 8 - j, 0), pltpu.roll(p, j, 0))
                    swap = jnp.logical_xor(xp < x, want_max)
                    keys[g] = jnp.where(swap, xp, x)
                    pays[g] = jnp.where(swap, pp, p)
            j //= 2
        k *= 2
    return keys, pays


def _route_kernel(block_rows, st_ref, idxl_ref, gate_ref, ovf_ref, idx_scr, gate_scr):
    def head(h, carry):
        base = pl.multiple_of(h * (2 * N_KEYS), 2 * N_KEYS)
        v0, i0 = _top_rows(st_ref[pl.ds(base, N_KEYS), :], None, PEER_TOPK)
        v1, i1 = _top_rows(st_ref[pl.ds(base + N_KEYS, N_KEYS), :], None, PEER_TOPK)
        half = PEER_TOPK // 2
        sub = lax.broadcasted_iota(I32, (half, v0.shape[1]), 0)
        cand = [v0[k:k + 1] + v1[:half] for k in range(half)]
        cidx = [i0[k:k + 1] * N_KEYS + i1[:half] for k in range(half)]
        flat = [sub + k * PEER_TOPK for k in range(half)]
        cand += [v0[:1] + v1[half:], v0[half:] + v1[:1]]
        cidx += [i0[:1] * N_KEYS + i1[half:], i0[half:] * N_KEYS + i1[:1]]
        flat += [sub + half, (sub + half) * PEER_TOPK]
        best, eid = _top_rows(jnp.concatenate(cand, axis=0), jnp.concatenate(cidx, axis=0), PEER_TOPK,
                              rank=jnp.concatenate(flat, axis=0))
        e = jnp.exp(best - jnp.max(best, axis=0, keepdims=True))
        gates = e / jnp.sum(e, axis=0, keepdims=True)
        off = pl.multiple_of(h * PEER_TOPK, PEER_TOPK)
        idx_scr[pl.ds(off, PEER_TOPK), :] = eid
        gate_scr[pl.ds(off, PEER_TOPK), :] = gates
        return carry

    lax.fori_loop(0, PEER_HEADS, head, 0, unroll=HEAD_UNROLL)

    n_groups = PAIRS // 8
    sub = lax.broadcasted_iota(I32, (8, idx_scr.shape[1]), 0)
    keys = [idx_scr[8 * g:8 * g + 8, :] * PAIRS + (sub + 8 * g) for g in range(n_groups)]
    pays = [gate_scr[8 * g:8 * g + 8, :] for g in range(n_groups)]
    keys, pays = _sort_rows(keys, pays)
    eids = [jnp.right_shift(k, PAIRS.bit_length() - 1) for k in keys]
    row_hi, row_lo = WINDOW_SLOTS, PAIRS - WINDOW_SLOTS - 1
    ovf0 = jnp.where(eids[row_hi // 8][row_hi % 8:row_hi % 8 + 1, :] < block_rows, 1, 0)
    ovf1 = jnp.where(eids[row_lo // 8][row_lo % 8:row_lo % 8 + 1, :] >= block_rows, 1, 0)
    ovf_ref[...] = jnp.concatenate([ovf0, ovf1], axis=0).astype(I32)
    rot = (PAIRS - WINDOW_SLOTS) // 8
    for b in range(N_TABLE_BLOCKS):
        order = [(g + b * rot) % n_groups for g in range(n_groups)]
        e = jnp.concatenate([eids[g] for g in order], axis=0)
        inb = (e >= b * block_rows) & (e < (b + 1) * block_rows)
        local = jnp.where(inb, (e - b * block_rows) * SLAB_WORDS, 0)
        gate = jnp.where(inb, jnp.concatenate([pays[g] for g in order], axis=0), 0.0)
        idxl_ref[b] = local.astype(F32).T.astype(I32)
        gate_ref[b] = gate.T


def _route(st, block_rows):
    n_rows, T = st.shape
    tr = 128
    return pl.pallas_call(
        functools.partial(_route_kernel, block_rows),
        grid=(T // tr,),
        in_specs=[pl.BlockSpec((n_rows, tr), lambda i: (0, i))],
        out_specs=[
            pl.BlockSpec((N_TABLE_BLOCKS, tr, PAIRS), lambda i: (0, i, 0)),
            pl.BlockSpec((N_TABLE_BLOCKS, tr, PAIRS), lambda i: (0, i, 0)),
            pl.BlockSpec((N_TABLE_BLOCKS, tr), lambda i: (0, i)),
        ],
        out_shape=[
            jax.ShapeDtypeStruct((N_TABLE_BLOCKS, T, PAIRS), I32),
            jax.ShapeDtypeStruct((N_TABLE_BLOCKS, T, PAIRS), F32),
            jax.ShapeDtypeStruct((N_TABLE_BLOCKS, T), I32),
        ],
        scratch_shapes=[pltpu.VMEM((PAIRS, tr), I32), pltpu.VMEM((PAIRS, tr), F32)],
        compiler_params=_cparams(("parallel",)),
        name="peer_route",
    )(st)


def _diag_mask():
    r = lax.broadcasted_iota(I32, (SLAB_ROWS, PAIRS * SLAB_ROWS), 0)
    c = lax.broadcasted_iota(I32, (SLAB_ROWS, PAIRS * SLAB_ROWS), 1)
    return r == (c % SLAB_ROWS)


def _for_overflow_tokens(ovf_ref, b, tm, body):
    group = 8

    def step(g, carry):
        flags = [ovf_ref[b, g * group + i] for i in range(group)]
        any_set = functools.reduce(lambda x, y: x | y, flags)

        @pl.when(any_set != 0)
        def _():
            def one(i, c):
                t = g * group + i

                @pl.when(ovf_ref[b, t] != 0)
                def _():
                    body(t)
                return c

            lax.fori_loop(0, group, one, 0)
        return carry

    lax.fori_loop(0, tm // group, step, 0)


def _gather_slabs(tab_ref, idx_ref, t, slots):
    words = [tab_ref[pl.ds(pl.multiple_of(idx_ref[0, t, j], SLAB_WORDS), SLAB_WORDS), :] for j in slots]
    return pltpu.bitcast(jnp.concatenate(words, axis=0), BF16)


def _peer_u_kernel(idx_ref, ovf_ref, h_ref, sel_ref, u_ref, o_ref, part_scr):
    tm = h_ref.shape[0]
    b = pl.program_id(0)
    sel = sel_ref[...]
    per_tile = sel.shape[1] // SLAB_ROWS

    def lane_partials(t, slots):
        h = pltpu.bitcast(h_ref[t], BF16)
        prods = [pltpu.bitcast(
            u_ref[pl.ds(pl.multiple_of(idx_ref[0, t, j], SLAB_WORDS), SLAB_WORDS), :], BF16) * h for j in slots]
        for k in range(0, len(prods), per_tile):
            part_scr[t, slots[k]:slots[k] + per_tile, :] = jnp.dot(
                sel, jnp.concatenate(prods[k:k + per_tile], axis=0), preferred_element_type=F32)

    def gather_trip(c):
        for i in range(TOKEN_UNROLL):
            lane_partials(c * TOKEN_UNROLL + i, range(WINDOW_SLOTS))

    def lane_sum_trip(c):
        for k in range(0, TOKEN_UNROLL, 8):
            r = pl.multiple_of(c * TOKEN_UNROLL + k, 8)
            o_ref[0, pl.ds(r, 8), 0:WINDOW_SLOTS] = jnp.sum(part_scr[pl.ds(r, 8), 0:WINDOW_SLOTS, :], axis=-1)

    def rare(t):
        lane_partials(t, range(WINDOW_SLOTS, PAIRS))
        o_ref[0, pl.ds(t, 1), WINDOW_SLOTS:] = jnp.sum(part_scr[pl.ds(t, 1), WINDOW_SLOTS:, :], axis=-1)

    def trip(c, carry):
        gather_trip(c)
        lane_sum_trip(c - 1)
        return carry

    n_trips = tm // TOKEN_UNROLL
    o_ref[0, :, WINDOW_SLOTS:] = jnp.zeros((tm, PAIRS - WINDOW_SLOTS), F32)
    gather_trip(0)
    lax.fori_loop(1, n_trips, trip, 0)
    lane_sum_trip(n_trips - 1)
    _for_overflow_tokens(ovf_ref, b, tm, rare)


def _peer_u(idxl, ovf, h_slab, row_sum, u_tab):
    T = h_slab.shape[0]
    block_words = u_tab.shape[0] // N_TABLE_BLOCKS
    tm = 128
    return pl.pallas_call(
        _peer_u_kernel,
        grid=(N_TABLE_BLOCKS, T // tm),
        in_specs=[
            pl.BlockSpec((1, tm, PAIRS), lambda b, i: (b, i, 0), memory_space=pltpu.SMEM),
            pl.BlockSpec((N_TABLE_BLOCKS, tm), lambda b, i: (0, i), memory_space=pltpu.SMEM),
            pl.BlockSpec((tm, SLAB_WORDS, LANES), lambda b, i: (i, 0, 0)),
            _resident(row_sum.shape, lambda b, i: (0, 0)),
            _resident((block_words, LANES), lambda b, i: (b, 0)),
        ],
        out_specs=pl.BlockSpec((1, tm, PAIRS), lambda b, i: (b, i, 0)),
        out_shape=jax.ShapeDtypeStruct((N_TABLE_BLOCKS, T, PAIRS), F32),
        scratch_shapes=[pltpu.VMEM((tm, PAIRS, LANES), F32)],
        compiler_params=_cparams(("arbitrary", "arbitrary")),
        name="peer_expert_u",
    )(idxl, ovf, h_slab, row_sum, u_tab)


def _peer_v_kernel(idx_ref, ovf_ref, gate_ref, act_ref, ex_ref, v_ref, o_ref, wexp):
    tm = gate_ref.shape[1]
    b = pl.program_id(0)
    act = act_ref[0]
    gelu = 0.5 * act * (1.0 + lax.erf(act * np.float32(math.sqrt(0.5))))
    w = gate_ref[0] * gelu
    wexp[...] = jnp.dot(w.astype(BF16), ex_ref[...], preferred_element_type=F32)
    eye = _diag_mask()
    wcols = WINDOW_SLOTS * SLAB_ROWS

    def weighted(t, slots, lo, hi):
        vsel = _gather_slabs(v_ref, idx_ref, t, slots)
        wrow = jnp.broadcast_to(wexp[pl.ds(t, 1), lo:hi], (SLAB_ROWS, hi - lo))
        wbd = jnp.where(eye[:, lo:hi], wrow, 0.0).astype(BF16)
        return jnp.dot(wbd, vsel, preferred_element_type=F32)

    def tok(t, carry):
        o_ref[0, t] = weighted(t, range(WINDOW_SLOTS), 0, wcols)
        return carry

    def rare(t):
        o_ref[0, t] = o_ref[0, t] + weighted(t, range(WINDOW_SLOTS, PAIRS), wcols, PAIRS * SLAB_ROWS)

    lax.fori_loop(0, tm, tok, 0, unroll=TOKEN_UNROLL)
    _for_overflow_tokens(ovf_ref, b, tm, rare)


def _peer_v(idxl, ovf, gate, act, expand, v_tab):
    T = gate.shape[1]
    block_words = v_tab.shape[0] // N_TABLE_BLOCKS
    tm = 128
    return pl.pallas_call(
        _peer_v_kernel,
        grid=(N_TABLE_BLOCKS, T // tm),
        in_specs=[
            pl.BlockSpec((1, tm, PAIRS), lambda b, i: (b, i, 0), memory_space=pltpu.SMEM),
            pl.BlockSpec((N_TABLE_BLOCKS, tm), lambda b, i: (0, i), memory_space=pltpu.SMEM),
            pl.BlockSpec((1, tm, PAIRS), lambda b, i: (b, i, 0)),
            pl.BlockSpec((1, tm, PAIRS), lambda b, i: (b, i, 0)),
            _resident((PAIRS, PAIRS * SLAB_ROWS), lambda b, i: (0, 0)),
            _resident((block_words, LANES), lambda b, i: (b, 0)),
        ],
        out_specs=pl.BlockSpec((1, tm, SLAB_ROWS, LANES), lambda b, i: (b, i, 0, 0)),
        out_shape=jax.ShapeDtypeStruct((N_TABLE_BLOCKS, T, SLAB_ROWS, LANES), F32),
        scratch_shapes=[pltpu.VMEM((tm, PAIRS * SLAB_ROWS), F32)],
        compiler_params=_cparams(("arbitrary", "arbitrary")),
        name="peer_expert_v",
    )(idxl, ovf, gate, act, expand, v_tab)


def _resid_kernel(final, x_ref, o_ref, g2_ref, fg_ref, y_ref):
    cols = []
    for s in range(SLAB_ROWS):
        part = o_ref[0, :, s, :]
        for k in range(1, N_TABLE_BLOCKS):
            part = part + o_ref[k, :, s, :]
        cols.append(part)
    x = x_ref[...] + g2_ref[0] * jnp.concatenate(cols, axis=1)
    if final:
        ms = jnp.mean(x * x, axis=-1, keepdims=True)
        x = (x * lax.rsqrt(ms + EPS)) * fg_ref[...]
    y_ref[...] = x


def _resid(x, peer_out, g2, final_g, seq, final):
    T, D = x.shape
    tm = 512
    per_seq = seq // tm
    return pl.pallas_call(
        functools.partial(_resid_kernel, final),
        grid=(T // tm,),
        in_specs=[
            pl.BlockSpec((tm, D), lambda i: (i, 0)),
            pl.BlockSpec((N_TABLE_BLOCKS, tm, SLAB_ROWS, LANES), lambda i: (0, i, 0, 0)),
            pl.BlockSpec((1, 1, D), lambda i: (i // per_seq, 0, 0)),
            pl.BlockSpec((1, D), lambda i: (0, 0)),
        ],
        out_specs=pl.BlockSpec((tm, D), lambda i: (i, 0)),
        out_shape=jax.ShapeDtypeStruct((T, D), F32),
        compiler_params=_cparams(("parallel",)),
        name="peer_residual",
    )(x, peer_out, g2, final_g)


def _pair_expand_matrix():
    p = np.arange(PAIRS)[:, None]
    c = np.arange(PAIRS * SLAB_ROWS)[None, :]
    return (p == c // SLAB_ROWS).astype(np.float32)


def kernel(x, c, rel_bias, ada_w, ada_b, norm1_g, w_in, dw_w, dw_b, conv_ln_g, conv_ln_b, w_conv_out,
           attn_sinks, w_attn_out, w_out, norm2_g, w_pq, sub_keys, peer_u, peer_v, final_g):
    B, S, D = x.shape
    L = ada_w.shape[0]
    T = B * S
    C = dw_w.shape[2]
    A = N_HEADS * HEAD_DIM
    KV = N_KV_HEADS * HEAD_DIM
    E = peer_u.shape[1]
    assert S % 1024 == 0 and D == SLAB_ROWS * LANES and E % N_TABLE_BLOCKS == 0

    o_a, o_b, o_q, o_k, o_v, o_g = 0, C, 2 * C, 2 * C + A, 2 * C + A + KV, 2 * C + A + 2 * KV
    order = np.concatenate([np.arange(o_a, o_a + 2 * C), np.arange(o_g, o_g + 2 * D),
                            np.arange(o_q, o_q + A + 2 * KV)])
    n_a, n_b, n_gc, n_ga, n_q, n_k, n_v = 0, C, 2 * C, 2 * C + D, 2 * C + 2 * D, 2 * C + 2 * D + A, 2 * C + 2 * D + A + KV

    c_pad = jnp.pad(c, ((0, 8 - B), (0, 0)))
    mod = _ada(c_pad, ada_w, ada_b)
    bias_tab = _rel_bias_table(rel_bias)
    expand = jnp.asarray(_pair_expand_matrix(), dtype=BF16)
    row_sum = jnp.asarray(_pair_expand_matrix()[:MXU_DEPTH // SLAB_ROWS, :MXU_DEPTH], dtype=BF16)
    final_g2 = final_g.reshape(1, D)

    xt = x.reshape(T, D)
    for l in range(L):
        m = mod[l, :B].reshape(B, 6, 1, D)
        sh1, sc1, g1, sh2, sc2, g2 = (m[:, k] for k in range(6))

        w_in_l = w_in[l][:, order].astype(BF16)
        proj = _inproj(xt, norm1_g[l].reshape(1, D), sc1, sh1, w_in_l, S)
        cact = _conv(proj, dw_w[l], dw_b[l].reshape(1, C), conv_ln_g[l].reshape(1, C),
                     conv_ln_b[l].reshape(1, C), S, n_a // C, n_b // C)
        attn = _attn(proj, bias_tab, attn_sinks[l], S, n_q // A, n_k // KV, n_v // KV)
        xt = _mix(cact, attn, proj, xt, g1, w_conv_out[l].astype(BF16), w_attn_out[l].astype(BF16),
                  w_out[l].astype(BF16), S, n_gc // D, n_ga // D)

        keys = sub_keys[l].reshape(2 * PEER_HEADS, N_KEYS, PEER_HALF).astype(BF16)
        h2, st = _peer_q(xt, norm2_g[l].reshape(1, D), sc2, sh2, w_pq[l].astype(BF16), keys, S)
        idxl, gate, ovf = _route(st, E // N_TABLE_BLOCKS)
        u_tab = _pack_table(peer_u, l)
        v_tab = _pack_table(peer_v, l)
        act = _peer_u(idxl, ovf, h2, row_sum, u_tab)
        pout = _peer_v(idxl, ovf, gate, act, expand, v_tab)
        xt = _resid(xt, pout, g2, final_g2, S, l == L - 1)
    return xt.reshape(B, S, D)
```

```python
import functools
import math

import numpy as np
import jax
import jax.numpy as jnp
from jax import lax
from jax.experimental import pallas as pl
from jax.experimental.pallas import tpu as pltpu

F32 = jnp.float32
BF16 = jnp.bfloat16
I32 = jnp.int32

EPS = 1e-6
CONV_WIDTH = 31
HALO = 32
N_HEADS = 16
N_KV_HEADS = 4
HEAD_DIM = 64
BLOCK = 128
NUM_BUCKETS = 32
MAX_DISTANCE = 128
N_KEYS = 128
PEER_HEADS = 8
PEER_TOPK = 16
PEER_HALF = 128
PAIRS = PEER_HEADS * PEER_TOPK
SLAB_ROWS = 16
LANES = 128
N_TABLE_BLOCKS = 2
TOKEN_UNROLL = 16
HEAD_UNROLL = 4
MXU_DEPTH = 256
SLAB_WORDS = 8
WINDOW_SLOTS = 80
assert PAIRS & (PAIRS - 1) == 0 and WINDOW_SLOTS % 8 == 0 and 2 * WINDOW_SLOTS >= PAIRS
VMEM_LIMIT = 56 * 1024 * 1024


def _cparams(sem, vmem=VMEM_LIMIT):
    return pltpu.CompilerParams(dimension_semantics=sem, vmem_limit_bytes=vmem)


def _resident(block_shape, index_map):
    return pl.BlockSpec(block_shape, index_map, pipeline_mode=pl.Buffered(1))


def _ada_kernel(c_ref, w_ref, b_ref, o_ref):
    c = c_ref[...]
    cs = c * jax.nn.sigmoid(c)
    o_ref[0] = jnp.dot(cs.astype(BF16), w_ref[0].astype(BF16), preferred_element_type=F32) + b_ref[0]


def _ada(c_pad, ada_w, ada_b):
    L, D, N = ada_w.shape
    tn = 1024
    return pl.pallas_call(
        _ada_kernel,
        grid=(L, N // tn),
        in_specs=[
            pl.BlockSpec((8, D), lambda l, j: (0, 0)),
            pl.BlockSpec((1, D, tn), lambda l, j: (l, 0, j)),
            pl.BlockSpec((1, 1, tn), lambda l, j: (l, 0, j)),
        ],
        out_specs=pl.BlockSpec((1, 8, tn), lambda l, j: (l, 0, j)),
        out_shape=jax.ShapeDtypeStruct((L, 8, N), F32),
        compiler_params=_cparams(("parallel", "parallel")),
        name="ada_mod",
    )(c_pad, ada_w, ada_b.reshape(L, 1, N))


def _bias_kernel(bucket_ref, rb_ref, o_ref):
    h = pl.program_id(0)
    bucket = bucket_ref[...]
    acc = jnp.zeros(bucket.shape, F32)
    for b in range(NUM_BUCKETS):
        acc = jnp.where(bucket == b, rb_ref[b, h], acc)
    o_ref[0] = acc


def _t5_bucket_table():
    qi = np.arange(BLOCK)[:, None] + BLOCK
    kj = np.arange(2 * BLOCK)[None, :]
    d = np.maximum(qi - kj, 0)
    max_exact = NUM_BUCKETS // 2
    ratio = (np.log(np.maximum(d, 1).astype(np.float64) / max_exact)
             / math.log(MAX_DISTANCE / max_exact) * (NUM_BUCKETS - max_exact))
    large = np.minimum(max_exact + ratio.astype(np.int64), NUM_BUCKETS - 1)
    return np.where(d < max_exact, d, large).astype(np.int32)


def _rel_bias_table(rel_bias):
    bucket = jnp.asarray(_t5_bucket_table())
    return pl.pallas_call(
        _bias_kernel,
        grid=(N_HEADS,),
        in_specs=[
            pl.BlockSpec((BLOCK, 2 * BLOCK), lambda h: (0, 0)),
            pl.BlockSpec(memory_space=pltpu.SMEM),
        ],
        out_specs=pl.BlockSpec((1, BLOCK, 2 * BLOCK), lambda h: (h, 0, 0)),
        out_shape=jax.ShapeDtypeStruct((N_HEADS, BLOCK, 2 * BLOCK), F32),
        compiler_params=_cparams(("arbitrary",)),
        name="rel_bias_table",
    )(bucket, rel_bias)


def _norm_mod(x, g, sc, sh):
    ms = jnp.mean(x * x, axis=-1, keepdims=True)
    y = x * lax.rsqrt(ms + EPS)
    return (y * g) * (1.0 + sc) + sh


def _inproj_kernel(x_ref, g_ref, sc_ref, sh_ref, w_ref, o_ref, h_scr):
    @pl.when(pl.program_id(1) == 0)
    def _():
        h_scr[...] = _norm_mod(x_ref[...], g_ref[...], sc_ref[0], sh_ref[0]).astype(BF16)

    o_ref[...] = jnp.dot(h_scr[...], w_ref[...], preferred_element_type=F32)


def _inproj(x, g, sc, sh, w, seq):
    T, D = x.shape
    N = w.shape[1]
    tm, tn = 1024, 1536
    per_seq = seq // tm
    return pl.pallas_call(
        _inproj_kernel,
        grid=(T // tm, N // tn),
        in_specs=[
            pl.BlockSpec((tm, D), lambda i, j: (i, 0)),
            pl.BlockSpec((1, D), lambda i, j: (0, 0)),
            pl.BlockSpec((1, 1, D), lambda i, j: (i // per_seq, 0, 0)),
            pl.BlockSpec((1, 1, D), lambda i, j: (i // per_seq, 0, 0)),
            pl.BlockSpec((D, tn), lambda i, j: (0, j)),
        ],
        out_specs=pl.BlockSpec((tm, tn), lambda i, j: (i, j)),
        out_shape=jax.ShapeDtypeStruct((T, N), F32),
        scratch_shapes=[pltpu.VMEM((tm, D), BF16)],
        compiler_params=_cparams(("parallel", "arbitrary")),
        name="in_proj",
    )(x, g, sc, sh, w)


def _conv_kernel(per_seq, a_ref, b_ref, ha_ref, hb_ref, dw_ref, db_ref, lg_ref, lb_ref, o_ref, ext, acc_scr):
    ts = a_ref.shape[0]
    first = (pl.program_id(0) % per_seq) == 0
    hu = ha_ref[...] * jax.nn.sigmoid(hb_ref[...])
    ext[0:HALO, :] = jnp.where(first, 0.0, hu)
    ext[HALO:, :] = a_ref[...] * jax.nn.sigmoid(b_ref[...])
    base = HALO - (CONV_WIDTH - 1)
    n_ch = a_ref.shape[1]
    for c0 in range(0, n_ch, LANES):
        acc = jnp.zeros((ts // 8, 8, LANES), F32)
        for r in range(8):
            offs = [o for o in range(base, base + CONV_WIDTH) if o % 8 == r]
            if not offs:
                continue
            shifted = ext[r:r + 8 * (max(offs) // 8) + ts, c0:c0 + LANES]
            for o in offs:
                tap = dw_ref[o - base, :, c0:c0 + LANES]
                acc = acc + shifted[o - r:o - r + ts, :].reshape(ts // 8, 8, LANES) * tap[None]
        acc_scr[:, c0:c0 + LANES] = acc.reshape(ts, LANES)
    acc = acc_scr[...] + db_ref[...]
    mu = jnp.mean(acc, axis=-1, keepdims=True)
    cen = acc - mu
    var = jnp.mean(cen * cen, axis=-1, keepdims=True)
    y = cen * lax.rsqrt(var + EPS) * lg_ref[...] + lb_ref[...]
    o_ref[...] = (y * jax.nn.sigmoid(y)).astype(BF16)


def _conv(proj, dw_w, dw_b, ln_g, ln_b, seq, a_col, b_col):
    T = proj.shape[0]
    C = dw_w.shape[-1]
    ts = 256
    per_seq = seq // ts
    rb = ts // HALO
    halo_map = lambda col: (lambda i: (jnp.maximum(i * rb - 1, 0), col))
    return pl.pallas_call(
        functools.partial(_conv_kernel, per_seq),
        grid=(T // ts,),
        in_specs=[
            pl.BlockSpec((ts, C), lambda i: (i, a_col)),
            pl.BlockSpec((ts, C), lambda i: (i, b_col)),
            pl.BlockSpec((HALO, C), halo_map(a_col)),
            pl.BlockSpec((HALO, C), halo_map(b_col)),
            pl.BlockSpec((CONV_WIDTH, 8, C), lambda i: (0, 0, 0)),
            pl.BlockSpec((1, C), lambda i: (0, 0)),
            pl.BlockSpec((1, C), lambda i: (0, 0)),
            pl.BlockSpec((1, C), lambda i: (0, 0)),
        ],
        out_specs=pl.BlockSpec((ts, C), lambda i: (i, 0)),
        out_shape=jax.ShapeDtypeStruct((T, C), BF16),
        scratch_shapes=[pltpu.VMEM((HALO + ts, C), F32), pltpu.VMEM((ts, C), F32)],
        compiler_params=_cparams(("parallel",)),
        name="conv_branch",
    )(proj, proj, proj, proj, dw_w, dw_b, ln_g, ln_b)


def _attn_kernel(nb, q_ref, kc_ref, kp_ref, vc_ref, vp_ref, bias_ref, sink_ref, o_ref):
    first = (pl.program_id(0) % nb) == 0
    kk = jnp.concatenate([kp_ref[...], kc_ref[...]], axis=0).astype(BF16)
    vv = jnp.concatenate([vp_ref[...], vc_ref[...]], axis=0).astype(BF16)
    row = lax.broadcasted_iota(I32, (BLOCK, 2 * BLOCK), 0)
    col = lax.broadcasted_iota(I32, (BLOCK, 2 * BLOCK), 1)
    dist = row + BLOCK - col
    valid = (dist >= 0) & (dist < BLOCK) & (jnp.logical_not(first) | (col >= BLOCK))
    group = N_HEADS // N_KV_HEADS
    for kvh in range(N_KV_HEADS):
        k_h = kk[:, kvh * HEAD_DIM:(kvh + 1) * HEAD_DIM]
        v_h = vv[:, kvh * HEAD_DIM:(kvh + 1) * HEAD_DIM]
        qg = q_ref[:, kvh * group * HEAD_DIM:(kvh + 1) * group * HEAD_DIM].astype(BF16)
        outs = []
        for g in range(group):
            h = kvh * group + g
            q_h = qg[:, g * HEAD_DIM:(g + 1) * HEAD_DIM]
            s = lax.dot_general(q_h, k_h, (((1,), (1,)), ((), ())), preferred_element_type=F32)
            s = s * (HEAD_DIM ** -0.5) + bias_ref[h]
            s = jnp.where(valid, s, -jnp.inf)
            sink = sink_ref[h]
            m = jnp.maximum(jnp.max(s, axis=-1, keepdims=True), sink)
            e = jnp.exp(s - m)
            denom = jnp.sum(e, axis=-1, keepdims=True) + jnp.exp(sink - m)
            p = e / denom
            outs.append(jnp.dot(p.astype(BF16), v_h, preferred_element_type=F32))
        o_ref[:, kvh * group * HEAD_DIM:(kvh + 1) * group * HEAD_DIM] = (
            jnp.concatenate(outs, axis=1).astype(BF16))


def _attn(proj, bias_tab, sinks, seq, q_col, k_col, v_col):
    T = proj.shape[0]
    nb = seq // BLOCK
    attn_dim = N_HEADS * HEAD_DIM
    kv_dim = N_KV_HEADS * HEAD_DIM
    prev = lambda col: (lambda i: (jnp.where(i % nb == 0, i, i - 1), col))
    return pl.pallas_call(
        functools.partial(_attn_kernel, nb),
        grid=(T // BLOCK,),
        in_specs=[
            pl.BlockSpec((BLOCK, attn_dim), lambda i: (i, q_col)),
            pl.BlockSpec((BLOCK, kv_dim), lambda i: (i, k_col)),
            pl.BlockSpec((BLOCK, kv_dim), prev(k_col)),
            pl.BlockSpec((BLOCK, kv_dim), lambda i: (i, v_col)),
            pl.BlockSpec((BLOCK, kv_dim), prev(v_col)),
            pl.BlockSpec((N_HEADS, BLOCK, 2 * BLOCK), lambda i: (0, 0, 0)),
            pl.BlockSpec(memory_space=pltpu.SMEM),
        ],
        out_specs=pl.BlockSpec((BLOCK, attn_dim), lambda i: (i, 0)),
        out_shape=jax.ShapeDtypeStruct((T, attn_dim), BF16),
        compiler_params=_cparams(("parallel",)),
        name="swa_attention",
    )(proj, proj, proj, proj, proj, bias_tab, sinks)


def _mix_kernel(ca_ref, at_ref, gc_ref, ga_ref, x_ref, g1_ref, wc_ref, wa_ref, wo_ref, o_ref):
    yc = jnp.dot(ca_ref[...], wc_ref[...], preferred_element_type=F32)
    ya = jnp.dot(at_ref[...], wa_ref[...], preferred_element_type=F32)
    mixed = jax.nn.sigmoid(gc_ref[...]) * yc + jax.nn.sigmoid(ga_ref[...]) * ya
    y = jnp.dot(mixed.astype(BF16), wo_ref[...], preferred_element_type=F32)
    o_ref[...] = x_ref[...] + g1_ref[0] * y


def _mix(cact, attn, proj, x, g1, wc, wa, wo, seq, gc_col, ga_col):
    T, D = x.shape
    C = cact.shape[1]
    A = attn.shape[1]
    tm = 256
    per_seq = seq // tm
    return pl.pallas_call(
        _mix_kernel,
        grid=(T // tm,),
        in_specs=[
            pl.BlockSpec((tm, C), lambda i: (i, 0)),
            pl.BlockSpec((tm, A), lambda i: (i, 0)),
            pl.BlockSpec((tm, D), lambda i: (i, gc_col)),
            pl.BlockSpec((tm, D), lambda i: (i, ga_col)),
            pl.BlockSpec((tm, D), lambda i: (i, 0)),
            pl.BlockSpec((1, 1, D), lambda i: (i // per_seq, 0, 0)),
            _resident((C, D), lambda i: (0, 0)),
            _resident((A, D), lambda i: (0, 0)),
            _resident((D, D), lambda i: (0, 0)),
        ],
        out_specs=pl.BlockSpec((tm, D), lambda i: (i, 0)),
        out_shape=jax.ShapeDtypeStruct((T, D), F32),
        compiler_params=_cparams(("parallel",)),
        name="merge_out_proj",
    )(cact, attn, proj, proj, x, g1, wc, wa, wo)


def _store_packed_slabs(o_ref, xb):
    w = lax.bitcast_convert_type(xb.astype(F32), jnp.uint32)
    for i in range(SLAB_WORDS):
        lo = w[:, (2 * i) * LANES:(2 * i + 1) * LANES]
        hi = w[:, (2 * i + 1) * LANES:(2 * i + 2) * LANES]
        o_ref[:, i, :] = hi | (lo >> 16)


def _pack_kernel(x_ref, o_ref):
    _store_packed_slabs(o_ref, x_ref[0].astype(BF16))


def _pack_table(tabs, layer):
    _, n, d = tabs.shape
    te = 256
    packed = pl.pallas_call(
        _pack_kernel,
        grid=(n // te,),
        in_specs=[pl.BlockSpec((1, te, d), lambda i: (layer, i, 0))],
        out_specs=pl.BlockSpec((te, SLAB_WORDS, LANES), lambda i: (i, 0, 0)),
        out_shape=jax.ShapeDtypeStruct((n, SLAB_WORDS, LANES), jnp.uint32),
        compiler_params=_cparams(("parallel",)),
        name="pack_expert_table",
    )(tabs)
    return packed.reshape(n * SLAB_WORDS, LANES)


def _peer_q_kernel(x_ref, g_ref, sc_ref, sh_ref, w_ref, k_ref, h_ref, st_ref):
    hb = _norm_mod(x_ref[...], g_ref[...], sc_ref[0], sh_ref[0]).astype(BF16)
    _store_packed_slabs(h_ref, hb)
    qb = jnp.dot(hb, w_ref[...], preferred_element_type=F32).astype(BF16)
    for hp in range(2 * PEER_HEADS):
        q_hp = qb[:, hp * PEER_HALF:(hp + 1) * PEER_HALF]
        st_ref[hp * N_KEYS:(hp + 1) * N_KEYS, :] = lax.dot_general(
            k_ref[hp], q_hp, (((1,), (1,)), ((), ())), preferred_element_type=F32)


def _peer_q(x, g, sc, sh, w_pq, keys, seq):
    T, D = x.shape
    Q = w_pq.shape[1]
    tm = 256
    per_seq = seq // tm
    n_rows = 2 * PEER_HEADS * N_KEYS
    return pl.pallas_call(
        _peer_q_kernel,
        grid=(T // tm,),
        in_specs=[
            pl.BlockSpec((tm, D), lambda i: (i, 0)),
            pl.BlockSpec((1, D), lambda i: (0, 0)),
            pl.BlockSpec((1, 1, D), lambda i: (i // per_seq, 0, 0)),
            pl.BlockSpec((1, 1, D), lambda i: (i // per_seq, 0, 0)),
            _resident((D, Q), lambda i: (0, 0)),
            _resident((2 * PEER_HEADS, N_KEYS, PEER_HALF), lambda i: (0, 0, 0)),
        ],
        out_specs=[
            pl.BlockSpec((tm, SLAB_WORDS, LANES), lambda i: (i, 0, 0)),
            pl.BlockSpec((n_rows, tm), lambda i: (0, i)),
        ],
        out_shape=[
            jax.ShapeDtypeStruct((T, SLAB_WORDS, LANES), jnp.uint32),
            jax.ShapeDtypeStruct((n_rows, T), F32),
        ],
        compiler_params=_cparams(("parallel",)),
        name="peer_query_scores",
    )(x, g, sc, sh, w_pq, keys)


def _top_rows(x, payload, k, rank=None):
    rows = (lax.broadcasted_iota(I32, x.shape, 0) if rank is None else rank).astype(F32)
    vals, pays = [], []
    for _ in range(k):
        m = jnp.max(x, axis=0, keepdims=True)
        first = jnp.min(jnp.where(x == m, rows, jnp.float32(1 << 24)), axis=0, keepdims=True)
        onehot = rows == first
        if payload is None:
            pays.append(first.astype(I32))
        else:
            pays.append(jnp.sum(jnp.where(onehot, payload, 0), axis=0, keepdims=True))
        vals.append(m)
        x = jnp.where(onehot, -jnp.inf, x)
    return jnp.concatenate(vals, axis=0), jnp.concatenate(pays, axis=0)


def _sort_rows(keys, pays):
    n_groups = len(keys)
    n_rows = 8 * n_groups
    sub = lax.broadcasted_iota(I32, keys[0].shape, 0)
    k = 2
    while k <= n_rows:
        j = k // 2
        while j >= 1:
            if j >= 8:
                gj = j // 8
                for lo in range(n_groups):
                    if lo & gj:
                        continue
                    hi = lo | gj
                    ascending = ((8 * lo) & k) == 0
                    a, b = keys[lo], keys[hi]
                    swap = (a > b) if ascending else (a < b)
                    keys[lo], keys[hi] = jnp.where(swap, b, a), jnp.where(swap, a, b)
                    pa, pb = pays[lo], pays[hi]
                    pays[lo], pays[hi] = jnp.where(swap, pb, pa), jnp.where(swap, pa, pb)
            else:
                lower = (sub & j) == 0
                upper = (sub & j) != 0
                for g in range(n_groups):
                    if k >= 8:
                        ascending = ((8 * g) & k) == 0
                        want_max = upper if ascending else lower
                    else:
                        want_max = jnp.logical_xor(upper, (sub & k) != 0)
                    x, p = keys[g], pays[g]
                    xp = jnp.where(lower, pltpu.roll(x, 8 - j, 0), pltpu.roll(x, j, 0))
                    pp = jnp.where(lower, pltpu.roll(p, 8 - j, 0), pltpu.roll(p, j, 0))
                    swap = jnp.logical_xor(xp < x, want_max)
                    keys[g] = jnp.where(swap, xp, x)
                    pays[g] = jnp.where(swap, pp, p)
            j //= 2
        k *= 2
    return keys, pays


def _route_kernel(block_rows, st_ref, idxl_ref, gate_ref, ovf_ref, idx_scr, gate_scr):
    def head(h, carry):
        base = pl.multiple_of(h * (2 * N_KEYS), 2 * N_KEYS)
        v0, i0 = _top_rows(st_ref[pl.ds(base, N_KEYS), :], None, PEER_TOPK)
        v1, i1 = _top_rows(st_ref[pl.ds(base + N_KEYS, N_KEYS), :], None, PEER_TOPK)
        half = PEER_TOPK // 2
        sub = lax.broadcasted_iota(I32, (half, v0.shape[1]), 0)
        cand = [v0[k:k + 1] + v1[:half] for k in range(half)]
        cidx = [i0[k:k + 1] * N_KEYS + i1[:half] for k in range(half)]
        flat = [sub + k * PEER_TOPK for k in range(half)]
        cand += [v0[:1] + v1[half:], v0[half:] + v1[:1]]
        cidx += [i0[:1] * N_KEYS + i1[half:], i0[half:] * N_KEYS + i1[:1]]
        flat += [sub + half, (sub + half) * PEER_TOPK]
        best, eid = _top_rows(jnp.concatenate(cand, axis=0), jnp.concatenate(cidx, axis=0), PEER_TOPK,
                              rank=jnp.concatenate(flat, axis=0))
        e = jnp.exp(best - jnp.max(best, axis=0, keepdims=True))
        gates = e / jnp.sum(e, axis=0, keepdims=True)
        off = pl.multiple_of(h * PEER_TOPK, PEER_TOPK)
        idx_scr[pl.ds(off, PEER_TOPK), :] = eid
        gate_scr[pl.ds(off, PEER_TOPK), :] = gates
        return carry

    lax.fori_loop(0, PEER_HEADS, head, 0, unroll=HEAD_UNROLL)

    n_groups = PAIRS // 8
    sub = lax.broadcasted_iota(I32, (8, idx_scr.shape[1]), 0)
    keys = [idx_scr[8 * g:8 * g + 8, :] * PAIRS + (sub + 8 * g) for g in range(n_groups)]
    pays = [gate_scr[8 * g:8 * g + 8, :] for g in range(n_groups)]
    keys, pays = _sort_rows(keys, pays)
    eids = [jnp.right_shift(k, PAIRS.bit_length() - 1) for k in keys]
    row_hi, row_lo = WINDOW_SLOTS, PAIRS - WINDOW_SLOTS - 1
    ovf0 = jnp.where(eids[row_hi // 8][row_hi % 8:row_hi % 8 + 1, :] < block_rows, 1, 0)
    ovf1 = jnp.where(eids[row_lo // 8][row_lo % 8:row_lo % 8 + 1, :] >= block_rows, 1, 0)
    ovf_ref[...] = jnp.concatenate([ovf0, ovf1], axis=0).astype(I32)
    rot = (PAIRS - WINDOW_SLOTS) // 8
    for b in range(N_TABLE_BLOCKS):
        order = [(g + b * rot) % n_groups for g in range(n_groups)]
        e = jnp.concatenate([eids[g] for g in order], axis=0)
        inb = (e >= b * block_rows) & (e < (b + 1) * block_rows)
        local = jnp.where(inb, (e - b * block_rows) * SLAB_WORDS, 0)
        gate = jnp.where(inb, jnp.concatenate([pays[g] for g in order], axis=0), 0.0)
        idxl_ref[b] = local.astype(F32).T.astype(I32)
        gate_ref[b] = gate.T


def _route(st, block_rows):
    n_rows, T = st.shape
    tr = 128
    return pl.pallas_call(
        functools.partial(_route_kernel, block_rows),
        grid=(T // tr,),
        in_specs=[pl.BlockSpec((n_rows, tr), lambda i: (0, i))],
        out_specs=[
            pl.BlockSpec((N_TABLE_BLOCKS, tr, PAIRS), lambda i: (0, i, 0)),
            pl.BlockSpec((N_TABLE_BLOCKS, tr, PAIRS), lambda i: (0, i, 0)),
            pl.BlockSpec((N_TABLE_BLOCKS, tr), lambda i: (0, i)),
        ],
        out_shape=[
            jax.ShapeDtypeStruct((N_TABLE_BLOCKS, T, PAIRS), I32),
            jax.ShapeDtypeStruct((N_TABLE_BLOCKS, T, PAIRS), F32),
            jax.ShapeDtypeStruct((N_TABLE_BLOCKS, T), I32),
        ],
        scratch_shapes=[pltpu.VMEM((PAIRS, tr), I32), pltpu.VMEM((PAIRS, tr), F32)],
        compiler_params=_cparams(("parallel",)),
        name="peer_route",
    )(st)


def _diag_mask():
    r = lax.broadcasted_iota(I32, (SLAB_ROWS, PAIRS * SLAB_ROWS), 0)
    c = lax.broadcasted_iota(I32, (SLAB_ROWS, PAIRS * SLAB_ROWS), 1)
    return r == (c % SLAB_ROWS)


def _for_overflow_tokens(ovf_ref, b, tm, body):
    group = 8

    def step(g, carry):
        flags = [ovf_ref[b, g * group + i] for i in range(group)]
        any_set = functools.reduce(lambda x, y: x | y, flags)

        @pl.when(any_set != 0)
        def _():
            def one(i, c):
                t = g * group + i

                @pl.when(ovf_ref[b, t] != 0)
                def _():
                    body(t)
                return c

            lax.fori_loop(0, group, one, 0)
        return carry

    lax.fori_loop(0, tm // group, step, 0)


def _gather_slabs(tab_ref, idx_ref, t, slots):
    words = [tab_ref[pl.ds(pl.multiple_of(idx_ref[0, t, j], SLAB_WORDS), SLAB_WORDS), :] for j in slots]
    return pltpu.bitcast(jnp.concatenate(words, axis=0), BF16)


def _peer_u_kernel(idx_ref, ovf_ref, h_ref, sel_ref, u_ref, o_ref, part_scr):
    tm = h_ref.shape[0]
    b = pl.program_id(0)
    sel = sel_ref[...]
    per_tile = sel.shape[1] // SLAB_ROWS

    def lane_partials(t, slots):
        h = pltpu.bitcast(h_ref[t], BF16)
        prods = [pltpu.bitcast(
            u_ref[pl.ds(pl.multiple_of(idx_ref[0, t, j], SLAB_WORDS), SLAB_WORDS), :], BF16) * h for j in slots]
        for k in range(0, len(prods), per_tile):
            part_scr[t, slots[k]:slots[k] + per_tile, :] = jnp.dot(
                sel, jnp.concatenate(prods[k:k + per_tile], axis=0), preferred_element_type=F32)

    def gather_trip(c):
        for i in range(TOKEN_UNROLL):
            lane_partials(c * TOKEN_UNROLL + i, range(WINDOW_SLOTS))

    def lane_sum_trip(c):
        for k in range(0, TOKEN_UNROLL, 8):
            r = pl.multiple_of(c * TOKEN_UNROLL + k, 8)
            o_ref[0, pl.ds(r, 8), 0:WINDOW_SLOTS] = jnp.sum(part_scr[pl.ds(r, 8), 0:WINDOW_SLOTS, :], axis=-1)

    def rare(t):
        lane_partials(t, range(WINDOW_SLOTS, PAIRS))
        o_ref[0, pl.ds(t, 1), WINDOW_SLOTS:] = jnp.sum(part_scr[pl.ds(t, 1), WINDOW_SLOTS:, :], axis=-1)

    def trip(c, carry):
        gather_trip(c)
        lane_sum_trip(c - 1)
        return carry

    n_trips = tm // TOKEN_UNROLL
    o_ref[0, :, WINDOW_SLOTS:] = jnp.zeros((tm, PAIRS - WINDOW_SLOTS), F32)
    gather_trip(0)
    lax.fori_loop(1, n_trips, trip, 0)
    lane_sum_trip(n_trips - 1)
    _for_overflow_tokens(ovf_ref, b, tm, rare)


def _peer_u(idxl, ovf, h_slab, row_sum, u_tab):
    T = h_slab.shape[0]
    block_words = u_tab.shape[0] // N_TABLE_BLOCKS
    tm = 128
    return pl.pallas_call(
        _peer_u_kernel,
        grid=(N_TABLE_BLOCKS, T // tm),
        in_specs=[
            pl.BlockSpec((1, tm, PAIRS), lambda b, i: (b, i, 0), memory_space=pltpu.SMEM),
            pl.BlockSpec((N_TABLE_BLOCKS, tm), lambda b, i: (0, i), memory_space=pltpu.SMEM),
            pl.BlockSpec((tm, SLAB_WORDS, LANES), lambda b, i: (i, 0, 0)),
            _resident(row_sum.shape, lambda b, i: (0, 0)),
            _resident((block_words, LANES), lambda b, i: (b, 0)),
        ],
        out_specs=pl.BlockSpec((1, tm, PAIRS), lambda b, i: (b, i, 0)),
        out_shape=jax.ShapeDtypeStruct((N_TABLE_BLOCKS, T, PAIRS), F32),
        scratch_shapes=[pltpu.VMEM((tm, PAIRS, LANES), F32)],
        compiler_params=_cparams(("arbitrary", "arbitrary")),
        name="peer_expert_u",
    )(idxl, ovf, h_slab, row_sum, u_tab)


def _peer_v_kernel(idx_ref, ovf_ref, gate_ref, act_ref, ex_ref, v_ref, o_ref, wexp):
    tm = gate_ref.shape[1]
    b = pl.program_id(0)
    act = act_ref[0]
    gelu = 0.5 * act * (1.0 + lax.erf(act * np.float32(math.sqrt(0.5))))
    w = gate_ref[0] * gelu
    wexp[...] = jnp.dot(w.astype(BF16), ex_ref[...], preferred_element_type=F32)
    eye = _diag_mask()
    wcols = WINDOW_SLOTS * SLAB_ROWS

    def weighted(t, slots, lo, hi):
        vsel = _gather_slabs(v_ref, idx_ref, t, slots)
        wrow = jnp.broadcast_to(wexp[pl.ds(t, 1), lo:hi], (SLAB_ROWS, hi - lo))
        wbd = jnp.where(eye[:, lo:hi], wrow, 0.0).astype(BF16)
        return jnp.dot(wbd, vsel, preferred_element_type=F32)

    def tok(t, carry):
        o_ref[0, t] = weighted(t, range(WINDOW_SLOTS), 0, wcols)
        return carry

    def rare(t):
        o_ref[0, t] = o_ref[0, t] + weighted(t, range(WINDOW_SLOTS, PAIRS), wcols, PAIRS * SLAB_ROWS)

    lax.fori_loop(0, tm, tok, 0, unroll=TOKEN_UNROLL)
    _for_overflow_tokens(ovf_ref, b, tm, rare)


def _peer_v(idxl, ovf, gate, act, expand, v_tab):
    T = gate.shape[1]
    block_words = v_tab.shape[0] // N_TABLE_BLOCKS
    tm = 128
    return pl.pallas_call(
        _peer_v_kernel,
        grid=(N_TABLE_BLOCKS, T // tm),
        in_specs=[
            pl.BlockSpec((1, tm, PAIRS), lambda b, i: (b, i, 0), memory_space=pltpu.SMEM),
            pl.BlockSpec((N_TABLE_BLOCKS, tm), lambda b, i: (0, i), memory_space=pltpu.SMEM),
            pl.BlockSpec((1, tm, PAIRS), lambda b, i: (b, i, 0)),
            pl.BlockSpec((1, tm, PAIRS), lambda b, i: (b, i, 0)),
            _resident((PAIRS, PAIRS * SLAB_ROWS), lambda b, i: (0, 0)),
            _resident((block_words, LANES), lambda b, i: (b, 0)),
        ],
        out_specs=pl.BlockSpec((1, tm, SLAB_ROWS, LANES), lambda b, i: (b, i, 0, 0)),
        out_shape=jax.ShapeDtypeStruct((N_TABLE_BLOCKS, T, SLAB_ROWS, LANES), F32),
        scratch_shapes=[pltpu.VMEM((tm, PAIRS * SLAB_ROWS), F32)],
        compiler_params=_cparams(("arbitrary", "arbitrary")),
        name="peer_expert_v",
    )(idxl, ovf, gate, act, expand, v_tab)


def _resid_kernel(final, x_ref, o_ref, g2_ref, fg_ref, y_ref):
    cols = []
    for s in range(SLAB_ROWS):
        part = o_ref[0, :, s, :]
        for k in range(1, N_TABLE_BLOCKS):
            part = part + o_ref[k, :, s, :]
        cols.append(part)
    x = x_ref[...] + g2_ref[0] * jnp.concatenate(cols, axis=1)
    if final:
        ms = jnp.mean(x * x, axis=-1, keepdims=True)
        x = (x * lax.rsqrt(ms + EPS)) * fg_ref[...]
    y_ref[...] = x


def _resid(x, peer_out, g2, final_g, seq, final):
    T, D = x.shape
    tm = 512
    per_seq = seq // tm
    return pl.pallas_call(
        functools.partial(_resid_kernel, final),
        grid=(T // tm,),
        in_specs=[
            pl.BlockSpec((tm, D), lambda i: (i, 0)),
            pl.BlockSpec((N_TABLE_BLOCKS, tm, SLAB_ROWS, LANES), lambda i: (0, i, 0, 0)),
            pl.BlockSpec((1, 1, D), lambda i: (i // per_seq, 0, 0)),
            pl.BlockSpec((1, D), lambda i: (0, 0)),
        ],
        out_specs=pl.BlockSpec((tm, D), lambda i: (i, 0)),
        out_shape=jax.ShapeDtypeStruct((T, D), F32),
        compiler_params=_cparams(("parallel",)),
        name="peer_residual",
    )(x, peer_out, g2, final_g)


def _pair_expand_matrix():
    p = np.arange(PAIRS)[:, None]
    c = np.arange(PAIRS * SLAB_ROWS)[None, :]
    return (p == c // SLAB_ROWS).astype(np.float32)


def kernel(x, c, rel_bias, ada_w, ada_b, norm1_g, w_in, dw_w, dw_b, conv_ln_g, conv_ln_b, w_conv_out,
           attn_sinks, w_attn_out, w_out, norm2_g, w_pq, sub_keys, peer_u, peer_v, final_g):
    B, S, D = x.shape
    L = ada_w.shape[0]
    T = B * S
    C = dw_w.shape[2]
    A = N_HEADS * HEAD_DIM
    KV = N_KV_HEADS * HEAD_DIM
    E = peer_u.shape[1]
    assert S % 1024 == 0 and D == SLAB_ROWS * LANES and E % N_TABLE_BLOCKS == 0

    o_a, o_b, o_q, o_k, o_v, o_g = 0, C, 2 * C, 2 * C + A, 2 * C + A + KV, 2 * C + A + 2 * KV
    order = np.concatenate([np.arange(o_a, o_a + 2 * C), np.arange(o_g, o_g + 2 * D),
                            np.arange(o_q, o_q + A + 2 * KV)])
    n_a, n_b, n_gc, n_ga, n_q, n_k, n_v = 0, C, 2 * C, 2 * C + D, 2 * C + 2 * D, 2 * C + 2 * D + A, 2 * C + 2 * D + A + KV

    c_pad = jnp.pad(c, ((0, 8 - B), (0, 0)))
    mod = _ada(c_pad, ada_w, ada_b)
    bias_tab = _rel_bias_table(rel_bias)
    expand = jnp.asarray(_pair_expand_matrix(), dtype=BF16)
    row_sum = jnp.asarray(_pair_expand_matrix()[:MXU_DEPTH // SLAB_ROWS, :MXU_DEPTH], dtype=BF16)
    final_g2 = final_g.reshape(1, D)

    xt = x.reshape(T, D)
    for l in range(L):
        m = mod[l, :B].reshape(B, 6, 1, D)
        sh1, sc1, g1, sh2, sc2, g2 = (m[:, k] for k in range(6))

        w_in_l = w_in[l][:, order].astype(BF16)
        proj = _inproj(xt, norm1_g[l].reshape(1, D), sc1, sh1, w_in_l, S)
        dw_rep = jnp.broadcast_to(dw_w[l][:, None, :], (CONV_WIDTH, 8, C))
        cact = _conv(proj, dw_rep, dw_b[l].reshape(1, C), conv_ln_g[l].reshape(1, C),
                     conv_ln_b[l].reshape(1, C), S, n_a // C, n_b // C)
        attn = _attn(proj, bias_tab, attn_sinks[l], S, n_q // A, n_k // KV, n_v // KV)
        xt = _mix(cact, attn, proj, xt, g1, w_conv_out[l].astype(BF16), w_attn_out[l].astype(BF16),
                  w_out[l].astype(BF16), S, n_gc // D, n_ga // D)

        keys = sub_keys[l].reshape(2 * PEER_HEADS, N_KEYS, PEER_HALF).astype(BF16)
        h2, st = _peer_q(xt, norm2_g[l].reshape(1, D), sc2, sh2, w_pq[l].astype(BF16), keys, S)
        idxl, gate, ovf = _route(st, E // N_TABLE_BLOCKS)
        u_tab = _pack_table(peer_u, l)
        v_tab = _pack_table(peer_v, l)
        act = _peer_u(idxl, ovf, h2, row_sum, u_tab)
        pout = _peer_v(idxl, ovf, gate, act, expand, v_tab)
        xt = _resid(xt, pout, g2, final_g2, S, l == L - 1)
    return xt.reshape(B, S, D)
```

```python
import functools
import math

import numpy as np
import jax
import jax.numpy as jnp
from jax import lax
from jax.experimental import pallas as pl
from jax.experimental.pallas import tpu as pltpu

F32 = jnp.float32
BF16 = jnp.bfloat16
I32 = jnp.int32

EPS = 1e-6
CONV_WIDTH = 31
HALO = 32
N_HEADS = 16
N_KV_HEADS = 4
HEAD_DIM = 64
BLOCK = 128
NUM_BUCKETS = 32
MAX_DISTANCE = 128
N_KEYS = 128
PEER_HEADS = 8
PEER_TOPK = 16
PEER_HALF = 128
PAIRS = PEER_HEADS * PEER_TOPK
SLAB_ROWS = 16
LANES = 128
N_TABLE_BLOCKS = 2
TOKEN_UNROLL = 16
HEAD_UNROLL = 4
MXU_DEPTH = 256
SLAB_WORDS = 8
WINDOW_SLOTS = 80
assert PAIRS & (PAIRS - 1) == 0 and WINDOW_SLOTS % 8 == 0 and 2 * WINDOW_SLOTS >= PAIRS
VMEM_LIMIT = 56 * 1024 * 1024


def _cparams(sem, vmem=VMEM_LIMIT):
    return pltpu.CompilerParams(dimension_semantics=sem, vmem_limit_bytes=vmem)


def _resident(block_shape, index_map):
    return pl.BlockSpec(block_shape, index_map, pipeline_mode=pl.Buffered(1))


def _ada_kernel(c_ref, w_ref, b_ref, o_ref):
    c = c_ref[...]
    cs = c * jax.nn.sigmoid(c)
    o_ref[0] = jnp.dot(cs.astype(BF16), w_ref[0].astype(BF16), preferred_element_type=F32) + b_ref[0]


def _ada(c_pad, ada_w, ada_b):
    L, D, N = ada_w.shape
    tn = 1024
    return pl.pallas_call(
        _ada_kernel,
        grid=(L, N // tn),
        in_specs=[
            pl.BlockSpec((8, D), lambda l, j: (0, 0)),
            pl.BlockSpec((1, D, tn), lambda l, j: (l, 0, j)),
            pl.BlockSpec((1, 1, tn), lambda l, j: (l, 0, j)),
        ],
        out_specs=pl.BlockSpec((1, 8, tn), lambda l, j: (l, 0, j)),
        out_shape=jax.ShapeDtypeStruct((L, 8, N), F32),
        compiler_params=_cparams(("parallel", "parallel")),
        name="ada_mod",
    )(c_pad, ada_w, ada_b.reshape(L, 1, N))


def _bias_kernel(bucket_ref, rb_ref, o_ref):
    h = pl.program_id(0)
    bucket = bucket_ref[...]
    acc = jnp.zeros(bucket.shape, F32)
    for b in range(NUM_BUCKETS):
        acc = jnp.where(bucket == b, rb_ref[b, h], acc)
    o_ref[0] = acc


def _t5_bucket_table():
    qi = np.arange(BLOCK)[:, None] + BLOCK
    kj = np.arange(2 * BLOCK)[None, :]
    d = np.maximum(qi - kj, 0)
    max_exact = NUM_BUCKETS // 2
    ratio = (np.log(np.maximum(d, 1).astype(np.float64) / max_exact)
             / math.log(MAX_DISTANCE / max_exact) * (NUM_BUCKETS - max_exact))
    large = np.minimum(max_exact + ratio.astype(np.int64), NUM_BUCKETS - 1)
    return np.where(d < max_exact, d, large).astype(np.int32)


def _rel_bias_table(rel_bias):
    bucket = jnp.asarray(_t5_bucket_table())
    return pl.pallas_call(
        _bias_kernel,
        grid=(N_HEADS,),
        in_specs=[
            pl.BlockSpec((BLOCK, 2 * BLOCK), lambda h: (0, 0)),
            pl.BlockSpec(memory_space=pltpu.SMEM),
        ],
        out_specs=pl.BlockSpec((1, BLOCK, 2 * BLOCK), lambda h: (h, 0, 0)),
        out_shape=jax.ShapeDtypeStruct((N_HEADS, BLOCK, 2 * BLOCK), F32),
        compiler_params=_cparams(("arbitrary",)),
        name="rel_bias_table",
    )(bucket, rel_bias)


def _norm_mod(x, g, sc, sh):
    ms = jnp.mean(x * x, axis=-1, keepdims=True)
    y = x * lax.rsqrt(ms + EPS)
    return (y * g) * (1.0 + sc) + sh


def _inproj_kernel(x_ref, g_ref, sc_ref, sh_ref, w_ref, o_ref, h_scr):
    @pl.when(pl.program_id(1) == 0)
    def _():
        h_scr[...] = _norm_mod(x_ref[...], g_ref[...], sc_ref[0], sh_ref[0]).astype(BF16)

    o_ref[...] = jnp.dot(h_scr[...], w_ref[...], preferred_element_type=F32)


def _inproj(x, g, sc, sh, w, seq):
    T, D = x.shape
    N = w.shape[1]
    tm, tn = 1024, 1536
    per_seq = seq // tm
    return pl.pallas_call(
        _inproj_kernel,
        grid=(T // tm, N // tn),
        in_specs=[
            pl.BlockSpec((tm, D), lambda i, j: (i, 0)),
            pl.BlockSpec((1, D), lambda i, j: (0, 0)),
            pl.BlockSpec((1, 1, D), lambda i, j: (i // per_seq, 0, 0)),
            pl.BlockSpec((1, 1, D), lambda i, j: (i // per_seq, 0, 0)),
            pl.BlockSpec((D, tn), lambda i, j: (0, j)),
        ],
        out_specs=pl.BlockSpec((tm, tn), lambda i, j: (i, j)),
        out_shape=jax.ShapeDtypeStruct((T, N), F32),
        scratch_shapes=[pltpu.VMEM((tm, D), BF16)],
        compiler_params=_cparams(("parallel", "arbitrary")),
        name="in_proj",
    )(x, g, sc, sh, w)


def _conv_kernel(per_seq, a_ref, b_ref, ha_ref, hb_ref, dw_ref, db_ref, lg_ref, lb_ref, o_ref, ext, acc_scr):
    ts = a_ref.shape[0]
    first = (pl.program_id(0) % per_seq) == 0
    hu = ha_ref[...] * jax.nn.sigmoid(hb_ref[...])
    ext[0:HALO, :] = jnp.where(first, 0.0, hu)
    ext[HALO:, :] = a_ref[...] * jax.nn.sigmoid(b_ref[...])
    base = HALO - (CONV_WIDTH - 1)
    n_ch = a_ref.shape[1]
    for c0 in range(0, n_ch, LANES):
        acc = jnp.zeros((ts // 8, 8, LANES), F32)
        for r in range(8):
            offs = [o for o in range(base, base + CONV_WIDTH) if o % 8 == r]
            if not offs:
                continue
            shifted = ext[r:r + 8 * (max(offs) // 8) + ts, c0:c0 + LANES]
            for o in offs:
                tap = dw_ref[o - base, :, c0:c0 + LANES]
                acc = acc + shifted[o - r:o - r + ts, :].reshape(ts // 8, 8, LANES) * tap[None]
        acc_scr[:, c0:c0 + LANES] = acc.reshape(ts, LANES)
    acc = acc_scr[...] + db_ref[...]
    mu = jnp.mean(acc, axis=-1, keepdims=True)
    cen = acc - mu
    var = jnp.mean(cen * cen, axis=-1, keepdims=True)
    y = cen * lax.rsqrt(var + EPS) * lg_ref[...] + lb_ref[...]
    o_ref[...] = (y * jax.nn.sigmoid(y)).astype(BF16)


def _conv(proj, dw_w, dw_b, ln_g, ln_b, seq, a_col, b_col):
    T = proj.shape[0]
    C = dw_w.shape[-1]
    ts = 256
    per_seq = seq // ts
    rb = ts // HALO
    halo_map = lambda col: (lambda i: (jnp.maximum(i * rb - 1, 0), col))
    return pl.pallas_call(
        functools.partial(_conv_kernel, per_seq),
        grid=(T // ts,),
        in_specs=[
            pl.BlockSpec((ts, C), lambda i: (i, a_col)),
            pl.BlockSpec((ts, C), lambda i: (i, b_col)),
            pl.BlockSpec((HALO, C), halo_map(a_col)),
            pl.BlockSpec((HALO, C), halo_map(b_col)),
            pl.BlockSpec((CONV_WIDTH, 8, C), lambda i: (0, 0, 0)),
            pl.BlockSpec((1, C), lambda i: (0, 0)),
            pl.BlockSpec((1, C), lambda i: (0, 0)),
            pl.BlockSpec((1, C), lambda i: (0, 0)),
        ],
        out_specs=pl.BlockSpec((ts, C), lambda i: (i, 0)),
        out_shape=jax.ShapeDtypeStruct((T, C), BF16),
        scratch_shapes=[pltpu.VMEM((HALO + ts, C), F32), pltpu.VMEM((ts, C), F32)],
        compiler_params=_cparams(("parallel",)),
        name="conv_branch",
    )(proj, proj, proj, proj, dw_w, dw_b, ln_g, ln_b)


def _attn_kernel(nb, q_ref, kc_ref, kp_ref, vc_ref, vp_ref, bias_ref, sink_ref, o_ref):
    first = (pl.program_id(0) % nb) == 0
    kk = jnp.concatenate([kp_ref[...], kc_ref[...]], axis=0).astype(BF16)
    vv = jnp.concatenate([vp_ref[...], vc_ref[...]], axis=0).astype(BF16)
    row = lax.broadcasted_iota(I32, (BLOCK, 2 * BLOCK), 0)
    col = lax.broadcasted_iota(I32, (BLOCK, 2 * BLOCK), 1)
    dist = row + BLOCK - col
    valid = (dist >= 0) & (dist < BLOCK) & (jnp.logical_not(first) | (col >= BLOCK))
    group = N_HEADS // N_KV_HEADS
    for kvh in range(N_KV_HEADS):
        k_h = kk[:, kvh * HEAD_DIM:(kvh + 1) * HEAD_DIM]
        v_h = vv[:, kvh * HEAD_DIM:(kvh + 1) * HEAD_DIM]
        qg = q_ref[:, kvh * group * HEAD_DIM:(kvh + 1) * group * HEAD_DIM].astype(BF16)
        outs = []
        for g in range(group):
            h = kvh * group + g
            q_h = qg[:, g * HEAD_DIM:(g + 1) * HEAD_DIM]
            s = lax.dot_general(q_h, k_h, (((1,), (1,)), ((), ())), preferred_element_type=F32)
            s = s * (HEAD_DIM ** -0.5) + bias_ref[h]
            s = jnp.where(valid, s, -jnp.inf)
            sink = sink_ref[h]
            m = jnp.maximum(jnp.max(s, axis=-1, keepdims=True), sink)
            e = jnp.exp(s - m)
            denom = jnp.sum(e, axis=-1, keepdims=True) + jnp.exp(sink - m)
            p = e / denom
            outs.append(jnp.dot(p.astype(BF16), v_h, preferred_element_type=F32))
        o_ref[:, kvh * group * HEAD_DIM:(kvh + 1) * group * HEAD_DIM] = (
            jnp.concatenate(outs, axis=1).astype(BF16))


def _attn(proj, bias_tab, sinks, seq, q_col, k_col, v_col):
    T = proj.shape[0]
    nb = seq // BLOCK
    attn_dim = N_HEADS * HEAD_DIM
    kv_dim = N_KV_HEADS * HEAD_DIM
    prev = lambda col: (lambda i: (jnp.where(i % nb == 0, i, i - 1), col))
    return pl.pallas_call(
        functools.partial(_attn_kernel, nb),
        grid=(T // BLOCK,),
        in_specs=[
            pl.BlockSpec((BLOCK, attn_dim), lambda i: (i, q_col)),
            pl.BlockSpec((BLOCK, kv_dim), lambda i: (i, k_col)),
            pl.BlockSpec((BLOCK, kv_dim), prev(k_col)),
            pl.BlockSpec((BLOCK, kv_dim), lambda i: (i, v_col)),
            pl.BlockSpec((BLOCK, kv_dim), prev(v_col)),
            pl.BlockSpec((N_HEADS, BLOCK, 2 * BLOCK), lambda i: (0, 0, 0)),
            pl.BlockSpec(memory_space=pltpu.SMEM),
        ],
        out_specs=pl.BlockSpec((BLOCK, attn_dim), lambda i: (i, 0)),
        out_shape=jax.ShapeDtypeStruct((T, attn_dim), BF16),
        compiler_params=_cparams(("parallel",)),
        name="swa_attention",
    )(proj, proj, proj, proj, proj, bias_tab, sinks)


def _mix_kernel(ca_ref, at_ref, gc_ref, ga_ref, x_ref, g1_ref, wc_ref, wa_ref, wo_ref, o_ref):
    yc = jnp.dot(ca_ref[...], wc_ref[...], preferred_element_type=F32)
    ya = jnp.dot(at_ref[...], wa_ref[...], preferred_element_type=F32)
    mixed = jax.nn.sigmoid(gc_ref[...]) * yc + jax.nn.sigmoid(ga_ref[...]) * ya
    y = jnp.dot(mixed.astype(BF16), wo_ref[...], preferred_element_type=F32)
    o_ref[...] = x_ref[...] + g1_ref[0] * y


def _mix(cact, attn, proj, x, g1, wc, wa, wo, seq, gc_col, ga_col):
    T, D = x.shape
    C = cact.shape[1]
    A = attn.shape[1]
    tm = 256
    per_seq = seq // tm
    return pl.pallas_call(
        _mix_kernel,
        grid=(T // tm,),
        in_specs=[
            pl.BlockSpec((tm, C), lambda i: (i, 0)),
            pl.BlockSpec((tm, A), lambda i: (i, 0)),
            pl.BlockSpec((tm, D), lambda i: (i, gc_col)),
            pl.BlockSpec((tm, D), lambda i: (i, ga_col)),
            pl.BlockSpec((tm, D), lambda i: (i, 0)),
            pl.BlockSpec((1, 1, D), lambda i: (i // per_seq, 0, 0)),
            _resident((C, D), lambda i: (0, 0)),
            _resident((A, D), lambda i: (0, 0)),
            _resident((D, D), lambda i: (0, 0)),
        ],
        out_specs=pl.BlockSpec((tm, D), lambda i: (i, 0)),
        out_shape=jax.ShapeDtypeStruct((T, D), F32),
        compiler_params=_cparams(("parallel",)),
        name="merge_out_proj",
    )(cact, attn, proj, proj, x, g1, wc, wa, wo)


def _store_packed_slabs(o_ref, xb):
    w = lax.bitcast_convert_type(xb.astype(F32), jnp.uint32)
    for i in range(SLAB_WORDS):
        lo = w[:, (2 * i) * LANES:(2 * i + 1) * LANES]
        hi = w[:, (2 * i + 1) * LANES:(2 * i + 2) * LANES]
        o_ref[:, i, :] = hi | (lo >> 16)


def _pack_kernel(x_ref, o_ref):
    _store_packed_slabs(o_ref, x_ref[0].astype(BF16))


def _pack_table(tabs, layer):
    _, n, d = tabs.shape
    te = 256
    packed = pl.pallas_call(
        _pack_kernel,
        grid=(n // te,),
        in_specs=[pl.BlockSpec((1, te, d), lambda i: (layer, i, 0))],
        out_specs=pl.BlockSpec((te, SLAB_WORDS, LANES), lambda i: (i, 0, 0)),
        out_shape=jax.ShapeDtypeStruct((n, SLAB_WORDS, LANES), jnp.uint32),
        compiler_params=_cparams(("parallel",)),
        name="pack_expert_table",
    )(tabs)
    return packed.reshape(n * SLAB_WORDS, LANES)


def _peer_q_kernel(x_ref, g_ref, sc_ref, sh_ref, w_ref, k_ref, h_ref, st_ref):
    hb = _norm_mod(x_ref[...], g_ref[...], sc_ref[0], sh_ref[0]).astype(BF16)
    _store_packed_slabs(h_ref, hb)
    qb = jnp.dot(hb, w_ref[...], preferred_element_type=F32).astype(BF16)
    for hp in range(2 * PEER_HEADS):
        q_hp = qb[:, hp * PEER_HALF:(hp + 1) * PEER_HALF]
        st_ref[hp * N_KEYS:(hp + 1) * N_KEYS, :] = lax.dot_general(
            k_ref[hp], q_hp, (((1,), (1,)), ((), ())), preferred_element_type=F32)


def _peer_q(x, g, sc, sh, w_pq, keys, seq):
    T, D = x.shape
    Q = w_pq.shape[1]
    tm = 256
    per_seq = seq // tm
    n_rows = 2 * PEER_HEADS * N_KEYS
    return pl.pallas_call(
        _peer_q_kernel,
        grid=(T // tm,),
        in_specs=[
            pl.BlockSpec((tm, D), lambda i: (i, 0)),
            pl.BlockSpec((1, D), lambda i: (0, 0)),
            pl.BlockSpec((1, 1, D), lambda i: (i // per_seq, 0, 0)),
            pl.BlockSpec((1, 1, D), lambda i: (i // per_seq, 0, 0)),
            _resident((D, Q), lambda i: (0, 0)),
            _resident((2 * PEER_HEADS, N_KEYS, PEER_HALF), lambda i: (0, 0, 0)),
        ],
        out_specs=[
            pl.BlockSpec((tm, SLAB_WORDS, LANES), lambda i: (i, 0, 0)),
            pl.BlockSpec((n_rows, tm), lambda i: (0, i)),
        ],
        out_shape=[
            jax.ShapeDtypeStruct((T, SLAB_WORDS, LANES), jnp.uint32),
            jax.ShapeDtypeStruct((n_rows, T), F32),
        ],
        compiler_params=_cparams(("parallel",)),
        name="peer_query_scores",
    )(x, g, sc, sh, w_pq, keys)


def _top_rows(x, payload, k, rank=None):
    rows = (lax.broadcasted_iota(I32, x.shape, 0) if rank is None else rank).astype(F32)
    vals, pays = [], []
    for _ in range(k):
        m = jnp.max(x, axis=0, keepdims=True)
        first = jnp.min(jnp.where(x == m, rows, jnp.float32(1 << 24)), axis=0, keepdims=True)
        onehot = rows == first
        if payload is None:
            pays.append(first.astype(I32))
        else:
            pays.append(jnp.sum(jnp.where(onehot, payload, 0), axis=0, keepdims=True))
        vals.append(m)
        x = jnp.where(onehot, -jnp.inf, x)
    return jnp.concatenate(vals, axis=0), jnp.concatenate(pays, axis=0)


def _sort_rows(keys, pays):
    n_groups = len(keys)
    n_rows = 8 * n_groups
    sub = lax.broadcasted_iota(I32, keys[0].shape, 0)
    k = 2
    while k <= n_rows:
        j = k // 2
        while j >= 1:
            if j >= 8:
                gj = j // 8
                for lo in range(n_groups):
                    if lo & gj:
                        continue
                    hi = lo | gj
                    ascending = ((8 * lo) & k) == 0
                    a, b = keys[lo], keys[hi]
                    swap = (a > b) if ascending else (a < b)
                    keys[lo], keys[hi] = jnp.where(swap, b, a), jnp.where(swap, a, b)
                    pa, pb = pays[lo], pays[hi]
                    pays[lo], pays[hi] = jnp.where(swap, pb, pa), jnp.where(swap, pa, pb)
            else:
                lower = (sub & j) == 0
                upper = (sub & j) != 0
                for g in range(n_groups):
                    if k >= 8:
                        ascending = ((8 * g) & k) == 0
                        want_max = upper if ascending else lower
                    else:
                        want_max = jnp.logical_xor(upper, (sub & k) != 0)
                    x, p = keys[g], pays[g]
                    xp = jnp.where(lower, pltpu.roll(x, 8 - j, 0), pltpu.roll(x, j, 0))
                    pp = jnp.where(lower, pltpu.roll(p, 8 - j, 0), pltpu.roll(p, j, 0))
                    swap = jnp.logical_xor(xp < x, want_max)
                    keys[g] = jnp.where(swap, xp, x)
                    pays[g] = jnp.where(swap, pp, p)
            j //= 2
        k *= 2
    return keys, pays


def _route_kernel(block_rows, st_ref, idxl_ref, gate_ref, ovf_ref, idx_scr, gate_scr):
    def head(h, carry):
        base = pl.multiple_of(h * (2 * N_KEYS), 2 * N_KEYS)
        v0, i0 = _top_rows(st_ref[pl.ds(base, N_KEYS), :], None, PEER_TOPK)
        v1, i1 = _top_rows(st_ref[pl.ds(base + N_KEYS, N_KEYS), :], None, PEER_TOPK)
        half = PEER_TOPK // 2
        sub = lax.broadcasted_iota(I32, (half, v0.shape[1]), 0)
        cand = [v0[k:k + 1] + v1[:half] for k in range(half)]
        cidx = [i0[k:k + 1] * N_KEYS + i1[:half] for k in range(half)]
        flat = [sub + k * PEER_TOPK for k in range(half)]
        cand += [v0[:1] + v1[half:], v0[half:] + v1[:1]]
        cidx += [i0[:1] * N_KEYS + i1[half:], i0[half:] * N_KEYS + i1[:1]]
        flat += [sub + half, (sub + half) * PEER_TOPK]
        best, eid = _top_rows(jnp.concatenate(cand, axis=0), jnp.concatenate(cidx, axis=0), PEER_TOPK,
                              rank=jnp.concatenate(flat, axis=0))
        e = jnp.exp(best - jnp.max(best, axis=0, keepdims=True))
        gates = e / jnp.sum(e, axis=0, keepdims=True)
        off = pl.multiple_of(h * PEER_TOPK, PEER_TOPK)
        idx_scr[pl.ds(off, PEER_TOPK), :] = eid
        gate_scr[pl.ds(off, PEER_TOPK), :] = gates
        return carry

    lax.fori_loop(0, PEER_HEADS, head, 0, unroll=HEAD_UNROLL)

    n_groups = PAIRS // 8
    sub = lax.broadcasted_iota(I32, (8, idx_scr.shape[1]), 0)
    keys = [idx_scr[8 * g:8 * g + 8, :] * PAIRS + (sub + 8 * g) for g in range(n_groups)]
    pays = [gate_scr[8 * g:8 * g + 8, :] for g in range(n_groups)]
    keys, pays = _sort_rows(keys, pays)
    eids = [jnp.right_shift(k, PAIRS.bit_length() - 1) for k in keys]
    row_hi, row_lo = WINDOW_SLOTS, PAIRS - WINDOW_SLOTS - 1
    ovf0 = jnp.where(eids[row_hi // 8][row_hi % 8:row_hi % 8 + 1, :] < block_rows, 1, 0)
    ovf1 = jnp.where(eids[row_lo // 8][row_lo % 8:row_lo % 8 + 1, :] >= block_rows, 1, 0)
    ovf_ref[...] = jnp.concatenate([ovf0, ovf1], axis=0).astype(I32)
    rot = (PAIRS - WINDOW_SLOTS) // 8
    for b in range(N_TABLE_BLOCKS):
        order = [(g + b * rot) % n_groups for g in range(n_groups)]
        e = jnp.concatenate([eids[g] for g in order], axis=0)
        inb = (e >= b * block_rows) & (e < (b + 1) * block_rows)
        local = jnp.where(inb, (e - b * block_rows) * SLAB_WORDS, 0)
        gate = jnp.where(inb, jnp.concatenate([pays[g] for g in order], axis=0), 0.0)
        idxl_ref[b] = local.astype(F32).T.astype(I32)
        gate_ref[b] = gate.T


def _route(st, block_rows):
    n_rows, T = st.shape
    tr = 128
    return pl.pallas_call(
        functools.partial(_route_kernel, block_rows),
        grid=(T // tr,),
        in_specs=[pl.BlockSpec((n_rows, tr), lambda i: (0, i))],
        out_specs=[
            pl.BlockSpec((N_TABLE_BLOCKS, tr, PAIRS), lambda i: (0, i, 0)),
            pl.BlockSpec((N_TABLE_BLOCKS, tr, PAIRS), lambda i: (0, i, 0)),
            pl.BlockSpec((N_TABLE_BLOCKS, tr), lambda i: (0, i)),
        ],
        out_shape=[
            jax.ShapeDtypeStruct((N_TABLE_BLOCKS, T, PAIRS), I32),
            jax.ShapeDtypeStruct((N_TABLE_BLOCKS, T, PAIRS), F32),
            jax.ShapeDtypeStruct((N_TABLE_BLOCKS, T), I32),
        ],
        scratch_shapes=[pltpu.VMEM((PAIRS, tr), I32), pltpu.VMEM((PAIRS, tr), F32)],
        compiler_params=_cparams(("parallel",)),
        name="peer_route",
    )(st)


def _diag_mask():
    r = lax.broadcasted_iota(I32, (SLAB_ROWS, PAIRS * SLAB_ROWS), 0)
    c = lax.broadcasted_iota(I32, (SLAB_ROWS, PAIRS * SLAB_ROWS), 1)
    return r == (c % SLAB_ROWS)


def _for_overflow_tokens(ovf_ref, b, tm, body):
    group = 8

    def step(g, carry):
        flags = [ovf_ref[b, g * group + i] for i in range(group)]
        any_set = functools.reduce(lambda x, y: x | y, flags)

        @pl.when(any_set != 0)
        def _():
            def one(i, c):
                t = g * group + i

                @pl.when(ovf_ref[b, t] != 0)
                def _():
                    body(t)
                return c

            lax.fori_loop(0, group, one, 0)
        return carry

    lax.fori_loop(0, tm // group, step, 0)


def _gather_slabs(tab_ref, idx_ref, t, slots):
    words = [tab_ref[pl.ds(pl.multiple_of(idx_ref[0, t, j], SLAB_WORDS), SLAB_WORDS), :] for j in slots]
    return pltpu.bitcast(jnp.concatenate(words, axis=0), BF16)


def _peer_u_kernel(idx_ref, ovf_ref, h_ref, sel_ref, u_ref, o_ref, part_scr):
    tm = h_ref.shape[0]
    b = pl.program_id(0)
    sel = sel_ref[...]
    per_tile = sel.shape[0]
    hi_mask = jnp.uint32(0xFFFF0000)

    def halves(words):
        return (lax.bitcast_convert_type(words << 16, F32), lax.bitcast_convert_type(words & hi_mask, F32))

    def lane_partials(t, slots):
        h_even, h_odd = halves(h_ref[t])
        sums = []
        for j in slots:
            u_even, u_odd = halves(u_ref[pl.ds(pl.multiple_of(idx_ref[0, t, j], SLAB_WORDS), SLAB_WORDS), :])
            sums.append(u_even * h_even + u_odd * h_odd)
        for k in range(0, len(sums), per_tile):
            grp = sums[k:k + per_tile]
            part_scr[t, slots[k]:slots[k] + len(grp), :] = jnp.dot(
                sel[:len(grp), :SLAB_WORDS * len(grp)], jnp.concatenate(grp, axis=0).astype(BF16),
                preferred_element_type=F32)

    def gather_trip(c):
        for i in range(TOKEN_UNROLL):
            lane_partials(c * TOKEN_UNROLL + i, range(WINDOW_SLOTS))

    def lane_sum_trip(c):
        for k in range(0, TOKEN_UNROLL, 8):
            r = pl.multiple_of(c * TOKEN_UNROLL + k, 8)
            o_ref[0, pl.ds(r, 8), 0:WINDOW_SLOTS] = jnp.sum(part_scr[pl.ds(r, 8), 0:WINDOW_SLOTS, :], axis=-1)

    def rare(t):
        lane_partials(t, range(WINDOW_SLOTS, PAIRS))
        o_ref[0, pl.ds(t, 1), WINDOW_SLOTS:] = jnp.sum(part_scr[pl.ds(t, 1), WINDOW_SLOTS:, :], axis=-1)

    def trip(c, carry):
        gather_trip(c)
        lane_sum_trip(c - 1)
        return carry

    n_trips = tm // TOKEN_UNROLL
    o_ref[0, :, WINDOW_SLOTS:] = jnp.zeros((tm, PAIRS - WINDOW_SLOTS), F32)
    gather_trip(0)
    lax.fori_loop(1, n_trips, trip, 0)
    lane_sum_trip(n_trips - 1)
    _for_overflow_tokens(ovf_ref, b, tm, rare)


def _peer_u(idxl, ovf, h_slab, row_sum, u_tab):
    T = h_slab.shape[0]
    block_words = u_tab.shape[0] // N_TABLE_BLOCKS
    tm = 128
    return pl.pallas_call(
        _peer_u_kernel,
        grid=(N_TABLE_BLOCKS, T // tm),
        in_specs=[
            pl.BlockSpec((1, tm, PAIRS), lambda b, i: (b, i, 0), memory_space=pltpu.SMEM),
            pl.BlockSpec((N_TABLE_BLOCKS, tm), lambda b, i: (0, i), memory_space=pltpu.SMEM),
            pl.BlockSpec((tm, SLAB_WORDS, LANES), lambda b, i: (i, 0, 0)),
            _resident(row_sum.shape, lambda b, i: (0, 0)),
            _resident((block_words, LANES), lambda b, i: (b, 0)),
        ],
        out_specs=pl.BlockSpec((1, tm, PAIRS), lambda b, i: (b, i, 0)),
        out_shape=jax.ShapeDtypeStruct((N_TABLE_BLOCKS, T, PAIRS), F32),
        scratch_shapes=[pltpu.VMEM((tm, PAIRS, LANES), F32)],
        compiler_params=_cparams(("arbitrary", "arbitrary")),
        name="peer_expert_u",
    )(idxl, ovf, h_slab, row_sum, u_tab)


def _peer_v_kernel(idx_ref, ovf_ref, gate_ref, act_ref, ex_ref, v_ref, o_ref, wexp):
    tm = gate_ref.shape[1]
    b = pl.program_id(0)
    act = act_ref[0]
    gelu = 0.5 * act * (1.0 + lax.erf(act * np.float32(math.sqrt(0.5))))
    w = gate_ref[0] * gelu
    wexp[...] = jnp.dot(w.astype(BF16), ex_ref[...], preferred_element_type=F32)
    eye = _diag_mask()
    wcols = WINDOW_SLOTS * SLAB_ROWS

    def weighted(t, slots, lo, hi):
        vsel = _gather_slabs(v_ref, idx_ref, t, slots)
        wrow = jnp.broadcast_to(wexp[pl.ds(t, 1), lo:hi], (SLAB_ROWS, hi - lo))
        wbd = jnp.where(eye[:, lo:hi], wrow, 0.0).astype(BF16)
        return jnp.dot(wbd, vsel, preferred_element_type=F32)

    def tok(t, carry):
        o_ref[0, t] = weighted(t, range(WINDOW_SLOTS), 0, wcols)
        return carry

    def rare(t):
        o_ref[0, t] = o_ref[0, t] + weighted(t, range(WINDOW_SLOTS, PAIRS), wcols, PAIRS * SLAB_ROWS)

    lax.fori_loop(0, tm, tok, 0, unroll=TOKEN_UNROLL)
    _for_overflow_tokens(ovf_ref, b, tm, rare)


def _peer_v(idxl, ovf, gate, act, expand, v_tab):
    T = gate.shape[1]
    block_words = v_tab.shape[0] // N_TABLE_BLOCKS
    tm = 128
    return pl.pallas_call(
        _peer_v_kernel,
        grid=(N_TABLE_BLOCKS, T // tm),
        in_specs=[
            pl.BlockSpec((1, tm, PAIRS), lambda b, i: (b, i, 0), memory_space=pltpu.SMEM),
            pl.BlockSpec((N_TABLE_BLOCKS, tm), lambda b, i: (0, i), memory_space=pltpu.SMEM),
            pl.BlockSpec((1, tm, PAIRS), lambda b, i: (b, i, 0)),
            pl.BlockSpec((1, tm, PAIRS), lambda b, i: (b, i, 0)),
            _resident((PAIRS, PAIRS * SLAB_ROWS), lambda b, i: (0, 0)),
            _resident((block_words, LANES), lambda b, i: (b, 0)),
        ],
        out_specs=pl.BlockSpec((1, tm, SLAB_ROWS, LANES), lambda b, i: (b, i, 0, 0)),
        out_shape=jax.ShapeDtypeStruct((N_TABLE_BLOCKS, T, SLAB_ROWS, LANES), F32),
        scratch_shapes=[pltpu.VMEM((tm, PAIRS * SLAB_ROWS), F32)],
        compiler_params=_cparams(("arbitrary", "arbitrary")),
        name="peer_expert_v",
    )(idxl, ovf, gate, act, expand, v_tab)


def _resid_kernel(final, x_ref, o_ref, g2_ref, fg_ref, y_ref):
    cols = []
    for s in range(SLAB_ROWS):
        part = o_ref[0, :, s, :]
        for k in range(1, N_TABLE_BLOCKS):
            part = part + o_ref[k, :, s, :]
        cols.append(part)
    x = x_ref[...] + g2_ref[0] * jnp.concatenate(cols, axis=1)
    if final:
        ms = jnp.mean(x * x, axis=-1, keepdims=True)
        x = (x * lax.rsqrt(ms + EPS)) * fg_ref[...]
    y_ref[...] = x


def _resid(x, peer_out, g2, final_g, seq, final):
    T, D = x.shape
    tm = 512
    per_seq = seq // tm
    return pl.pallas_call(
        functools.partial(_resid_kernel, final),
        grid=(T // tm,),
        in_specs=[
            pl.BlockSpec((tm, D), lambda i: (i, 0)),
            pl.BlockSpec((N_TABLE_BLOCKS, tm, SLAB_ROWS, LANES), lambda i: (0, i, 0, 0)),
            pl.BlockSpec((1, 1, D), lambda i: (i // per_seq, 0, 0)),
            pl.BlockSpec((1, D), lambda i: (0, 0)),
        ],
        out_specs=pl.BlockSpec((tm, D), lambda i: (i, 0)),
        out_shape=jax.ShapeDtypeStruct((T, D), F32),
        compiler_params=_cparams(("parallel",)),
        name="peer_residual",
    )(x, peer_out, g2, final_g)


def _pair_expand_matrix():
    p = np.arange(PAIRS)[:, None]
    c = np.arange(PAIRS * SLAB_ROWS)[None, :]
    return (p == c // SLAB_ROWS).astype(np.float32)


def kernel(x, c, rel_bias, ada_w, ada_b, norm1_g, w_in, dw_w, dw_b, conv_ln_g, conv_ln_b, w_conv_out,
           attn_sinks, w_attn_out, w_out, norm2_g, w_pq, sub_keys, peer_u, peer_v, final_g):
    B, S, D = x.shape
    L = ada_w.shape[0]
    T = B * S
    C = dw_w.shape[2]
    A = N_HEADS * HEAD_DIM
    KV = N_KV_HEADS * HEAD_DIM
    E = peer_u.shape[1]
    assert S % 1024 == 0 and D == SLAB_ROWS * LANES and E % N_TABLE_BLOCKS == 0

    o_a, o_b, o_q, o_k, o_v, o_g = 0, C, 2 * C, 2 * C + A, 2 * C + A + KV, 2 * C + A + 2 * KV
    order = np.concatenate([np.arange(o_a, o_a + 2 * C), np.arange(o_g, o_g + 2 * D),
                            np.arange(o_q, o_q + A + 2 * KV)])
    n_a, n_b, n_gc, n_ga, n_q, n_k, n_v = 0, C, 2 * C, 2 * C + D, 2 * C + 2 * D, 2 * C + 2 * D + A, 2 * C + 2 * D + A + KV

    c_pad = jnp.pad(c, ((0, 8 - B), (0, 0)))
    mod = _ada(c_pad, ada_w, ada_b)
    bias_tab = _rel_bias_table(rel_bias)
    expand = jnp.asarray(_pair_expand_matrix(), dtype=BF16)
    row_sum = jnp.asarray(np.arange(MXU_DEPTH)[None, :] // SLAB_WORDS == np.arange(MXU_DEPTH // SLAB_WORDS)[:, None],
                          dtype=BF16)
    final_g2 = final_g.reshape(1, D)

    xt = x.reshape(T, D)
    for l in range(L):
        m = mod[l, :B].reshape(B, 6, 1, D)
        sh1, sc1, g1, sh2, sc2, g2 = (m[:, k] for k in range(6))

        w_in_l = w_in[l][:, order].astype(BF16)
        proj = _inproj(xt, norm1_g[l].reshape(1, D), sc1, sh1, w_in_l, S)
        dw_rep = jnp.broadcast_to(dw_w[l][:, None, :], (CONV_WIDTH, 8, C))
        cact = _conv(proj, dw_rep, dw_b[l].reshape(1, C), conv_ln_g[l].reshape(1, C),
                     conv_ln_b[l].reshape(1, C), S, n_a // C, n_b // C)
        attn = _attn(proj, bias_tab, attn_sinks[l], S, n_q // A, n_k // KV, n_v // KV)
        xt = _mix(cact, attn, proj, xt, g1, w_conv_out[l].astype(BF16), w_attn_out[l].astype(BF16),
                  w_out[l].astype(BF16), S, n_gc // D, n_ga // D)

        keys = sub_keys[l].reshape(2 * PEER_HEADS, N_KEYS, PEER_HALF).astype(BF16)
        h2, st = _peer_q(xt, norm2_g[l].reshape(1, D), sc2, sh2, w_pq[l].astype(BF16), keys, S)
        idxl, gate, ovf = _route(st, E // N_TABLE_BLOCKS)
        u_tab = _pack_table(peer_u, l)
        v_tab = _pack_table(peer_v, l)
        act = _peer_u(idxl, ovf, h2, row_sum, u_tab)
        pout = _peer_v(idxl, ovf, gate, act, expand, v_tab)
        xt = _resid(xt, pout, g2, final_g2, S, l == L - 1)
    return xt.reshape(B, S, D)
```

```python
import functools
import math

import numpy as np
import jax
import jax.numpy as jnp
from jax import lax
from jax.experimental import pallas as pl
from jax.experimental.pallas import tpu as pltpu

F32 = jnp.float32
BF16 = jnp.bfloat16
I32 = jnp.int32

EPS = 1e-6
CONV_WIDTH = 31
HALO = 32
N_HEADS = 16
N_KV_HEADS = 4
HEAD_DIM = 64
BLOCK = 128
NUM_BUCKETS = 32
MAX_DISTANCE = 128
N_KEYS = 128
PEER_HEADS = 8
PEER_TOPK = 16
PEER_HALF = 128
PAIRS = PEER_HEADS * PEER_TOPK
SLAB_ROWS = 16
LANES = 128
N_TABLE_BLOCKS = 2
TOKEN_UNROLL = 8
HEAD_UNROLL = 4
MXU_DEPTH = 256
SLAB_WORDS = 8
WINDOW_SLOTS = 80
assert PAIRS & (PAIRS - 1) == 0 and WINDOW_SLOTS % 8 == 0 and 2 * WINDOW_SLOTS >= PAIRS
VMEM_LIMIT = 56 * 1024 * 1024


def _cparams(sem, vmem=VMEM_LIMIT):
    return pltpu.CompilerParams(dimension_semantics=sem, vmem_limit_bytes=vmem)


def _resident(block_shape, index_map):
    return pl.BlockSpec(block_shape, index_map, pipeline_mode=pl.Buffered(1))


def _ada_kernel(c_ref, w_ref, b_ref, o_ref):
    c = c_ref[...]
    cs = c * jax.nn.sigmoid(c)
    o_ref[0] = jnp.dot(cs.astype(BF16), w_ref[0].astype(BF16), preferred_element_type=F32) + b_ref[0]


def _ada(c_pad, ada_w, ada_b):
    L, D, N = ada_w.shape
    tn = 1024
    return pl.pallas_call(
        _ada_kernel,
        grid=(L, N // tn),
        in_specs=[
            pl.BlockSpec((8, D), lambda l, j: (0, 0)),
            pl.BlockSpec((1, D, tn), lambda l, j: (l, 0, j)),
            pl.BlockSpec((1, 1, tn), lambda l, j: (l, 0, j)),
        ],
        out_specs=pl.BlockSpec((1, 8, tn), lambda l, j: (l, 0, j)),
        out_shape=jax.ShapeDtypeStruct((L, 8, N), F32),
        compiler_params=_cparams(("parallel", "parallel")),
        name="ada_mod",
    )(c_pad, ada_w, ada_b.reshape(L, 1, N))


def _bias_kernel(bucket_ref, rb_ref, o_ref):
    h = pl.program_id(0)
    bucket = bucket_ref[...]
    acc = jnp.zeros(bucket.shape, F32)
    for b in range(NUM_BUCKETS):
        acc = jnp.where(bucket == b, rb_ref[b, h], acc)
    o_ref[0] = acc


def _t5_bucket_table():
    qi = np.arange(BLOCK)[:, None] + BLOCK
    kj = np.arange(2 * BLOCK)[None, :]
    d = np.maximum(qi - kj, 0)
    max_exact = NUM_BUCKETS // 2
    ratio = (np.log(np.maximum(d, 1).astype(np.float64) / max_exact)
             / math.log(MAX_DISTANCE / max_exact) * (NUM_BUCKETS - max_exact))
    large = np.minimum(max_exact + ratio.astype(np.int64), NUM_BUCKETS - 1)
    return np.where(d < max_exact, d, large).astype(np.int32)


def _rel_bias_table(rel_bias):
    bucket = jnp.asarray(_t5_bucket_table())
    return pl.pallas_call(
        _bias_kernel,
        grid=(N_HEADS,),
        in_specs=[
            pl.BlockSpec((BLOCK, 2 * BLOCK), lambda h: (0, 0)),
            pl.BlockSpec(memory_space=pltpu.SMEM),
        ],
        out_specs=pl.BlockSpec((1, BLOCK, 2 * BLOCK), lambda h: (h, 0, 0)),
        out_shape=jax.ShapeDtypeStruct((N_HEADS, BLOCK, 2 * BLOCK), F32),
        compiler_params=_cparams(("arbitrary",)),
        name="rel_bias_table",
    )(bucket, rel_bias)


def _norm_mod(x, g, sc, sh):
    ms = jnp.mean(x * x, axis=-1, keepdims=True)
    y = x * lax.rsqrt(ms + EPS)
    return (y * g) * (1.0 + sc) + sh


def _inproj_kernel(x_ref, g_ref, sc_ref, sh_ref, w_ref, o_ref, h_scr):
    @pl.when(pl.program_id(1) == 0)
    def _():
        h_scr[...] = _norm_mod(x_ref[...], g_ref[...], sc_ref[0], sh_ref[0]).astype(BF16)

    o_ref[...] = jnp.dot(h_scr[...], w_ref[...], preferred_element_type=F32)


def _inproj(x, g, sc, sh, w, seq):
    T, D = x.shape
    N = w.shape[1]
    tm, tn = 1024, 1536
    per_seq = seq // tm
    return pl.pallas_call(
        _inproj_kernel,
        grid=(T // tm, N // tn),
        in_specs=[
            pl.BlockSpec((tm, D), lambda i, j: (i, 0)),
            pl.BlockSpec((1, D), lambda i, j: (0, 0)),
            pl.BlockSpec((1, 1, D), lambda i, j: (i // per_seq, 0, 0)),
            pl.BlockSpec((1, 1, D), lambda i, j: (i // per_seq, 0, 0)),
            pl.BlockSpec((D, tn), lambda i, j: (0, j)),
        ],
        out_specs=pl.BlockSpec((tm, tn), lambda i, j: (i, j)),
        out_shape=jax.ShapeDtypeStruct((T, N), F32),
        scratch_shapes=[pltpu.VMEM((tm, D), BF16)],
        compiler_params=_cparams(("parallel", "arbitrary")),
        name="in_proj",
    )(x, g, sc, sh, w)


def _conv_kernel(per_seq, a_ref, b_ref, ha_ref, hb_ref, dw_ref, db_ref, lg_ref, lb_ref, o_ref, ext, acc_scr):
    ts = a_ref.shape[0]
    first = (pl.program_id(0) % per_seq) == 0
    hu = ha_ref[...] * jax.nn.sigmoid(hb_ref[...])
    ext[0:HALO, :] = jnp.where(first, 0.0, hu)
    ext[HALO:, :] = a_ref[...] * jax.nn.sigmoid(b_ref[...])
    base = HALO - (CONV_WIDTH - 1)
    n_ch = a_ref.shape[1]
    for c0 in range(0, n_ch, LANES):
        acc = jnp.zeros((ts // 8, 8, LANES), F32)
        for r in range(8):
            offs = [o for o in range(base, base + CONV_WIDTH) if o % 8 == r]
            if not offs:
                continue
            shifted = ext[r:r + 8 * (max(offs) // 8) + ts, c0:c0 + LANES]
            for o in offs:
                tap = dw_ref[o - base, :, c0:c0 + LANES]
                acc = acc + shifted[o - r:o - r + ts, :].reshape(ts // 8, 8, LANES) * tap[None]
        acc_scr[:, c0:c0 + LANES] = acc.reshape(ts, LANES)
    acc = acc_scr[...] + db_ref[...]
    mu = jnp.mean(acc, axis=-1, keepdims=True)
    cen = acc - mu
    var = jnp.mean(cen * cen, axis=-1, keepdims=True)
    y = cen * lax.rsqrt(var + EPS) * lg_ref[...] + lb_ref[...]
    o_ref[...] = (y * jax.nn.sigmoid(y)).astype(BF16)


def _conv(proj, dw_w, dw_b, ln_g, ln_b, seq, a_col, b_col):
    T = proj.shape[0]
    C = dw_w.shape[-1]
    ts = 256
    per_seq = seq // ts
    rb = ts // HALO
    halo_map = lambda col: (lambda i: (jnp.maximum(i * rb - 1, 0), col))
    return pl.pallas_call(
        functools.partial(_conv_kernel, per_seq),
        grid=(T // ts,),
        in_specs=[
            pl.BlockSpec((ts, C), lambda i: (i, a_col)),
            pl.BlockSpec((ts, C), lambda i: (i, b_col)),
            pl.BlockSpec((HALO, C), halo_map(a_col)),
            pl.BlockSpec((HALO, C), halo_map(b_col)),
            pl.BlockSpec((CONV_WIDTH, 8, C), lambda i: (0, 0, 0)),
            pl.BlockSpec((1, C), lambda i: (0, 0)),
            pl.BlockSpec((1, C), lambda i: (0, 0)),
            pl.BlockSpec((1, C), lambda i: (0, 0)),
        ],
        out_specs=pl.BlockSpec((ts, C), lambda i: (i, 0)),
        out_shape=jax.ShapeDtypeStruct((T, C), BF16),
        scratch_shapes=[pltpu.VMEM((HALO + ts, C), F32), pltpu.VMEM((ts, C), F32)],
        compiler_params=_cparams(("parallel",)),
        name="conv_branch",
    )(proj, proj, proj, proj, dw_w, dw_b, ln_g, ln_b)


def _attn_kernel(nb, q_ref, kc_ref, kp_ref, vc_ref, vp_ref, bias_ref, sink_ref, o_ref):
    first = (pl.program_id(0) % nb) == 0
    kk = jnp.concatenate([kp_ref[...], kc_ref[...]], axis=0).astype(BF16)
    vv = jnp.concatenate([vp_ref[...], vc_ref[...]], axis=0).astype(BF16)
    row = lax.broadcasted_iota(I32, (BLOCK, 2 * BLOCK), 0)
    col = lax.broadcasted_iota(I32, (BLOCK, 2 * BLOCK), 1)
    dist = row + BLOCK - col
    valid = (dist >= 0) & (dist < BLOCK) & (jnp.logical_not(first) | (col >= BLOCK))
    group = N_HEADS // N_KV_HEADS
    for kvh in range(N_KV_HEADS):
        k_h = kk[:, kvh * HEAD_DIM:(kvh + 1) * HEAD_DIM]
        v_h = vv[:, kvh * HEAD_DIM:(kvh + 1) * HEAD_DIM]
        qg = q_ref[:, kvh * group * HEAD_DIM:(kvh + 1) * group * HEAD_DIM].astype(BF16)
        outs = []
        for g in range(group):
            h = kvh * group + g
            q_h = qg[:, g * HEAD_DIM:(g + 1) * HEAD_DIM]
            s = lax.dot_general(q_h, k_h, (((1,), (1,)), ((), ())), preferred_element_type=F32)
            s = s * (HEAD_DIM ** -0.5) + bias_ref[h]
            s = jnp.where(valid, s, -jnp.inf)
            sink = sink_ref[h]
            m = jnp.maximum(jnp.max(s, axis=-1, keepdims=True), sink)
            e = jnp.exp(s - m)
            denom = jnp.sum(e, axis=-1, keepdims=True) + jnp.exp(sink - m)
            p = e / denom
            outs.append(jnp.dot(p.astype(BF16), v_h, preferred_element_type=F32))
        o_ref[:, kvh * group * HEAD_DIM:(kvh + 1) * group * HEAD_DIM] = (
            jnp.concatenate(outs, axis=1).astype(BF16))


def _attn(proj, bias_tab, sinks, seq, q_col, k_col, v_col):
    T = proj.shape[0]
    nb = seq // BLOCK
    attn_dim = N_HEADS * HEAD_DIM
    kv_dim = N_KV_HEADS * HEAD_DIM
    prev = lambda col: (lambda i: (jnp.where(i % nb == 0, i, i - 1), col))
    return pl.pallas_call(
        functools.partial(_attn_kernel, nb),
        grid=(T // BLOCK,),
        in_specs=[
            pl.BlockSpec((BLOCK, attn_dim), lambda i: (i, q_col)),
            pl.BlockSpec((BLOCK, kv_dim), lambda i: (i, k_col)),
            pl.BlockSpec((BLOCK, kv_dim), prev(k_col)),
            pl.BlockSpec((BLOCK, kv_dim), lambda i: (i, v_col)),
            pl.BlockSpec((BLOCK, kv_dim), prev(v_col)),
            pl.BlockSpec((N_HEADS, BLOCK, 2 * BLOCK), lambda i: (0, 0, 0)),
            pl.BlockSpec(memory_space=pltpu.SMEM),
        ],
        out_specs=pl.BlockSpec((BLOCK, attn_dim), lambda i: (i, 0)),
        out_shape=jax.ShapeDtypeStruct((T, attn_dim), BF16),
        compiler_params=_cparams(("parallel",)),
        name="swa_attention",
    )(proj, proj, proj, proj, proj, bias_tab, sinks)


def _mix_kernel(ca_ref, at_ref, gc_ref, ga_ref, x_ref, g1_ref, wc_ref, wa_ref, wo_ref, o_ref):
    yc = jnp.dot(ca_ref[...], wc_ref[...], preferred_element_type=F32)
    ya = jnp.dot(at_ref[...], wa_ref[...], preferred_element_type=F32)
    mixed = jax.nn.sigmoid(gc_ref[...]) * yc + jax.nn.sigmoid(ga_ref[...]) * ya
    y = jnp.dot(mixed.astype(BF16), wo_ref[...], preferred_element_type=F32)
    o_ref[...] = x_ref[...] + g1_ref[0] * y


def _mix(cact, attn, proj, x, g1, wc, wa, wo, seq, gc_col, ga_col):
    T, D = x.shape
    C = cact.shape[1]
    A = attn.shape[1]
    tm = 256
    per_seq = seq // tm
    return pl.pallas_call(
        _mix_kernel,
        grid=(T // tm,),
        in_specs=[
            pl.BlockSpec((tm, C), lambda i: (i, 0)),
            pl.BlockSpec((tm, A), lambda i: (i, 0)),
            pl.BlockSpec((tm, D), lambda i: (i, gc_col)),
            pl.BlockSpec((tm, D), lambda i: (i, ga_col)),
            pl.BlockSpec((tm, D), lambda i: (i, 0)),
            pl.BlockSpec((1, 1, D), lambda i: (i // per_seq, 0, 0)),
            _resident((C, D), lambda i: (0, 0)),
            _resident((A, D), lambda i: (0, 0)),
            _resident((D, D), lambda i: (0, 0)),
        ],
        out_specs=pl.BlockSpec((tm, D), lambda i: (i, 0)),
        out_shape=jax.ShapeDtypeStruct((T, D), F32),
        compiler_params=_cparams(("parallel",)),
        name="merge_out_proj",
    )(cact, attn, proj, proj, x, g1, wc, wa, wo)


def _store_packed_slabs(o_ref, xb):
    w = lax.bitcast_convert_type(xb.astype(F32), jnp.uint32)
    for i in range(SLAB_WORDS):
        lo = w[:, (2 * i) * LANES:(2 * i + 1) * LANES]
        hi = w[:, (2 * i + 1) * LANES:(2 * i + 2) * LANES]
        o_ref[:, i, :] = hi | (lo >> 16)


def _pack_kernel(x_ref, o_ref):
    _store_packed_slabs(o_ref, x_ref[0].astype(BF16))


def _pack_table(tabs, layer):
    _, n, d = tabs.shape
    te = 256
    packed = pl.pallas_call(
        _pack_kernel,
        grid=(n // te,),
        in_specs=[pl.BlockSpec((1, te, d), lambda i: (layer, i, 0))],
        out_specs=pl.BlockSpec((te, SLAB_WORDS, LANES), lambda i: (i, 0, 0)),
        out_shape=jax.ShapeDtypeStruct((n, SLAB_WORDS, LANES), jnp.uint32),
        compiler_params=_cparams(("parallel",)),
        name="pack_expert_table",
    )(tabs)
    return packed.reshape(n * SLAB_WORDS, LANES)


def _peer_q_kernel(x_ref, g_ref, sc_ref, sh_ref, w_ref, k_ref, h_ref, st_ref):
    hb = _norm_mod(x_ref[...], g_ref[...], sc_ref[0], sh_ref[0]).astype(BF16)
    _store_packed_slabs(h_ref, hb)
    qb = jnp.dot(hb, w_ref[...], preferred_element_type=F32).astype(BF16)
    for hp in range(2 * PEER_HEADS):
        q_hp = qb[:, hp * PEER_HALF:(hp + 1) * PEER_HALF]
        st_ref[hp * N_KEYS:(hp + 1) * N_KEYS, :] = lax.dot_general(
            k_ref[hp], q_hp, (((1,), (1,)), ((), ())), preferred_element_type=F32)


def _peer_q(x, g, sc, sh, w_pq, keys, seq):
    T, D = x.shape
    Q = w_pq.shape[1]
    tm = 256
    per_seq = seq // tm
    n_rows = 2 * PEER_HEADS * N_KEYS
    return pl.pallas_call(
        _peer_q_kernel,
        grid=(T // tm,),
        in_specs=[
            pl.BlockSpec((tm, D), lambda i: (i, 0)),
            pl.BlockSpec((1, D), lambda i: (0, 0)),
            pl.BlockSpec((1, 1, D), lambda i: (i // per_seq, 0, 0)),
            pl.BlockSpec((1, 1, D), lambda i: (i // per_seq, 0, 0)),
            _resident((D, Q), lambda i: (0, 0)),
            _resident((2 * PEER_HEADS, N_KEYS, PEER_HALF), lambda i: (0, 0, 0)),
        ],
        out_specs=[
            pl.BlockSpec((tm, SLAB_WORDS, LANES), lambda i: (i, 0, 0)),
            pl.BlockSpec((n_rows, tm), lambda i: (0, i)),
        ],
        out_shape=[
            jax.ShapeDtypeStruct((T, SLAB_WORDS, LANES), jnp.uint32),
            jax.ShapeDtypeStruct((n_rows, T), F32),
        ],
        compiler_params=_cparams(("parallel",)),
        name="peer_query_scores",
    )(x, g, sc, sh, w_pq, keys)


def _top_rows(x, payload, k, rank=None):
    rows = (lax.broadcasted_iota(I32, x.shape, 0) if rank is None else rank).astype(F32)
    vals, pays = [], []
    for _ in range(k):
        m = jnp.max(x, axis=0, keepdims=True)
        first = jnp.min(jnp.where(x == m, rows, jnp.float32(1 << 24)), axis=0, keepdims=True)
        onehot = rows == first
        if payload is None:
            pays.append(first.astype(I32))
        else:
            pays.append(jnp.sum(jnp.where(onehot, payload, 0), axis=0, keepdims=True))
        vals.append(m)
        x = jnp.where(onehot, -jnp.inf, x)
    return jnp.concatenate(vals, axis=0), jnp.concatenate(pays, axis=0)


def _sort_rows(keys, pays):
    n_groups = len(keys)
    n_rows = 8 * n_groups
    sub = lax.broadcasted_iota(I32, keys[0].shape, 0)
    k = 2
    while k <= n_rows:
        j = k // 2
        while j >= 1:
            if j >= 8:
                gj = j // 8
                for lo in range(n_groups):
                    if lo & gj:
                        continue
                    hi = lo | gj
                    ascending = ((8 * lo) & k) == 0
                    a, b = keys[lo], keys[hi]
                    swap = (a > b) if ascending else (a < b)
                    keys[lo], keys[hi] = jnp.where(swap, b, a), jnp.where(swap, a, b)
                    pa, pb = pays[lo], pays[hi]
                    pays[lo], pays[hi] = jnp.where(swap, pb, pa), jnp.where(swap, pa, pb)
            else:
                lower = (sub & j) == 0
                upper = (sub & j) != 0
                for g in range(n_groups):
                    if k >= 8:
                        ascending = ((8 * g) & k) == 0
                        want_max = upper if ascending else lower
                    else:
                        want_max = jnp.logical_xor(upper, (sub & k) != 0)
                    x, p = keys[g], pays[g]
                    xp = jnp.where(lower, pltpu.roll(x, 8 - j, 0), pltpu.roll(x, j, 0))
                    pp = jnp.where(lower, pltpu.roll(p, 8 - j, 0), pltpu.roll(p, j, 0))
                    swap = jnp.logical_xor(xp < x, want_max)
                    keys[g] = jnp.where(swap, xp, x)
                    pays[g] = jnp.where(swap, pp, p)
            j //= 2
        k *= 2
    return keys, pays


def _route_kernel(block_rows, st_ref, idxl_ref, gate_ref, ovf_ref, idx_scr, gate_scr):
    def head(h, carry):
        base = pl.multiple_of(h * (2 * N_KEYS), 2 * N_KEYS)
        v0, i0 = _top_rows(st_ref[pl.ds(base, N_KEYS), :], None, PEER_TOPK)
        v1, i1 = _top_rows(st_ref[pl.ds(base + N_KEYS, N_KEYS), :], None, PEER_TOPK)
        half = PEER_TOPK // 2
        sub = lax.broadcasted_iota(I32, (half, v0.shape[1]), 0)
        cand = [v0[k:k + 1] + v1[:half] for k in range(half)]
        cidx = [i0[k:k + 1] * N_KEYS + i1[:half] for k in range(half)]
        flat = [sub + k * PEER_TOPK for k in range(half)]
        cand += [v0[:1] + v1[half:], v0[half:] + v1[:1]]
        cidx += [i0[:1] * N_KEYS + i1[half:], i0[half:] * N_KEYS + i1[:1]]
        flat += [sub + half, (sub + half) * PEER_TOPK]
        best, eid = _top_rows(jnp.concatenate(cand, axis=0), jnp.concatenate(cidx, axis=0), PEER_TOPK,
                              rank=jnp.concatenate(flat, axis=0))
        e = jnp.exp(best - jnp.max(best, axis=0, keepdims=True))
        gates = e / jnp.sum(e, axis=0, keepdims=True)
        off = pl.multiple_of(h * PEER_TOPK, PEER_TOPK)
        idx_scr[pl.ds(off, PEER_TOPK), :] = eid
        gate_scr[pl.ds(off, PEER_TOPK), :] = gates
        return carry

    lax.fori_loop(0, PEER_HEADS, head, 0, unroll=HEAD_UNROLL)

    n_groups = PAIRS // 8
    sub = lax.broadcasted_iota(I32, (8, idx_scr.shape[1]), 0)
    keys = [idx_scr[8 * g:8 * g + 8, :] * PAIRS + (sub + 8 * g) for g in range(n_groups)]
    pays = [gate_scr[8 * g:8 * g + 8, :] for g in range(n_groups)]
    keys, pays = _sort_rows(keys, pays)
    eids = [jnp.right_shift(k, PAIRS.bit_length() - 1) for k in keys]
    row_hi, row_lo = WINDOW_SLOTS, PAIRS - WINDOW_SLOTS - 1
    ovf0 = jnp.where(eids[row_hi // 8][row_hi % 8:row_hi % 8 + 1, :] < block_rows, 1, 0)
    ovf1 = jnp.where(eids[row_lo // 8][row_lo % 8:row_lo % 8 + 1, :] >= block_rows, 1, 0)
    ovf_ref[...] = jnp.concatenate([ovf0, ovf1], axis=0).astype(I32)
    rot = (PAIRS - WINDOW_SLOTS) // 8
    for b in range(N_TABLE_BLOCKS):
        order = [(g + b * rot) % n_groups for g in range(n_groups)]
        e = jnp.concatenate([eids[g] for g in order], axis=0)
        inb = (e >= b * block_rows) & (e < (b + 1) * block_rows)
        local = jnp.where(inb, (e - b * block_rows) * SLAB_WORDS, 0)
        gate = jnp.where(inb, jnp.concatenate([pays[g] for g in order], axis=0), 0.0)
        idxl_ref[b] = local.astype(F32).T.astype(I32)
        gate_ref[b] = gate.T


def _route(st, block_rows):
    n_rows, T = st.shape
    tr = 128
    return pl.pallas_call(
        functools.partial(_route_kernel, block_rows),
        grid=(T // tr,),
        in_specs=[pl.BlockSpec((n_rows, tr), lambda i: (0, i))],
        out_specs=[
            pl.BlockSpec((N_TABLE_BLOCKS, tr, PAIRS), lambda i: (0, i, 0)),
            pl.BlockSpec((N_TABLE_BLOCKS, tr, PAIRS), lambda i: (0, i, 0)),
            pl.BlockSpec((N_TABLE_BLOCKS, tr), lambda i: (0, i)),
        ],
        out_shape=[
            jax.ShapeDtypeStruct((N_TABLE_BLOCKS, T, PAIRS), I32),
            jax.ShapeDtypeStruct((N_TABLE_BLOCKS, T, PAIRS), F32),
            jax.ShapeDtypeStruct((N_TABLE_BLOCKS, T), I32),
        ],
        scratch_shapes=[pltpu.VMEM((PAIRS, tr), I32), pltpu.VMEM((PAIRS, tr), F32)],
        compiler_params=_cparams(("parallel",)),
        name="peer_route",
    )(st)


def _diag_mask():
    r = lax.broadcasted_iota(I32, (SLAB_ROWS, PAIRS * SLAB_ROWS), 0)
    c = lax.broadcasted_iota(I32, (SLAB_ROWS, PAIRS * SLAB_ROWS), 1)
    return r == (c % SLAB_ROWS)


def _for_overflow_tokens(ovf_ref, b, tm, body):
    group = 8

    def step(g, carry):
        flags = [ovf_ref[b, g * group + i] for i in range(group)]
        any_set = functools.reduce(lambda x, y: x | y, flags)

        @pl.when(any_set != 0)
        def _():
            def one(i, c):
                t = g * group + i

                @pl.when(ovf_ref[b, t] != 0)
                def _():
                    body(t)
                return c

            lax.fori_loop(0, group, one, 0)
        return carry

    lax.fori_loop(0, tm // group, step, 0)


def _gather_slabs(tab_ref, idx_ref, t, slots):
    words = [tab_ref[pl.ds(pl.multiple_of(idx_ref[0, t, j], SLAB_WORDS), SLAB_WORDS), :] for j in slots]
    return pltpu.bitcast(jnp.concatenate(words, axis=0), BF16)


def _peer_u_kernel(idx_ref, ovf_ref, h_ref, sel_ref, u_ref, o_ref, part_scr):
    tm = h_ref.shape[0]
    b = pl.program_id(0)
    sel = sel_ref[...]
    per_tile = sel.shape[0]
    hi_mask = jnp.uint32(0xFFFF0000)

    def halves(words):
        return (lax.bitcast_convert_type(words << 16, F32), lax.bitcast_convert_type(words & hi_mask, F32))

    def lane_partials(t, slots):
        h_even, h_odd = halves(h_ref[t])
        sums = []
        for j in slots:
            u_even, u_odd = halves(u_ref[pl.ds(pl.multiple_of(idx_ref[0, t, j], SLAB_WORDS), SLAB_WORDS), :])
            sums.append(u_even * h_even + u_odd * h_odd)
        for k in range(0, len(sums), per_tile):
            grp = sums[k:k + per_tile]
            part_scr[t, slots[k]:slots[k] + len(grp), :] = jnp.dot(
                sel[:len(grp), :SLAB_WORDS * len(grp)], jnp.concatenate(grp, axis=0).astype(BF16),
                preferred_element_type=F32)

    def gather_trip(c):
        for i in range(TOKEN_UNROLL):
            lane_partials(c * TOKEN_UNROLL + i, range(WINDOW_SLOTS))

    def lane_sum_trip(c):
        for k in range(0, TOKEN_UNROLL, 8):
            r = pl.multiple_of(c * TOKEN_UNROLL + k, 8)
            o_ref[0, pl.ds(r, 8), 0:WINDOW_SLOTS] = jnp.sum(part_scr[pl.ds(r, 8), 0:WINDOW_SLOTS, :], axis=-1)

    def rare(t):
        lane_partials(t, range(WINDOW_SLOTS, PAIRS))
        o_ref[0, pl.ds(t, 1), WINDOW_SLOTS:] = jnp.sum(part_scr[pl.ds(t, 1), WINDOW_SLOTS:, :], axis=-1)

    def trip(c, carry):
        gather_trip(c)
        lane_sum_trip(c - 1)
        return carry

    n_trips = tm // TOKEN_UNROLL
    o_ref[0, :, WINDOW_SLOTS:] = jnp.zeros((tm, PAIRS - WINDOW_SLOTS), F32)
    gather_trip(0)
    lax.fori_loop(1, n_trips, trip, 0)
    lane_sum_trip(n_trips - 1)
    _for_overflow_tokens(ovf_ref, b, tm, rare)


def _peer_u(idxl, ovf, h_slab, row_sum, u_tab):
    T = h_slab.shape[0]
    block_words = u_tab.shape[0] // N_TABLE_BLOCKS
    tm = 128
    return pl.pallas_call(
        _peer_u_kernel,
        grid=(N_TABLE_BLOCKS, T // tm),
        in_specs=[
            pl.BlockSpec((1, tm, PAIRS), lambda b, i: (b, i, 0), memory_space=pltpu.SMEM),
            pl.BlockSpec((N_TABLE_BLOCKS, tm), lambda b, i: (0, i), memory_space=pltpu.SMEM),
            pl.BlockSpec((tm, SLAB_WORDS, LANES), lambda b, i: (i, 0, 0)),
            _resident(row_sum.shape, lambda b, i: (0, 0)),
            _resident((block_words, LANES), lambda b, i: (b, 0)),
        ],
        out_specs=pl.BlockSpec((1, tm, PAIRS), lambda b, i: (b, i, 0)),
        out_shape=jax.ShapeDtypeStruct((N_TABLE_BLOCKS, T, PAIRS), F32),
        scratch_shapes=[pltpu.VMEM((tm, PAIRS, LANES), F32)],
        compiler_params=_cparams(("arbitrary", "arbitrary")),
        name="peer_expert_u",
    )(idxl, ovf, h_slab, row_sum, u_tab)


def _peer_v_kernel(idx_ref, ovf_ref, gate_ref, act_ref, ex_ref, v_ref, o_ref, wexp):
    tm = gate_ref.shape[1]
    b = pl.program_id(0)
    act = act_ref[0]
    gelu = 0.5 * act * (1.0 + lax.erf(act * np.float32(math.sqrt(0.5))))
    w = gate_ref[0] * gelu
    wexp[...] = jnp.dot(w.astype(BF16), ex_ref[...], preferred_element_type=F32)
    eye = _diag_mask()
    wcols = WINDOW_SLOTS * SLAB_ROWS

    def weighted(t, slots, lo, hi):
        vsel = _gather_slabs(v_ref, idx_ref, t, slots)
        wrow = jnp.broadcast_to(wexp[pl.ds(t, 1), lo:hi], (SLAB_ROWS, hi - lo))
        wbd = jnp.where(eye[:, lo:hi], wrow, 0.0).astype(BF16)
        return jnp.dot(wbd, vsel, preferred_element_type=F32)

    def tok(t, carry):
        o_ref[0, t] = weighted(t, range(WINDOW_SLOTS), 0, wcols)
        return carry

    def rare(t):
        o_ref[0, t] = o_ref[0, t] + weighted(t, range(WINDOW_SLOTS, PAIRS), wcols, PAIRS * SLAB_ROWS)

    lax.fori_loop(0, tm, tok, 0, unroll=TOKEN_UNROLL)
    _for_overflow_tokens(ovf_ref, b, tm, rare)


def _peer_v(idxl, ovf, gate, act, expand, v_tab):
    T = gate.shape[1]
    block_words = v_tab.shape[0] // N_TABLE_BLOCKS
    tm = 128
    return pl.pallas_call(
        _peer_v_kernel,
        grid=(N_TABLE_BLOCKS, T // tm),
        in_specs=[
            pl.BlockSpec((1, tm, PAIRS), lambda b, i: (b, i, 0), memory_space=pltpu.SMEM),
            pl.BlockSpec((N_TABLE_BLOCKS, tm), lambda b, i: (0, i), memory_space=pltpu.SMEM),
            pl.BlockSpec((1, tm, PAIRS), lambda b, i: (b, i, 0)),
            pl.BlockSpec((1, tm, PAIRS), lambda b, i: (b, i, 0)),
            _resident((PAIRS, PAIRS * SLAB_ROWS), lambda b, i: (0, 0)),
            _resident((block_words, LANES), lambda b, i: (b, 0)),
        ],
        out_specs=pl.BlockSpec((1, tm, SLAB_ROWS, LANES), lambda b, i: (b, i, 0, 0)),
        out_shape=jax.ShapeDtypeStruct((N_TABLE_BLOCKS, T, SLAB_ROWS, LANES), F32),
        scratch_shapes=[pltpu.VMEM((tm, PAIRS * SLAB_ROWS), F32)],
        compiler_params=_cparams(("arbitrary", "arbitrary")),
        name="peer_expert_v",
    )(idxl, ovf, gate, act, expand, v_tab)


def _resid_kernel(final, x_ref, o_ref, g2_ref, fg_ref, y_ref):
    cols = []
    for s in range(SLAB_ROWS):
        part = o_ref[0, :, s, :]
        for k in range(1, N_TABLE_BLOCKS):
            part = part + o_ref[k, :, s, :]
        cols.append(part)
    x = x_ref[...] + g2_ref[0] * jnp.concatenate(cols, axis=1)
    if final:
        ms = jnp.mean(x * x, axis=-1, keepdims=True)
        x = (x * lax.rsqrt(ms + EPS)) * fg_ref[...]
    y_ref[...] = x


def _resid(x, peer_out, g2, final_g, seq, final):
    T, D = x.shape
    tm = 512
    per_seq = seq // tm
    return pl.pallas_call(
        functools.partial(_resid_kernel, final),
        grid=(T // tm,),
        in_specs=[
            pl.BlockSpec((tm, D), lambda i: (i, 0)),
            pl.BlockSpec((N_TABLE_BLOCKS, tm, SLAB_ROWS, LANES), lambda i: (0, i, 0, 0)),
            pl.BlockSpec((1, 1, D), lambda i: (i // per_seq, 0, 0)),
            pl.BlockSpec((1, D), lambda i: (0, 0)),
        ],
        out_specs=pl.BlockSpec((tm, D), lambda i: (i, 0)),
        out_shape=jax.ShapeDtypeStruct((T, D), F32),
        compiler_params=_cparams(("parallel",)),
        name="peer_residual",
    )(x, peer_out, g2, final_g)


def _pair_expand_matrix():
    p = np.arange(PAIRS)[:, None]
    c = np.arange(PAIRS * SLAB_ROWS)[None, :]
    return (p == c // SLAB_ROWS).astype(np.float32)


def kernel(x, c, rel_bias, ada_w, ada_b, norm1_g, w_in, dw_w, dw_b, conv_ln_g, conv_ln_b, w_conv_out,
           attn_sinks, w_attn_out, w_out, norm2_g, w_pq, sub_keys, peer_u, peer_v, final_g):
    B, S, D = x.shape
    L = ada_w.shape[0]
    T = B * S
    C = dw_w.shape[2]
    A = N_HEADS * HEAD_DIM
    KV = N_KV_HEADS * HEAD_DIM
    E = peer_u.shape[1]
    assert S % 1024 == 0 and D == SLAB_ROWS * LANES and E % N_TABLE_BLOCKS == 0

    o_a, o_b, o_q, o_k, o_v, o_g = 0, C, 2 * C, 2 * C + A, 2 * C + A + KV, 2 * C + A + 2 * KV
    order = np.concatenate([np.arange(o_a, o_a + 2 * C), np.arange(o_g, o_g + 2 * D),
                            np.arange(o_q, o_q + A + 2 * KV)])
    n_a, n_b, n_gc, n_ga, n_q, n_k, n_v = 0, C, 2 * C, 2 * C + D, 2 * C + 2 * D, 2 * C + 2 * D + A, 2 * C + 2 * D + A + KV

    c_pad = jnp.pad(c, ((0, 8 - B), (0, 0)))
    mod = _ada(c_pad, ada_w, ada_b)
    bias_tab = _rel_bias_table(rel_bias)
    expand = jnp.asarray(_pair_expand_matrix(), dtype=BF16)
    row_sum = jnp.asarray(np.arange(MXU_DEPTH)[None, :] // SLAB_WORDS == np.arange(MXU_DEPTH // SLAB_WORDS)[:, None],
                          dtype=BF16)
    final_g2 = final_g.reshape(1, D)

    xt = x.reshape(T, D)
    for l in range(L):
        m = mod[l, :B].reshape(B, 6, 1, D)
        sh1, sc1, g1, sh2, sc2, g2 = (m[:, k] for k in range(6))

        w_in_l = w_in[l][:, order].astype(BF16)
        proj = _inproj(xt, norm1_g[l].reshape(1, D), sc1, sh1, w_in_l, S)
        dw_rep = jnp.broadcast_to(dw_w[l][:, None, :], (CONV_WIDTH, 8, C))
        cact = _conv(proj, dw_rep, dw_b[l].reshape(1, C), conv_ln_g[l].reshape(1, C),
                     conv_ln_b[l].reshape(1, C), S, n_a // C, n_b // C)
        attn = _attn(proj, bias_tab, attn_sinks[l], S, n_q // A, n_k // KV, n_v // KV)
        xt = _mix(cact, attn, proj, xt, g1, w_conv_out[l].astype(BF16), w_attn_out[l].astype(BF16),
                  w_out[l].astype(BF16), S, n_gc // D, n_ga // D)

        keys = sub_keys[l].reshape(2 * PEER_HEADS, N_KEYS, PEER_HALF).astype(BF16)
        h2, st = _peer_q(xt, norm2_g[l].reshape(1, D), sc2, sh2, w_pq[l].astype(BF16), keys, S)
        idxl, gate, ovf = _route(st, E // N_TABLE_BLOCKS)
        u_tab = _pack_table(peer_u, l)
        v_tab = _pack_table(peer_v, l)
        act = _peer_u(idxl, ovf, h2, row_sum, u_tab)
        pout = _peer_v(idxl, ovf, gate, act, expand, v_tab)
        xt = _resid(xt, pout, g2, final_g2, S, l == L - 1)
    return xt.reshape(B, S, D)
```

```python
import functools
import math

import numpy as np
import jax
import jax.numpy as jnp
from jax import lax
from jax.experimental import pallas as pl
from jax.experimental.pallas import tpu as pltpu

F32 = jnp.float32
BF16 = jnp.bfloat16
I32 = jnp.int32

EPS = 1e-6
CONV_WIDTH = 31
HALO = 32
N_HEADS = 16
N_KV_HEADS = 4
HEAD_DIM = 64
BLOCK = 128
NUM_BUCKETS = 32
MAX_DISTANCE = 128
N_KEYS = 128
PEER_HEADS = 8
PEER_TOPK = 16
PEER_HALF = 128
PAIRS = PEER_HEADS * PEER_TOPK
SLAB_ROWS = 16
LANES = 128
N_TABLE_BLOCKS = 2
TOKEN_UNROLL = 128
EXPERT_TILE = 128
HEAD_UNROLL = 4
MXU_DEPTH = 256
SLAB_WORDS = 8
WINDOW_SLOTS = 80
assert PAIRS & (PAIRS - 1) == 0 and WINDOW_SLOTS % 8 == 0 and 2 * WINDOW_SLOTS >= PAIRS
VMEM_LIMIT = 56 * 1024 * 1024


def _cparams(sem, vmem=VMEM_LIMIT):
    return pltpu.CompilerParams(dimension_semantics=sem, vmem_limit_bytes=vmem)


def _resident(block_shape, index_map):
    return pl.BlockSpec(block_shape, index_map, pipeline_mode=pl.Buffered(1))


def _ada_kernel(c_ref, w_ref, b_ref, o_ref):
    c = c_ref[...]
    cs = c * jax.nn.sigmoid(c)
    o_ref[0] = jnp.dot(cs.astype(BF16), w_ref[0].astype(BF16), preferred_element_type=F32) + b_ref[0]


def _ada(c_pad, ada_w, ada_b):
    L, D, N = ada_w.shape
    tn = 1024
    return pl.pallas_call(
        _ada_kernel,
        grid=(L, N // tn),
        in_specs=[
            pl.BlockSpec((8, D), lambda l, j: (0, 0)),
            pl.BlockSpec((1, D, tn), lambda l, j: (l, 0, j)),
            pl.BlockSpec((1, 1, tn), lambda l, j: (l, 0, j)),
        ],
        out_specs=pl.BlockSpec((1, 8, tn), lambda l, j: (l, 0, j)),
        out_shape=jax.ShapeDtypeStruct((L, 8, N), F32),
        compiler_params=_cparams(("parallel", "parallel")),
        name="ada_mod",
    )(c_pad, ada_w, ada_b.reshape(L, 1, N))


def _bias_kernel(bucket_ref, rb_ref, o_ref):
    h = pl.program_id(0)
    bucket = bucket_ref[...]
    acc = jnp.zeros(bucket.shape, F32)
    for b in range(NUM_BUCKETS):
        acc = jnp.where(bucket == b, rb_ref[b, h], acc)
    o_ref[0] = acc


def _t5_bucket_table():
    qi = np.arange(BLOCK)[:, None] + BLOCK
    kj = np.arange(2 * BLOCK)[None, :]
    d = np.maximum(qi - kj, 0)
    max_exact = NUM_BUCKETS // 2
    ratio = (np.log(np.maximum(d, 1).astype(np.float64) / max_exact)
             / math.log(MAX_DISTANCE / max_exact) * (NUM_BUCKETS - max_exact))
    large = np.minimum(max_exact + ratio.astype(np.int64), NUM_BUCKETS - 1)
    return np.where(d < max_exact, d, large).astype(np.int32)


def _rel_bias_table(rel_bias):
    bucket = jnp.asarray(_t5_bucket_table())
    return pl.pallas_call(
        _bias_kernel,
        grid=(N_HEADS,),
        in_specs=[
            pl.BlockSpec((BLOCK, 2 * BLOCK), lambda h: (0, 0)),
            pl.BlockSpec(memory_space=pltpu.SMEM),
        ],
        out_specs=pl.BlockSpec((1, BLOCK, 2 * BLOCK), lambda h: (h, 0, 0)),
        out_shape=jax.ShapeDtypeStruct((N_HEADS, BLOCK, 2 * BLOCK), F32),
        compiler_params=_cparams(("arbitrary",)),
        name="rel_bias_table",
    )(bucket, rel_bias)


def _norm_mod(x, g, sc, sh):
    ms = jnp.mean(x * x, axis=-1, keepdims=True)
    y = x * lax.rsqrt(ms + EPS)
    return (y * g) * (1.0 + sc) + sh


def _inproj_kernel(x_ref, g_ref, sc_ref, sh_ref, w_ref, o_ref, h_scr):
    @pl.when(pl.program_id(1) == 0)
    def _():
        h_scr[...] = _norm_mod(x_ref[...], g_ref[...], sc_ref[0], sh_ref[0]).astype(BF16)

    o_ref[...] = jnp.dot(h_scr[...], w_ref[...], preferred_element_type=F32)


def _inproj(x, g, sc, sh, w, seq):
    T, D = x.shape
    N = w.shape[1]
    tm, tn = 1024, 1536
    per_seq = seq // tm
    return pl.pallas_call(
        _inproj_kernel,
        grid=(T // tm, N // tn),
        in_specs=[
            pl.BlockSpec((tm, D), lambda i, j: (i, 0)),
            pl.BlockSpec((1, D), lambda i, j: (0, 0)),
            pl.BlockSpec((1, 1, D), lambda i, j: (i // per_seq, 0, 0)),
            pl.BlockSpec((1, 1, D), lambda i, j: (i // per_seq, 0, 0)),
            pl.BlockSpec((D, tn), lambda i, j: (0, j)),
        ],
        out_specs=pl.BlockSpec((tm, tn), lambda i, j: (i, j)),
        out_shape=jax.ShapeDtypeStruct((T, N), F32),
        scratch_shapes=[pltpu.VMEM((tm, D), BF16)],
        compiler_params=_cparams(("parallel", "arbitrary")),
        name="in_proj",
    )(x, g, sc, sh, w)


def _conv_kernel(per_seq, a_ref, b_ref, ha_ref, hb_ref, dw_ref, db_ref, lg_ref, lb_ref, o_ref, ext, acc_scr):
    ts = a_ref.shape[0]
    first = (pl.program_id(0) % per_seq) == 0
    hu = ha_ref[...] * jax.nn.sigmoid(hb_ref[...])
    ext[0:HALO, :] = jnp.where(first, 0.0, hu)
    ext[HALO:, :] = a_ref[...] * jax.nn.sigmoid(b_ref[...])
    base = HALO - (CONV_WIDTH - 1)
    n_ch = a_ref.shape[1]
    for c0 in range(0, n_ch, LANES):
        acc = jnp.zeros((ts // 8, 8, LANES), F32)
        for r in range(8):
            offs = [o for o in range(base, base + CONV_WIDTH) if o % 8 == r]
            if not offs:
                continue
            shifted = ext[r:r + 8 * (max(offs) // 8) + ts, c0:c0 + LANES]
            for o in offs:
                tap = dw_ref[o - base, :, c0:c0 + LANES]
                acc = acc + shifted[o - r:o - r + ts, :].reshape(ts // 8, 8, LANES) * tap[None]
        acc_scr[:, c0:c0 + LANES] = acc.reshape(ts, LANES)
    acc = acc_scr[...] + db_ref[...]
    mu = jnp.mean(acc, axis=-1, keepdims=True)
    cen = acc - mu
    var = jnp.mean(cen * cen, axis=-1, keepdims=True)
    y = cen * lax.rsqrt(var + EPS) * lg_ref[...] + lb_ref[...]
    o_ref[...] = (y * jax.nn.sigmoid(y)).astype(BF16)


def _conv(proj, dw_w, dw_b, ln_g, ln_b, seq, a_col, b_col):
    T = proj.shape[0]
    C = dw_w.shape[-1]
    ts = 256
    per_seq = seq // ts
    rb = ts // HALO
    halo_map = lambda col: (lambda i: (jnp.maximum(i * rb - 1, 0), col))
    return pl.pallas_call(
        functools.partial(_conv_kernel, per_seq),
        grid=(T // ts,),
        in_specs=[
            pl.BlockSpec((ts, C), lambda i: (i, a_col)),
            pl.BlockSpec((ts, C), lambda i: (i, b_col)),
            pl.BlockSpec((HALO, C), halo_map(a_col)),
            pl.BlockSpec((HALO, C), halo_map(b_col)),
            pl.BlockSpec((CONV_WIDTH, 8, C), lambda i: (0, 0, 0)),
            pl.BlockSpec((1, C), lambda i: (0, 0)),
            pl.BlockSpec((1, C), lambda i: (0, 0)),
            pl.BlockSpec((1, C), lambda i: (0, 0)),
        ],
        out_specs=pl.BlockSpec((ts, C), lambda i: (i, 0)),
        out_shape=jax.ShapeDtypeStruct((T, C), BF16),
        scratch_shapes=[pltpu.VMEM((HALO + ts, C), F32), pltpu.VMEM((ts, C), F32)],
        compiler_params=_cparams(("parallel",)),
        name="conv_branch",
    )(proj, proj, proj, proj, dw_w, dw_b, ln_g, ln_b)


def _attn_kernel(nb, q_ref, kc_ref, kp_ref, vc_ref, vp_ref, bias_ref, sink_ref, o_ref):
    first = (pl.program_id(0) % nb) == 0
    kk = jnp.concatenate([kp_ref[...], kc_ref[...]], axis=0).astype(BF16)
    vv = jnp.concatenate([vp_ref[...], vc_ref[...]], axis=0).astype(BF16)
    row = lax.broadcasted_iota(I32, (BLOCK, 2 * BLOCK), 0)
    col = lax.broadcasted_iota(I32, (BLOCK, 2 * BLOCK), 1)
    dist = row + BLOCK - col
    valid = (dist >= 0) & (dist < BLOCK) & (jnp.logical_not(first) | (col >= BLOCK))
    group = N_HEADS // N_KV_HEADS
    for kvh in range(N_KV_HEADS):
        k_h = kk[:, kvh * HEAD_DIM:(kvh + 1) * HEAD_DIM]
        v_h = vv[:, kvh * HEAD_DIM:(kvh + 1) * HEAD_DIM]
        qg = q_ref[:, kvh * group * HEAD_DIM:(kvh + 1) * group * HEAD_DIM].astype(BF16)
        outs = []
        for g in range(group):
            h = kvh * group + g
            q_h = qg[:, g * HEAD_DIM:(g + 1) * HEAD_DIM]
            s = lax.dot_general(q_h, k_h, (((1,), (1,)), ((), ())), preferred_element_type=F32)
            s = s * (HEAD_DIM ** -0.5) + bias_ref[h]
            s = jnp.where(valid, s, -jnp.inf)
            sink = sink_ref[h]
            m = jnp.maximum(jnp.max(s, axis=-1, keepdims=True), sink)
            e = jnp.exp(s - m)
            denom = jnp.sum(e, axis=-1, keepdims=True) + jnp.exp(sink - m)
            p = e / denom
            outs.append(jnp.dot(p.astype(BF16), v_h, preferred_element_type=F32))
        o_ref[:, kvh * group * HEAD_DIM:(kvh + 1) * group * HEAD_DIM] = (
            jnp.concatenate(outs, axis=1).astype(BF16))


def _attn(proj, bias_tab, sinks, seq, q_col, k_col, v_col):
    T = proj.shape[0]
    nb = seq // BLOCK
    attn_dim = N_HEADS * HEAD_DIM
    kv_dim = N_KV_HEADS * HEAD_DIM
    prev = lambda col: (lambda i: (jnp.where(i % nb == 0, i, i - 1), col))
    return pl.pallas_call(
        functools.partial(_attn_kernel, nb),
        grid=(T // BLOCK,),
        in_specs=[
            pl.BlockSpec((BLOCK, attn_dim), lambda i: (i, q_col)),
            pl.BlockSpec((BLOCK, kv_dim), lambda i: (i, k_col)),
            pl.BlockSpec((BLOCK, kv_dim), prev(k_col)),
            pl.BlockSpec((BLOCK, kv_dim), lambda i: (i, v_col)),
            pl.BlockSpec((BLOCK, kv_dim), prev(v_col)),
            pl.BlockSpec((N_HEADS, BLOCK, 2 * BLOCK), lambda i: (0, 0, 0)),
            pl.BlockSpec(memory_space=pltpu.SMEM),
        ],
        out_specs=pl.BlockSpec((BLOCK, attn_dim), lambda i: (i, 0)),
        out_shape=jax.ShapeDtypeStruct((T, attn_dim), BF16),
        compiler_params=_cparams(("parallel",)),
        name="swa_attention",
    )(proj, proj, proj, proj, proj, bias_tab, sinks)


def _mix_kernel(ca_ref, at_ref, gc_ref, ga_ref, x_ref, g1_ref, wc_ref, wa_ref, wo_ref, o_ref):
    yc = jnp.dot(ca_ref[...], wc_ref[...], preferred_element_type=F32)
    ya = jnp.dot(at_ref[...], wa_ref[...], preferred_element_type=F32)
    mixed = jax.nn.sigmoid(gc_ref[...]) * yc + jax.nn.sigmoid(ga_ref[...]) * ya
    y = jnp.dot(mixed.astype(BF16), wo_ref[...], preferred_element_type=F32)
    o_ref[...] = x_ref[...] + g1_ref[0] * y


def _mix(cact, attn, proj, x, g1, wc, wa, wo, seq, gc_col, ga_col):
    T, D = x.shape
    C = cact.shape[1]
    A = attn.shape[1]
    tm = 256
    per_seq = seq // tm
    return pl.pallas_call(
        _mix_kernel,
        grid=(T // tm,),
        in_specs=[
            pl.BlockSpec((tm, C), lambda i: (i, 0)),
            pl.BlockSpec((tm, A), lambda i: (i, 0)),
            pl.BlockSpec((tm, D), lambda i: (i, gc_col)),
            pl.BlockSpec((tm, D), lambda i: (i, ga_col)),
            pl.BlockSpec((tm, D), lambda i: (i, 0)),
            pl.BlockSpec((1, 1, D), lambda i: (i // per_seq, 0, 0)),
            _resident((C, D), lambda i: (0, 0)),
            _resident((A, D), lambda i: (0, 0)),
            _resident((D, D), lambda i: (0, 0)),
        ],
        out_specs=pl.BlockSpec((tm, D), lambda i: (i, 0)),
        out_shape=jax.ShapeDtypeStruct((T, D), F32),
        compiler_params=_cparams(("parallel",)),
        name="merge_out_proj",
    )(cact, attn, proj, proj, x, g1, wc, wa, wo)


def _store_packed_slabs(o_ref, xb):
    w = lax.bitcast_convert_type(xb.astype(F32), jnp.uint32)
    for i in range(SLAB_WORDS):
        lo = w[:, (2 * i) * LANES:(2 * i + 1) * LANES]
        hi = w[:, (2 * i + 1) * LANES:(2 * i + 2) * LANES]
        o_ref[:, i, :] = hi | (lo >> 16)


def _pack_kernel(x_ref, o_ref):
    _store_packed_slabs(o_ref, x_ref[0].astype(BF16))


def _pack_table(tabs, layer):
    _, n, d = tabs.shape
    te = 256
    packed = pl.pallas_call(
        _pack_kernel,
        grid=(n // te,),
        in_specs=[pl.BlockSpec((1, te, d), lambda i: (layer, i, 0))],
        out_specs=pl.BlockSpec((te, SLAB_WORDS, LANES), lambda i: (i, 0, 0)),
        out_shape=jax.ShapeDtypeStruct((n, SLAB_WORDS, LANES), jnp.uint32),
        compiler_params=_cparams(("parallel",)),
        name="pack_expert_table",
    )(tabs)
    return packed.reshape(n * SLAB_WORDS, LANES)


def _peer_q_kernel(x_ref, g_ref, sc_ref, sh_ref, w_ref, k_ref, h_ref, st_ref):
    hb = _norm_mod(x_ref[...], g_ref[...], sc_ref[0], sh_ref[0]).astype(BF16)
    _store_packed_slabs(h_ref, hb)
    qb = jnp.dot(hb, w_ref[...], preferred_element_type=F32).astype(BF16)
    for hp in range(2 * PEER_HEADS):
        q_hp = qb[:, hp * PEER_HALF:(hp + 1) * PEER_HALF]
        st_ref[hp * N_KEYS:(hp + 1) * N_KEYS, :] = lax.dot_general(
            k_ref[hp], q_hp, (((1,), (1,)), ((), ())), preferred_element_type=F32)


def _peer_q(x, g, sc, sh, w_pq, keys, seq):
    T, D = x.shape
    Q = w_pq.shape[1]
    tm = 256
    per_seq = seq // tm
    n_rows = 2 * PEER_HEADS * N_KEYS
    return pl.pallas_call(
        _peer_q_kernel,
        grid=(T // tm,),
        in_specs=[
            pl.BlockSpec((tm, D), lambda i: (i, 0)),
            pl.BlockSpec((1, D), lambda i: (0, 0)),
            pl.BlockSpec((1, 1, D), lambda i: (i // per_seq, 0, 0)),
            pl.BlockSpec((1, 1, D), lambda i: (i // per_seq, 0, 0)),
            _resident((D, Q), lambda i: (0, 0)),
            _resident((2 * PEER_HEADS, N_KEYS, PEER_HALF), lambda i: (0, 0, 0)),
        ],
        out_specs=[
            pl.BlockSpec((tm, SLAB_WORDS, LANES), lambda i: (i, 0, 0)),
            pl.BlockSpec((n_rows, tm), lambda i: (0, i)),
        ],
        out_shape=[
            jax.ShapeDtypeStruct((T, SLAB_WORDS, LANES), jnp.uint32),
            jax.ShapeDtypeStruct((n_rows, T), F32),
        ],
        compiler_params=_cparams(("parallel",)),
        name="peer_query_scores",
    )(x, g, sc, sh, w_pq, keys)


def _top_rows(x, payload, k, rank=None):
    rows = (lax.broadcasted_iota(I32, x.shape, 0) if rank is None else rank).astype(F32)
    vals, pays = [], []
    for _ in range(k):
        m = jnp.max(x, axis=0, keepdims=True)
        first = jnp.min(jnp.where(x == m, rows, jnp.float32(1 << 24)), axis=0, keepdims=True)
        onehot = rows == first
        if payload is None:
            pays.append(first.astype(I32))
        else:
            pays.append(jnp.sum(jnp.where(onehot, payload, 0), axis=0, keepdims=True))
        vals.append(m)
        x = jnp.where(onehot, -jnp.inf, x)
    return jnp.concatenate(vals, axis=0), jnp.concatenate(pays, axis=0)


def _sort_rows(keys, pays):
    n_groups = len(keys)
    n_rows = 8 * n_groups
    sub = lax.broadcasted_iota(I32, keys[0].shape, 0)
    k = 2
    while k <= n_rows:
        j = k // 2
        while j >= 1:
            if j >= 8:
                gj = j // 8
                for lo in range(n_groups):
                    if lo & gj:
                        continue
                    hi = lo | gj
                    ascending = ((8 * lo) & k) == 0
                    a, b = keys[lo], keys[hi]
                    swap = (a > b) if ascending else (a < b)
                    keys[lo], keys[hi] = jnp.where(swap, b, a), jnp.where(swap, a, b)
                    pa, pb = pays[lo], pays[hi]
                    pays[lo], pays[hi] = jnp.where(swap, pb, pa), jnp.where(swap, pa, pb)
            else:
                lower = (sub & j) == 0
                upper = (sub & j) != 0
                for g in range(n_groups):
                    if k >= 8:
                        ascending = ((8 * g) & k) == 0
                        want_max = upper if ascending else lower
                    else:
                        want_max = jnp.logical_xor(upper, (sub & k) != 0)
                    x, p = keys[g], pays[g]
                    xp = jnp.where(lower, pltpu.roll(x, 8 - j, 0), pltpu.roll(x, j, 0))
                    pp = jnp.where(lower, pltpu.roll(p, 8 - j, 0), pltpu.roll(p, j, 0))
                    swap = jnp.logical_xor(xp < x, want_max)
                    keys[g] = jnp.where(swap, xp, x)
                    pays[g] = jnp.where(swap, pp, p)
            j //= 2
        k *= 2
    return keys, pays


def _route_kernel(block_rows, st_ref, idxl_ref, gate_ref, ovf_ref, idx_scr, gate_scr):
    def head(h, carry):
        base = pl.multiple_of(h * (2 * N_KEYS), 2 * N_KEYS)
        v0, i0 = _top_rows(st_ref[pl.ds(base, N_KEYS), :], None, PEER_TOPK)
        v1, i1 = _top_rows(st_ref[pl.ds(base + N_KEYS, N_KEYS), :], None, PEER_TOPK)
        half = PEER_TOPK // 2
        sub = lax.broadcasted_iota(I32, (half, v0.shape[1]), 0)
        cand = [v0[k:k + 1] + v1[:half] for k in range(half)]
        cidx = [i0[k:k + 1] * N_KEYS + i1[:half] for k in range(half)]
        flat = [sub + k * PEER_TOPK for k in range(half)]
        cand += [v0[:1] + v1[half:], v0[half:] + v1[:1]]
        cidx += [i0[:1] * N_KEYS + i1[half:], i0[half:] * N_KEYS + i1[:1]]
        flat += [sub + half, (sub + half) * PEER_TOPK]
        best, eid = _top_rows(jnp.concatenate(cand, axis=0), jnp.concatenate(cidx, axis=0), PEER_TOPK,
                              rank=jnp.concatenate(flat, axis=0))
        e = jnp.exp(best - jnp.max(best, axis=0, keepdims=True))
        gates = e / jnp.sum(e, axis=0, keepdims=True)
        off = pl.multiple_of(h * PEER_TOPK, PEER_TOPK)
        idx_scr[pl.ds(off, PEER_TOPK), :] = eid
        gate_scr[pl.ds(off, PEER_TOPK), :] = gates
        return carry

    lax.fori_loop(0, PEER_HEADS, head, 0, unroll=HEAD_UNROLL)

    n_groups = PAIRS // 8
    sub = lax.broadcasted_iota(I32, (8, idx_scr.shape[1]), 0)
    keys = [idx_scr[8 * g:8 * g + 8, :] * PAIRS + (sub + 8 * g) for g in range(n_groups)]
    pays = [gate_scr[8 * g:8 * g + 8, :] for g in range(n_groups)]
    keys, pays = _sort_rows(keys, pays)
    eids = [jnp.right_shift(k, PAIRS.bit_length() - 1) for k in keys]
    row_hi, row_lo = WINDOW_SLOTS, PAIRS - WINDOW_SLOTS - 1
    ovf0 = jnp.where(eids[row_hi // 8][row_hi % 8:row_hi % 8 + 1, :] < block_rows, 1, 0)
    ovf1 = jnp.where(eids[row_lo // 8][row_lo % 8:row_lo % 8 + 1, :] >= block_rows, 1, 0)
    flags = jnp.concatenate([ovf0, ovf1], axis=0).astype(I32)
    tile_any = jnp.broadcast_to(jnp.max(flags, axis=1, keepdims=True), flags.shape)
    ovf_ref[...] = jnp.concatenate([flags, tile_any], axis=0)
    rot = (PAIRS - WINDOW_SLOTS) // 8
    for b in range(N_TABLE_BLOCKS):
        order = [(g + b * rot) % n_groups for g in range(n_groups)]
        e = jnp.concatenate([eids[g] for g in order], axis=0)
        inb = (e >= b * block_rows) & (e < (b + 1) * block_rows)
        local = jnp.where(inb, (e - b * block_rows) * SLAB_WORDS, 0)
        gate = jnp.where(inb, jnp.concatenate([pays[g] for g in order], axis=0), 0.0)
        idxl_ref[b] = local.astype(F32).T.astype(I32)
        gate_ref[b] = gate.T


def _route(st, block_rows):
    n_rows, T = st.shape
    tr = EXPERT_TILE
    return pl.pallas_call(
        functools.partial(_route_kernel, block_rows),
        grid=(T // tr,),
        in_specs=[pl.BlockSpec((n_rows, tr), lambda i: (0, i))],
        out_specs=[
            pl.BlockSpec((N_TABLE_BLOCKS, tr, PAIRS), lambda i: (0, i, 0)),
            pl.BlockSpec((N_TABLE_BLOCKS, tr, PAIRS), lambda i: (0, i, 0)),
            pl.BlockSpec((2 * N_TABLE_BLOCKS, tr), lambda i: (0, i)),
        ],
        out_shape=[
            jax.ShapeDtypeStruct((N_TABLE_BLOCKS, T, PAIRS), I32),
            jax.ShapeDtypeStruct((N_TABLE_BLOCKS, T, PAIRS), F32),
            jax.ShapeDtypeStruct((2 * N_TABLE_BLOCKS, T), I32),
        ],
        scratch_shapes=[pltpu.VMEM((PAIRS, tr), I32), pltpu.VMEM((PAIRS, tr), F32)],
        compiler_params=_cparams(("parallel",)),
        name="peer_route",
    )(st)


def _diag_mask():
    r = lax.broadcasted_iota(I32, (SLAB_ROWS, PAIRS * SLAB_ROWS), 0)
    c = lax.broadcasted_iota(I32, (SLAB_ROWS, PAIRS * SLAB_ROWS), 1)
    return r == (c % SLAB_ROWS)


def _for_overflow_tokens(ovf_ref, b, tm, body):
    group = 8
    assert tm == EXPERT_TILE

    def step(g, carry):
        flags = [ovf_ref[b, g * group + i] for i in range(group)]
        any_set = functools.reduce(lambda x, y: x | y, flags)

        @pl.when(any_set != 0)
        def _():
            def one(i, c):
                t = g * group + i

                @pl.when(ovf_ref[b, t] != 0)
                def _():
                    body(t)
                return c

            lax.fori_loop(0, group, one, 0)
        return carry

    @pl.when(ovf_ref[N_TABLE_BLOCKS + b, 0] != 0)
    def _():
        lax.fori_loop(0, tm // group, step, 0)


def _gather_slabs(tab_ref, idx_ref, t, slots):
    words = [tab_ref[pl.ds(pl.multiple_of(idx_ref[0, t, j], SLAB_WORDS), SLAB_WORDS), :] for j in slots]
    return pltpu.bitcast(jnp.concatenate(words, axis=0), BF16)


def _peer_u_kernel(idx_ref, ovf_ref, h_ref, sel_ref, u_ref, o_ref, part_scr):
    tm = h_ref.shape[0]
    b = pl.program_id(0)
    sel = sel_ref[...]
    per_tile = sel.shape[0]
    hi_mask = jnp.uint32(0xFFFF0000)

    def halves(words):
        return (lax.bitcast_convert_type(words << 16, F32), lax.bitcast_convert_type(words & hi_mask, F32))

    def lane_partials(t, slots):
        h_even, h_odd = halves(h_ref[t])
        sums = []
        for j in slots:
            u_even, u_odd = halves(u_ref[pl.ds(pl.multiple_of(idx_ref[0, t, j], SLAB_WORDS), SLAB_WORDS), :])
            sums.append(u_even * h_even + u_odd * h_odd)
        for k in range(0, len(sums), per_tile):
            grp = sums[k:k + per_tile]
            part_scr[t, slots[k]:slots[k] + len(grp), :] = jnp.dot(
                sel[:len(grp), :SLAB_WORDS * len(grp)], jnp.concatenate(grp, axis=0).astype(BF16),
                preferred_element_type=F32)

    def gather_trip(c):
        for i in range(TOKEN_UNROLL):
            lane_partials(c * TOKEN_UNROLL + i, range(WINDOW_SLOTS))

    def lane_sum_trip(c):
        for k in range(0, TOKEN_UNROLL, 8):
            r = pl.multiple_of(c * TOKEN_UNROLL + k, 8)
            o_ref[0, pl.ds(r, 8), 0:WINDOW_SLOTS] = jnp.sum(part_scr[pl.ds(r, 8), 0:WINDOW_SLOTS, :], axis=-1)

    def rare(t):
        lane_partials(t, range(WINDOW_SLOTS, PAIRS))
        o_ref[0, pl.ds(t, 1), WINDOW_SLOTS:] = jnp.sum(part_scr[pl.ds(t, 1), WINDOW_SLOTS:, :], axis=-1)

    def trip(c, carry):
        gather_trip(c)
        lane_sum_trip(c - 1)
        return carry

    n_trips = tm // TOKEN_UNROLL
    o_ref[0, :, WINDOW_SLOTS:] = jnp.zeros((tm, PAIRS - WINDOW_SLOTS), F32)
    gather_trip(0)
    lax.fori_loop(1, n_trips, trip, 0)
    lane_sum_trip(n_trips - 1)
    _for_overflow_tokens(ovf_ref, b, tm, rare)


def _peer_u(idxl, ovf, h_slab, row_sum, u_tab):
    T = h_slab.shape[0]
    block_words = u_tab.shape[0] // N_TABLE_BLOCKS
    tm = EXPERT_TILE
    return pl.pallas_call(
        _peer_u_kernel,
        grid=(N_TABLE_BLOCKS, T // tm),
        in_specs=[
            pl.BlockSpec((1, tm, PAIRS), lambda b, i: (b, i, 0), memory_space=pltpu.SMEM),
            pl.BlockSpec((2 * N_TABLE_BLOCKS, tm), lambda b, i: (0, i), memory_space=pltpu.SMEM),
            pl.BlockSpec((tm, SLAB_WORDS, LANES), lambda b, i: (i, 0, 0)),
            _resident(row_sum.shape, lambda b, i: (0, 0)),
            _resident((block_words, LANES), lambda b, i: (b, 0)),
        ],
        out_specs=pl.BlockSpec((1, tm, PAIRS), lambda b, i: (b, i, 0)),
        out_shape=jax.ShapeDtypeStruct((N_TABLE_BLOCKS, T, PAIRS), F32),
        scratch_shapes=[pltpu.VMEM((tm, PAIRS, LANES), F32)],
        compiler_params=_cparams(("arbitrary", "arbitrary")),
        name="peer_expert_u",
    )(idxl, ovf, h_slab, row_sum, u_tab)


def _peer_v_kernel(idx_ref, ovf_ref, gate_ref, act_ref, ex_ref, v_ref, o_ref, wexp):
    tm = gate_ref.shape[1]
    b = pl.program_id(0)
    act = act_ref[0]
    gelu = 0.5 * act * (1.0 + lax.erf(act * np.float32(math.sqrt(0.5))))
    w = gate_ref[0] * gelu
    wexp[...] = jnp.dot(w.astype(BF16), ex_ref[...], preferred_element_type=F32)
    eye = _diag_mask()
    wcols = WINDOW_SLOTS * SLAB_ROWS

    def weighted(t, slots, lo, hi):
        vsel = _gather_slabs(v_ref, idx_ref, t, slots)
        wrow = jnp.broadcast_to(wexp[pl.ds(t, 1), lo:hi], (SLAB_ROWS, hi - lo))
        wbd = jnp.where(eye[:, lo:hi], wrow, 0.0).astype(BF16)
        return jnp.dot(wbd, vsel, preferred_element_type=F32)

    def tok(t, carry):
        o_ref[0, t] = weighted(t, range(WINDOW_SLOTS), 0, wcols)
        return carry

    def rare(t):
        o_ref[0, t] = o_ref[0, t] + weighted(t, range(WINDOW_SLOTS, PAIRS), wcols, PAIRS * SLAB_ROWS)

    lax.fori_loop(0, tm, tok, 0, unroll=TOKEN_UNROLL)
    _for_overflow_tokens(ovf_ref, b, tm, rare)


def _peer_v(idxl, ovf, gate, act, expand, v_tab):
    T = gate.shape[1]
    block_words = v_tab.shape[0] // N_TABLE_BLOCKS
    tm = EXPERT_TILE
    return pl.pallas_call(
        _peer_v_kernel,
        grid=(N_TABLE_BLOCKS, T // tm),
        in_specs=[
            pl.BlockSpec((1, tm, PAIRS), lambda b, i: (b, i, 0), memory_space=pltpu.SMEM),
            pl.BlockSpec((2 * N_TABLE_BLOCKS, tm), lambda b, i: (0, i), memory_space=pltpu.SMEM),
            pl.BlockSpec((1, tm, PAIRS), lambda b, i: (b, i, 0)),
            pl.BlockSpec((1, tm, PAIRS), lambda b, i: (b, i, 0)),
            _resident((PAIRS, PAIRS * SLAB_ROWS), lambda b, i: (0, 0)),
            _resident((block_words, LANES), lambda b, i: (b, 0)),
        ],
        out_specs=pl.BlockSpec((1, tm, SLAB_ROWS, LANES), lambda b, i: (b, i, 0, 0)),
        out_shape=jax.ShapeDtypeStruct((N_TABLE_BLOCKS, T, SLAB_ROWS, LANES), F32),
        scratch_shapes=[pltpu.VMEM((tm, PAIRS * SLAB_ROWS), F32)],
        compiler_params=_cparams(("arbitrary", "arbitrary")),
        name="peer_expert_v",
    )(idxl, ovf, gate, act, expand, v_tab)


def _resid_kernel(final, x_ref, o_ref, g2_ref, fg_ref, y_ref):
    cols = []
    for s in range(SLAB_ROWS):
        part = o_ref[0, :, s, :]
        for k in range(1, N_TABLE_BLOCKS):
            part = part + o_ref[k, :, s, :]
        cols.append(part)
    x = x_ref[...] + g2_ref[0] * jnp.concatenate(cols, axis=1)
    if final:
        ms = jnp.mean(x * x, axis=-1, keepdims=True)
        x = (x * lax.rsqrt(ms + EPS)) * fg_ref[...]
    y_ref[...] = x


def _resid(x, peer_out, g2, final_g, seq, final):
    T, D = x.shape
    tm = 512
    per_seq = seq // tm
    return pl.pallas_call(
        functools.partial(_resid_kernel, final),
        grid=(T // tm,),
        in_specs=[
            pl.BlockSpec((tm, D), lambda i: (i, 0)),
            pl.BlockSpec((N_TABLE_BLOCKS, tm, SLAB_ROWS, LANES), lambda i: (0, i, 0, 0)),
            pl.BlockSpec((1, 1, D), lambda i: (i // per_seq, 0, 0)),
            pl.BlockSpec((1, D), lambda i: (0, 0)),
        ],
        out_specs=pl.BlockSpec((tm, D), lambda i: (i, 0)),
        out_shape=jax.ShapeDtypeStruct((T, D), F32),
        compiler_params=_cparams(("parallel",)),
        name="peer_residual",
    )(x, peer_out, g2, final_g)


def _pair_expand_matrix():
    p = np.arange(PAIRS)[:, None]
    c = np.arange(PAIRS * SLAB_ROWS)[None, :]
    return (p == c // SLAB_ROWS).astype(np.float32)


def kernel(x, c, rel_bias, ada_w, ada_b, norm1_g, w_in, dw_w, dw_b, conv_ln_g, conv_ln_b, w_conv_out,
           attn_sinks, w_attn_out, w_out, norm2_g, w_pq, sub_keys, peer_u, peer_v, final_g):
    B, S, D = x.shape
    L = ada_w.shape[0]
    T = B * S
    C = dw_w.shape[2]
    A = N_HEADS * HEAD_DIM
    KV = N_KV_HEADS * HEAD_DIM
    E = peer_u.shape[1]
    assert S % 1024 == 0 and D == SLAB_ROWS * LANES and E % N_TABLE_BLOCKS == 0

    o_a, o_b, o_q, o_k, o_v, o_g = 0, C, 2 * C, 2 * C + A, 2 * C + A + KV, 2 * C + A + 2 * KV
    order = np.concatenate([np.arange(o_a, o_a + 2 * C), np.arange(o_g, o_g + 2 * D),
                            np.arange(o_q, o_q + A + 2 * KV)])
    n_a, n_b, n_gc, n_ga, n_q, n_k, n_v = 0, C, 2 * C, 2 * C + D, 2 * C + 2 * D, 2 * C + 2 * D + A, 2 * C + 2 * D + A + KV

    c_pad = jnp.pad(c, ((0, 8 - B), (0, 0)))
    mod = _ada(c_pad, ada_w, ada_b)
    bias_tab = _rel_bias_table(rel_bias)
    expand = jnp.asarray(_pair_expand_matrix(), dtype=BF16)
    row_sum = jnp.asarray(np.arange(MXU_DEPTH)[None, :] // SLAB_WORDS == np.arange(MXU_DEPTH // SLAB_WORDS)[:, None],
                          dtype=BF16)
    final_g2 = final_g.reshape(1, D)

    xt = x.reshape(T, D)
    for l in range(L):
        m = mod[l, :B].reshape(B, 6, 1, D)
        sh1, sc1, g1, sh2, sc2, g2 = (m[:, k] for k in range(6))

        w_in_l = w_in[l][:, order].astype(BF16)
        proj = _inproj(xt, norm1_g[l].reshape(1, D), sc1, sh1, w_in_l, S)
        dw_rep = jnp.broadcast_to(dw_w[l][:, None, :], (CONV_WIDTH, 8, C))
        cact = _conv(proj, dw_rep, dw_b[l].reshape(1, C), conv_ln_g[l].reshape(1, C),
                     conv_ln_b[l].reshape(1, C), S, n_a // C, n_b // C)
        attn = _attn(proj, bias_tab, attn_sinks[l], S, n_q // A, n_k // KV, n_v // KV)
        xt = _mix(cact, attn, proj, xt, g1, w_conv_out[l].astype(BF16), w_attn_out[l].astype(BF16),
                  w_out[l].astype(BF16), S, n_gc // D, n_ga // D)

        keys = sub_keys[l].reshape(2 * PEER_HEADS, N_KEYS, PEER_HALF).astype(BF16)
        h2, st = _peer_q(xt, norm2_g[l].reshape(1, D), sc2, sh2, w_pq[l].astype(BF16), keys, S)
        idxl, gate, ovf = _route(st, E // N_TABLE_BLOCKS)
        u_tab = _pack_table(peer_u, l)
        v_tab = _pack_table(peer_v, l)
        act = _peer_u(idxl, ovf, h2, row_sum, u_tab)
        pout = _peer_v(idxl, ovf, gate, act, expand, v_tab)
        xt = _resid(xt, pout, g2, final_g2, S, l == L - 1)
    return xt.reshape(B, S, D)
```

```python
import functools
import math

import numpy as np
import jax
import jax.numpy as jnp
from jax import lax
from jax.experimental import pallas as pl
from jax.experimental.pallas import tpu as pltpu

F32 = jnp.float32
BF16 = jnp.bfloat16
I32 = jnp.int32

EPS = 1e-6
CONV_WIDTH = 31
HALO = 32
N_HEADS = 16
N_KV_HEADS = 4
HEAD_DIM = 64
BLOCK = 128
NUM_BUCKETS = 32
MAX_DISTANCE = 128
N_KEYS = 128
PEER_HEADS = 8
PEER_TOPK = 16
PEER_HALF = 128
PAIRS = PEER_HEADS * PEER_TOPK
SLAB_ROWS = 16
LANES = 128
N_TABLE_BLOCKS = 2
TOKEN_UNROLL = 128
EXPERT_TILE = 128
HEAD_UNROLL = 4
MXU_DEPTH = 256
SLAB_WORDS = 8
WINDOW_SLOTS = 80
assert PAIRS & (PAIRS - 1) == 0 and WINDOW_SLOTS % 8 == 0 and 2 * WINDOW_SLOTS >= PAIRS
VMEM_LIMIT = 56 * 1024 * 1024


def _cparams(sem, vmem=VMEM_LIMIT):
    return pltpu.CompilerParams(dimension_semantics=sem, vmem_limit_bytes=vmem)


def _resident(block_shape, index_map):
    return pl.BlockSpec(block_shape, index_map, pipeline_mode=pl.Buffered(1))


def _ada_kernel(c_ref, w_ref, b_ref, o_ref):
    c = c_ref[...]
    cs = c * jax.nn.sigmoid(c)
    o_ref[0] = jnp.dot(cs.astype(BF16), w_ref[0].astype(BF16), preferred_element_type=F32) + b_ref[0]


def _ada(c_pad, ada_w, ada_b):
    L, D, N = ada_w.shape
    tn = 1024
    return pl.pallas_call(
        _ada_kernel,
        grid=(L, N // tn),
        in_specs=[
            pl.BlockSpec((8, D), lambda l, j: (0, 0)),
            pl.BlockSpec((1, D, tn), lambda l, j: (l, 0, j)),
            pl.BlockSpec((1, 1, tn), lambda l, j: (l, 0, j)),
        ],
        out_specs=pl.BlockSpec((1, 8, tn), lambda l, j: (l, 0, j)),
        out_shape=jax.ShapeDtypeStruct((L, 8, N), F32),
        compiler_params=_cparams(("parallel", "parallel")),
        name="ada_mod",
    )(c_pad, ada_w, ada_b.reshape(L, 1, N))


def _bias_kernel(bucket_ref, rb_ref, o_ref):
    h = pl.program_id(0)
    bucket = bucket_ref[...]
    acc = jnp.zeros(bucket.shape, F32)
    for b in range(NUM_BUCKETS):
        acc = jnp.where(bucket == b, rb_ref[b, h], acc)
    o_ref[0] = acc


def _t5_bucket_table():
    qi = np.arange(BLOCK)[:, None] + BLOCK
    kj = np.arange(2 * BLOCK)[None, :]
    d = np.maximum(qi - kj, 0)
    max_exact = NUM_BUCKETS // 2
    ratio = (np.log(np.maximum(d, 1).astype(np.float64) / max_exact)
             / math.log(MAX_DISTANCE / max_exact) * (NUM_BUCKETS - max_exact))
    large = np.minimum(max_exact + ratio.astype(np.int64), NUM_BUCKETS - 1)
    return np.where(d < max_exact, d, large).astype(np.int32)


def _rel_bias_table(rel_bias):
    bucket = jnp.asarray(_t5_bucket_table())
    return pl.pallas_call(
        _bias_kernel,
        grid=(N_HEADS,),
        in_specs=[
            pl.BlockSpec((BLOCK, 2 * BLOCK), lambda h: (0, 0)),
            pl.BlockSpec(memory_space=pltpu.SMEM),
        ],
        out_specs=pl.BlockSpec((1, BLOCK, 2 * BLOCK), lambda h: (h, 0, 0)),
        out_shape=jax.ShapeDtypeStruct((N_HEADS, BLOCK, 2 * BLOCK), F32),
        compiler_params=_cparams(("arbitrary",)),
        name="rel_bias_table",
    )(bucket, rel_bias)


def _norm_mod(x, g, sc, sh):
    ms = jnp.mean(x * x, axis=-1, keepdims=True)
    y = x * lax.rsqrt(ms + EPS)
    return (y * g) * (1.0 + sc) + sh


def _inproj_kernel(x_ref, g_ref, sc_ref, sh_ref, w_ref, o_ref, h_scr):
    @pl.when(pl.program_id(1) == 0)
    def _():
        h_scr[...] = _norm_mod(x_ref[...], g_ref[...], sc_ref[0], sh_ref[0]).astype(BF16)

    o_ref[...] = jnp.dot(h_scr[...], w_ref[...], preferred_element_type=F32)


def _inproj(x, g, sc, sh, w, seq):
    T, D = x.shape
    N = w.shape[1]
    tm, tn = 1024, 1536
    per_seq = seq // tm
    return pl.pallas_call(
        _inproj_kernel,
        grid=(T // tm, N // tn),
        in_specs=[
            pl.BlockSpec((tm, D), lambda i, j: (i, 0)),
            pl.BlockSpec((1, D), lambda i, j: (0, 0)),
            pl.BlockSpec((1, 1, D), lambda i, j: (i // per_seq, 0, 0)),
            pl.BlockSpec((1, 1, D), lambda i, j: (i // per_seq, 0, 0)),
            pl.BlockSpec((D, tn), lambda i, j: (0, j)),
        ],
        out_specs=pl.BlockSpec((tm, tn), lambda i, j: (i, j)),
        out_shape=jax.ShapeDtypeStruct((T, N), F32),
        scratch_shapes=[pltpu.VMEM((tm, D), BF16)],
        compiler_params=_cparams(("parallel", "arbitrary")),
        name="in_proj",
    )(x, g, sc, sh, w)


def _conv_kernel(per_seq, a_ref, b_ref, ha_ref, hb_ref, dw_ref, db_ref, lg_ref, lb_ref, o_ref, ext, acc_scr):
    ts = a_ref.shape[0]
    first = (pl.program_id(0) % per_seq) == 0
    hu = ha_ref[...] * jax.nn.sigmoid(hb_ref[...])
    ext[0:HALO, :] = jnp.where(first, 0.0, hu)
    ext[HALO:, :] = a_ref[...] * jax.nn.sigmoid(b_ref[...])
    base = HALO - (CONV_WIDTH - 1)
    n_ch = a_ref.shape[1]
    for c0 in range(0, n_ch, LANES):
        acc = jnp.zeros((ts // 8, 8, LANES), F32)
        for r in range(8):
            offs = [o for o in range(base, base + CONV_WIDTH) if o % 8 == r]
            if not offs:
                continue
            shifted = ext[r:r + 8 * (max(offs) // 8) + ts, c0:c0 + LANES]
            for o in offs:
                tap = dw_ref[o - base, :, c0:c0 + LANES]
                acc = acc + shifted[o - r:o - r + ts, :].reshape(ts // 8, 8, LANES) * tap[None]
        acc_scr[:, c0:c0 + LANES] = acc.reshape(ts, LANES)
    acc = acc_scr[...] + db_ref[...]
    mu = jnp.mean(acc, axis=-1, keepdims=True)
    cen = acc - mu
    var = jnp.mean(cen * cen, axis=-1, keepdims=True)
    y = cen * lax.rsqrt(var + EPS) * lg_ref[...] + lb_ref[...]
    o_ref[...] = (y * jax.nn.sigmoid(y)).astype(BF16)


def _conv(proj, dw_w, dw_b, ln_g, ln_b, seq, a_col, b_col):
    T = proj.shape[0]
    C = dw_w.shape[-1]
    ts = 256
    per_seq = seq // ts
    rb = ts // HALO
    halo_map = lambda col: (lambda i: (jnp.maximum(i * rb - 1, 0), col))
    return pl.pallas_call(
        functools.partial(_conv_kernel, per_seq),
        grid=(T // ts,),
        in_specs=[
            pl.BlockSpec((ts, C), lambda i: (i, a_col)),
            pl.BlockSpec((ts, C), lambda i: (i, b_col)),
            pl.BlockSpec((HALO, C), halo_map(a_col)),
            pl.BlockSpec((HALO, C), halo_map(b_col)),
            pl.BlockSpec((CONV_WIDTH, 8, C), lambda i: (0, 0, 0)),
            pl.BlockSpec((1, C), lambda i: (0, 0)),
            pl.BlockSpec((1, C), lambda i: (0, 0)),
            pl.BlockSpec((1, C), lambda i: (0, 0)),
        ],
        out_specs=pl.BlockSpec((ts, C), lambda i: (i, 0)),
        out_shape=jax.ShapeDtypeStruct((T, C), BF16),
        scratch_shapes=[pltpu.VMEM((HALO + ts, C), F32), pltpu.VMEM((ts, C), F32)],
        compiler_params=_cparams(("parallel",)),
        name="conv_branch",
    )(proj, proj, proj, proj, dw_w, dw_b, ln_g, ln_b)


def _attn_kernel(nb, q_ref, kc_ref, kp_ref, vc_ref, vp_ref, bias_ref, sink_ref, o_ref):
    first = (pl.program_id(0) % nb) == 0
    kk = jnp.concatenate([kp_ref[...], kc_ref[...]], axis=0).astype(BF16)
    vv = jnp.concatenate([vp_ref[...], vc_ref[...]], axis=0).astype(BF16)
    row = lax.broadcasted_iota(I32, (BLOCK, 2 * BLOCK), 0)
    col = lax.broadcasted_iota(I32, (BLOCK, 2 * BLOCK), 1)
    dist = row + BLOCK - col
    valid = (dist >= 0) & (dist < BLOCK) & (jnp.logical_not(first) | (col >= BLOCK))
    group = N_HEADS // N_KV_HEADS
    for kvh in range(N_KV_HEADS):
        k_h = kk[:, kvh * HEAD_DIM:(kvh + 1) * HEAD_DIM]
        v_h = vv[:, kvh * HEAD_DIM:(kvh + 1) * HEAD_DIM]
        qg = q_ref[:, kvh * group * HEAD_DIM:(kvh + 1) * group * HEAD_DIM].astype(BF16)
        outs = []
        for g in range(group):
            h = kvh * group + g
            q_h = qg[:, g * HEAD_DIM:(g + 1) * HEAD_DIM]
            s = lax.dot_general(q_h, k_h, (((1,), (1,)), ((), ())), preferred_element_type=F32)
            s = s * (HEAD_DIM ** -0.5) + bias_ref[h]
            s = jnp.where(valid, s, -jnp.inf)
            sink = sink_ref[h]
            m = jnp.maximum(jnp.max(s, axis=-1, keepdims=True), sink)
            e = jnp.exp(s - m)
            denom = jnp.sum(e, axis=-1, keepdims=True) + jnp.exp(sink - m)
            p = e / denom
            outs.append(jnp.dot(p.astype(BF16), v_h, preferred_element_type=F32))
        o_ref[:, kvh * group * HEAD_DIM:(kvh + 1) * group * HEAD_DIM] = (
            jnp.concatenate(outs, axis=1).astype(BF16))


def _attn(proj, bias_tab, sinks, seq, q_col, k_col, v_col):
    T = proj.shape[0]
    nb = seq // BLOCK
    attn_dim = N_HEADS * HEAD_DIM
    kv_dim = N_KV_HEADS * HEAD_DIM
    prev = lambda col: (lambda i: (jnp.where(i % nb == 0, i, i - 1), col))
    return pl.pallas_call(
        functools.partial(_attn_kernel, nb),
        grid=(T // BLOCK,),
        in_specs=[
            pl.BlockSpec((BLOCK, attn_dim), lambda i: (i, q_col)),
            pl.BlockSpec((BLOCK, kv_dim), lambda i: (i, k_col)),
            pl.BlockSpec((BLOCK, kv_dim), prev(k_col)),
            pl.BlockSpec((BLOCK, kv_dim), lambda i: (i, v_col)),
            pl.BlockSpec((BLOCK, kv_dim), prev(v_col)),
            pl.BlockSpec((N_HEADS, BLOCK, 2 * BLOCK), lambda i: (0, 0, 0)),
            pl.BlockSpec(memory_space=pltpu.SMEM),
        ],
        out_specs=pl.BlockSpec((BLOCK, attn_dim), lambda i: (i, 0)),
        out_shape=jax.ShapeDtypeStruct((T, attn_dim), BF16),
        compiler_params=_cparams(("parallel",)),
        name="swa_attention",
    )(proj, proj, proj, proj, proj, bias_tab, sinks)


def _mix_kernel(ca_ref, at_ref, gc_ref, ga_ref, x_ref, g1_ref, wc_ref, wa_ref, wo_ref, o_ref):
    yc = jnp.dot(ca_ref[...], wc_ref[...], preferred_element_type=F32)
    ya = jnp.dot(at_ref[...], wa_ref[...], preferred_element_type=F32)
    mixed = jax.nn.sigmoid(gc_ref[...]) * yc + jax.nn.sigmoid(ga_ref[...]) * ya
    y = jnp.dot(mixed.astype(BF16), wo_ref[...], preferred_element_type=F32)
    o_ref[...] = x_ref[...] + g1_ref[0] * y


def _mix(cact, attn, proj, x, g1, wc, wa, wo, seq, gc_col, ga_col):
    T, D = x.shape
    C = cact.shape[1]
    A = attn.shape[1]
    tm = 256
    per_seq = seq // tm
    return pl.pallas_call(
        _mix_kernel,
        grid=(T // tm,),
        in_specs=[
            pl.BlockSpec((tm, C), lambda i: (i, 0)),
            pl.BlockSpec((tm, A), lambda i: (i, 0)),
            pl.BlockSpec((tm, D), lambda i: (i, gc_col)),
            pl.BlockSpec((tm, D), lambda i: (i, ga_col)),
            pl.BlockSpec((tm, D), lambda i: (i, 0)),
            pl.BlockSpec((1, 1, D), lambda i: (i // per_seq, 0, 0)),
            _resident((C, D), lambda i: (0, 0)),
            _resident((A, D), lambda i: (0, 0)),
            _resident((D, D), lambda i: (0, 0)),
        ],
        out_specs=pl.BlockSpec((tm, D), lambda i: (i, 0)),
        out_shape=jax.ShapeDtypeStruct((T, D), F32),
        compiler_params=_cparams(("parallel",)),
        name="merge_out_proj",
    )(cact, attn, proj, proj, x, g1, wc, wa, wo)


def _store_packed_slabs(o_ref, xb):
    w = lax.bitcast_convert_type(xb.astype(F32), jnp.uint32)
    for i in range(SLAB_WORDS):
        lo = w[:, (2 * i) * LANES:(2 * i + 1) * LANES]
        hi = w[:, (2 * i + 1) * LANES:(2 * i + 2) * LANES]
        o_ref[:, i, :] = hi | (lo >> 16)


def _pack_kernel(x_ref, o_ref):
    _store_packed_slabs(o_ref, x_ref[0].astype(BF16))


def _pack_table(tabs, layer):
    _, n, d = tabs.shape
    te = 256
    packed = pl.pallas_call(
        _pack_kernel,
        grid=(n // te,),
        in_specs=[pl.BlockSpec((1, te, d), lambda i: (layer, i, 0))],
        out_specs=pl.BlockSpec((te, SLAB_WORDS, LANES), lambda i: (i, 0, 0)),
        out_shape=jax.ShapeDtypeStruct((n, SLAB_WORDS, LANES), jnp.uint32),
        compiler_params=_cparams(("parallel",)),
        name="pack_expert_table",
    )(tabs)
    return packed.reshape(n * SLAB_WORDS, LANES)


def _peer_q_kernel(x_ref, g_ref, sc_ref, sh_ref, w_ref, k_ref, h_ref, st_ref):
    hb = _norm_mod(x_ref[...], g_ref[...], sc_ref[0], sh_ref[0]).astype(BF16)
    _store_packed_slabs(h_ref, hb)
    qb = jnp.dot(hb, w_ref[...], preferred_element_type=F32).astype(BF16)
    for hp in range(2 * PEER_HEADS):
        q_hp = qb[:, hp * PEER_HALF:(hp + 1) * PEER_HALF]
        st_ref[hp * N_KEYS:(hp + 1) * N_KEYS, :] = lax.dot_general(
            k_ref[hp], q_hp, (((1,), (1,)), ((), ())), preferred_element_type=F32)


def _peer_q(x, g, sc, sh, w_pq, keys, seq):
    T, D = x.shape
    Q = w_pq.shape[1]
    tm = 256
    per_seq = seq // tm
    n_rows = 2 * PEER_HEADS * N_KEYS
    return pl.pallas_call(
        _peer_q_kernel,
        grid=(T // tm,),
        in_specs=[
            pl.BlockSpec((tm, D), lambda i: (i, 0)),
            pl.BlockSpec((1, D), lambda i: (0, 0)),
            pl.BlockSpec((1, 1, D), lambda i: (i // per_seq, 0, 0)),
            pl.BlockSpec((1, 1, D), lambda i: (i // per_seq, 0, 0)),
            _resident((D, Q), lambda i: (0, 0)),
            _resident((2 * PEER_HEADS, N_KEYS, PEER_HALF), lambda i: (0, 0, 0)),
        ],
        out_specs=[
            pl.BlockSpec((tm, SLAB_WORDS, LANES), lambda i: (i, 0, 0)),
            pl.BlockSpec((n_rows, tm), lambda i: (0, i)),
        ],
        out_shape=[
            jax.ShapeDtypeStruct((T, SLAB_WORDS, LANES), jnp.uint32),
            jax.ShapeDtypeStruct((n_rows, T), F32),
        ],
        compiler_params=_cparams(("parallel",)),
        name="peer_query_scores",
    )(x, g, sc, sh, w_pq, keys)


def _top_rows(x, payload, k, rank=None):
    rows = (lax.broadcasted_iota(I32, x.shape, 0) if rank is None else rank).astype(F32)
    vals, pays = [], []
    for _ in range(k):
        m = jnp.max(x, axis=0, keepdims=True)
        first = jnp.min(jnp.where(x == m, rows, jnp.float32(1 << 24)), axis=0, keepdims=True)
        onehot = rows == first
        if payload is None:
            pays.append(first.astype(I32))
        else:
            pays.append(jnp.sum(jnp.where(onehot, payload, 0), axis=0, keepdims=True))
        vals.append(m)
        x = jnp.where(onehot, -jnp.inf, x)
    return jnp.concatenate(vals, axis=0), jnp.concatenate(pays, axis=0)


def _sort_rows(keys, pays):
    n_groups = len(keys)
    n_rows = 8 * n_groups
    sub = lax.broadcasted_iota(I32, keys[0].shape, 0)
    k = 2
    while k <= n_rows:
        j = k // 2
        while j >= 1:
            if j >= 8:
                gj = j // 8
                for lo in range(n_groups):
                    if lo & gj:
                        continue
                    hi = lo | gj
                    ascending = ((8 * lo) & k) == 0
                    a, b = keys[lo], keys[hi]
                    swap = (a > b) if ascending else (a < b)
                    keys[lo], keys[hi] = jnp.where(swap, b, a), jnp.where(swap, a, b)
                    pa, pb = pays[lo], pays[hi]
                    pays[lo], pays[hi] = jnp.where(swap, pb, pa), jnp.where(swap, pa, pb)
            else:
                lower = (sub & j) == 0
                upper = (sub & j) != 0
                for g in range(n_groups):
                    if k >= 8:
                        ascending = ((8 * g) & k) == 0
                        want_max = upper if ascending else lower
                    else:
                        want_max = jnp.logical_xor(upper, (sub & k) != 0)
                    x, p = keys[g], pays[g]
                    xp = jnp.where(lower, pltpu.roll(x, 8 - j, 0), pltpu.roll(x, j, 0))
                    pp = jnp.where(lower, pltpu.roll(p, 8 - j, 0), pltpu.roll(p, j, 0))
                    swap = jnp.logical_xor(xp < x, want_max)
                    keys[g] = jnp.where(swap, xp, x)
                    pays[g] = jnp.where(swap, pp, p)
            j //= 2
        k *= 2
    return keys, pays


def _route_kernel(block_rows, st_ref, idxl_ref, gate_ref, ovf_ref, idx_scr, gate_scr):
    def head(h, carry):
        base = pl.multiple_of(h * (2 * N_KEYS), 2 * N_KEYS)
        v0, i0 = _top_rows(st_ref[pl.ds(base, N_KEYS), :], None, PEER_TOPK)
        v1, i1 = _top_rows(st_ref[pl.ds(base + N_KEYS, N_KEYS), :], None, PEER_TOPK)
        half = PEER_TOPK // 2
        sub = lax.broadcasted_iota(I32, (half, v0.shape[1]), 0)
        cand = [v0[k:k + 1] + v1[:half] for k in range(half)]
        cidx = [i0[k:k + 1] * N_KEYS + i1[:half] for k in range(half)]
        flat = [sub + k * PEER_TOPK for k in range(half)]
        cand += [v0[:1] + v1[half:], v0[half:] + v1[:1]]
        cidx += [i0[:1] * N_KEYS + i1[half:], i0[half:] * N_KEYS + i1[:1]]
        flat += [sub + half, (sub + half) * PEER_TOPK]
        best, eid = _top_rows(jnp.concatenate(cand, axis=0), jnp.concatenate(cidx, axis=0), PEER_TOPK,
                              rank=jnp.concatenate(flat, axis=0))
        e = jnp.exp(best - jnp.max(best, axis=0, keepdims=True))
        gates = e / jnp.sum(e, axis=0, keepdims=True)
        off = pl.multiple_of(h * PEER_TOPK, PEER_TOPK)
        idx_scr[pl.ds(off, PEER_TOPK), :] = eid
        gate_scr[pl.ds(off, PEER_TOPK), :] = gates
        return carry

    lax.fori_loop(0, PEER_HEADS, head, 0, unroll=HEAD_UNROLL)

    n_groups = PAIRS // 8
    sub = lax.broadcasted_iota(I32, (8, idx_scr.shape[1]), 0)
    keys = [idx_scr[8 * g:8 * g + 8, :] * PAIRS + (sub + 8 * g) for g in range(n_groups)]
    pays = [gate_scr[8 * g:8 * g + 8, :] for g in range(n_groups)]
    keys, pays = _sort_rows(keys, pays)
    eids = [jnp.right_shift(k, PAIRS.bit_length() - 1) for k in keys]
    row_hi, row_lo = WINDOW_SLOTS, PAIRS - WINDOW_SLOTS - 1
    ovf0 = jnp.where(eids[row_hi // 8][row_hi % 8:row_hi % 8 + 1, :] < block_rows, 1, 0)
    ovf1 = jnp.where(eids[row_lo // 8][row_lo % 8:row_lo % 8 + 1, :] >= block_rows, 1, 0)
    flags = jnp.concatenate([ovf0, ovf1], axis=0).astype(I32)
    tile_any = jnp.broadcast_to(jnp.max(flags, axis=1, keepdims=True), flags.shape)
    ovf_ref[...] = jnp.concatenate([flags, tile_any], axis=0)
    rot = (PAIRS - WINDOW_SLOTS) // 8
    for b in range(N_TABLE_BLOCKS):
        order = [(g + b * rot) % n_groups for g in range(n_groups)]
        e = jnp.concatenate([eids[g] for g in order], axis=0)
        inb = (e >= b * block_rows) & (e < (b + 1) * block_rows)
        local = jnp.where(inb, (e - b * block_rows) * SLAB_WORDS, 0)
        gate = jnp.where(inb, jnp.concatenate([pays[g] for g in order], axis=0), 0.0)
        idxl_ref[b] = local.astype(F32).T.astype(I32)
        gate_ref[b] = gate.T


def _route(st, block_rows):
    n_rows, T = st.shape
    tr = EXPERT_TILE
    return pl.pallas_call(
        functools.partial(_route_kernel, block_rows),
        grid=(T // tr,),
        in_specs=[pl.BlockSpec((n_rows, tr), lambda i: (0, i))],
        out_specs=[
            pl.BlockSpec((N_TABLE_BLOCKS, tr, PAIRS), lambda i: (0, i, 0)),
            pl.BlockSpec((N_TABLE_BLOCKS, tr, PAIRS), lambda i: (0, i, 0)),
            pl.BlockSpec((2 * N_TABLE_BLOCKS, tr), lambda i: (0, i)),
        ],
        out_shape=[
            jax.ShapeDtypeStruct((N_TABLE_BLOCKS, T, PAIRS), I32),
            jax.ShapeDtypeStruct((N_TABLE_BLOCKS, T, PAIRS), F32),
            jax.ShapeDtypeStruct((2 * N_TABLE_BLOCKS, T), I32),
        ],
        scratch_shapes=[pltpu.VMEM((PAIRS, tr), I32), pltpu.VMEM((PAIRS, tr), F32)],
        compiler_params=_cparams(("parallel",)),
        name="peer_route",
    )(st)


def _diag_mask():
    r = lax.broadcasted_iota(I32, (SLAB_ROWS, PAIRS * SLAB_ROWS), 0)
    c = lax.broadcasted_iota(I32, (SLAB_ROWS, PAIRS * SLAB_ROWS), 1)
    return r == (c % SLAB_ROWS)


def _for_overflow_tokens(ovf_ref, b, tm, body):
    group = 8
    assert tm == EXPERT_TILE

    def step(g, carry):
        flags = [ovf_ref[b, g * group + i] for i in range(group)]
        any_set = functools.reduce(lambda x, y: x | y, flags)

        @pl.when(any_set != 0)
        def _():
            def one(i, c):
                t = g * group + i

                @pl.when(ovf_ref[b, t] != 0)
                def _():
                    body(t)
                return c

            lax.fori_loop(0, group, one, 0)
        return carry

    @pl.when(ovf_ref[N_TABLE_BLOCKS + b, 0] != 0)
    def _():
        lax.fori_loop(0, tm // group, step, 0)


def _gather_slabs(tab_ref, idx_ref, t, slots):
    words = [tab_ref[pl.ds(pl.multiple_of(idx_ref[0, t, j], SLAB_WORDS), SLAB_WORDS), :] for j in slots]
    return pltpu.bitcast(jnp.concatenate(words, axis=0), BF16)


def _peer_u_kernel(idx_ref, ovf_ref, h_ref, sel_ref, u_ref, o_ref, part_scr):
    tm = h_ref.shape[0]
    b = pl.program_id(0)
    sel = sel_ref[...]
    per_tile = sel.shape[0]
    hi_mask = jnp.uint32(0xFFFF0000)

    def halves(words):
        return (lax.bitcast_convert_type(words << 16, F32), lax.bitcast_convert_type(words & hi_mask, F32))

    def lane_partials(t, slots):
        h_even, h_odd = halves(h_ref[t])
        sums = []
        for j in slots:
            u_even, u_odd = halves(u_ref[pl.ds(pl.multiple_of(idx_ref[0, t, j], SLAB_WORDS), SLAB_WORDS), :])
            sums.append(u_even * h_even + u_odd * h_odd)
        for k in range(0, len(sums), per_tile):
            grp = sums[k:k + per_tile]
            part_scr[t, slots[k]:slots[k] + len(grp), :] = jnp.dot(
                sel[:len(grp), :SLAB_WORDS * len(grp)], jnp.concatenate(grp, axis=0).astype(BF16),
                preferred_element_type=F32)

    def gather_trip(c):
        for i in range(TOKEN_UNROLL):
            lane_partials(c * TOKEN_UNROLL + i, range(WINDOW_SLOTS))

    def lane_sum_trip(c):
        for k in range(0, TOKEN_UNROLL, 8):
            r = pl.multiple_of(c * TOKEN_UNROLL + k, 8)
            o_ref[0, pl.ds(r, 8), 0:WINDOW_SLOTS] = jnp.sum(part_scr[pl.ds(r, 8), 0:WINDOW_SLOTS, :], axis=-1)

    def rare(t):
        lane_partials(t, range(WINDOW_SLOTS, PAIRS))
        o_ref[0, pl.ds(t, 1), WINDOW_SLOTS:] = jnp.sum(part_scr[pl.ds(t, 1), WINDOW_SLOTS:, :], axis=-1)

    def trip(c, carry):
        gather_trip(c)
        lane_sum_trip(c - 1)
        return carry

    n_trips = tm // TOKEN_UNROLL
    o_ref[0, :, WINDOW_SLOTS:] = jnp.zeros((tm, PAIRS - WINDOW_SLOTS), F32)
    gather_trip(0)
    lax.fori_loop(1, n_trips, trip, 0)
    lane_sum_trip(n_trips - 1)
    _for_overflow_tokens(ovf_ref, b, tm, rare)


def _peer_u(idxl, ovf, h_slab, row_sum, u_tab):
    T = h_slab.shape[0]
    block_words = u_tab.shape[0] // N_TABLE_BLOCKS
    tm = EXPERT_TILE
    return pl.pallas_call(
        _peer_u_kernel,
        grid=(N_TABLE_BLOCKS, T // tm),
        in_specs=[
            pl.BlockSpec((1, tm, PAIRS), lambda b, i: (b, i, 0), memory_space=pltpu.SMEM,
                         pipeline_mode=pl.Buffered(1)),
            pl.BlockSpec((2 * N_TABLE_BLOCKS, tm), lambda b, i: (0, i), memory_space=pltpu.SMEM),
            pl.BlockSpec((tm, SLAB_WORDS, LANES), lambda b, i: (i, 0, 0)),
            _resident(row_sum.shape, lambda b, i: (0, 0)),
            _resident((block_words, LANES), lambda b, i: (b, 0)),
        ],
        out_specs=pl.BlockSpec((1, tm, PAIRS), lambda b, i: (b, i, 0)),
        out_shape=jax.ShapeDtypeStruct((N_TABLE_BLOCKS, T, PAIRS), F32),
        scratch_shapes=[pltpu.VMEM((tm, PAIRS, LANES), F32)],
        compiler_params=_cparams(("arbitrary", "arbitrary")),
        name="peer_expert_u",
    )(idxl, ovf, h_slab, row_sum, u_tab)


def _peer_v_kernel(idx_ref, ovf_ref, gate_ref, act_ref, ex_ref, v_ref, o_ref, wexp):
    tm = gate_ref.shape[1]
    b = pl.program_id(0)
    act = act_ref[0]
    gelu = 0.5 * act * (1.0 + lax.erf(act * np.float32(math.sqrt(0.5))))
    w = gate_ref[0] * gelu
    wexp[...] = jnp.dot(w.astype(BF16), ex_ref[...], preferred_element_type=F32)
    eye = _diag_mask()
    wcols = WINDOW_SLOTS * SLAB_ROWS

    def weighted(t, slots, lo, hi):
        vsel = _gather_slabs(v_ref, idx_ref, t, slots)
        wrow = jnp.broadcast_to(wexp[pl.ds(t, 1), lo:hi], (SLAB_ROWS, hi - lo))
        wbd = jnp.where(eye[:, lo:hi], wrow, 0.0).astype(BF16)
        return jnp.dot(wbd, vsel, preferred_element_type=F32)

    def tok(t, carry):
        o_ref[0, t] = weighted(t, range(WINDOW_SLOTS), 0, wcols)
        return carry

    def rare(t):
        o_ref[0, t] = o_ref[0, t] + weighted(t, range(WINDOW_SLOTS, PAIRS), wcols, PAIRS * SLAB_ROWS)

    lax.fori_loop(0, tm, tok, 0, unroll=TOKEN_UNROLL)
    _for_overflow_tokens(ovf_ref, b, tm, rare)


def _peer_v(idxl, ovf, gate, act, expand, v_tab):
    T = gate.shape[1]
    block_words = v_tab.shape[0] // N_TABLE_BLOCKS
    tm = EXPERT_TILE
    return pl.pallas_call(
        _peer_v_kernel,
        grid=(N_TABLE_BLOCKS, T // tm),
        in_specs=[
            pl.BlockSpec((1, tm, PAIRS), lambda b, i: (b, i, 0), memory_space=pltpu.SMEM,
                         pipeline_mode=pl.Buffered(1)),
            pl.BlockSpec((2 * N_TABLE_BLOCKS, tm), lambda b, i: (0, i), memory_space=pltpu.SMEM),
            pl.BlockSpec((1, tm, PAIRS), lambda b, i: (b, i, 0)),
            pl.BlockSpec((1, tm, PAIRS), lambda b, i: (b, i, 0)),
            _resident((PAIRS, PAIRS * SLAB_ROWS), lambda b, i: (0, 0)),
            _resident((block_words, LANES), lambda b, i: (b, 0)),
        ],
        out_specs=pl.BlockSpec((1, tm, SLAB_ROWS, LANES), lambda b, i: (b, i, 0, 0)),
        out_shape=jax.ShapeDtypeStruct((N_TABLE_BLOCKS, T, SLAB_ROWS, LANES), F32),
        scratch_shapes=[pltpu.VMEM((tm, PAIRS * SLAB_ROWS), F32)],
        compiler_params=_cparams(("arbitrary", "arbitrary")),
        name="peer_expert_v",
    )(idxl, ovf, gate, act, expand, v_tab)


def _resid_kernel(final, x_ref, o_ref, g2_ref, fg_ref, y_ref):
    cols = []
    for s in range(SLAB_ROWS):
        part = o_ref[0, :, s, :]
        for k in range(1, N_TABLE_BLOCKS):
            part = part + o_ref[k, :, s, :]
        cols.append(part)
    x = x_ref[...] + g2_ref[0] * jnp.concatenate(cols, axis=1)
    if final:
        ms = jnp.mean(x * x, axis=-1, keepdims=True)
        x = (x * lax.rsqrt(ms + EPS)) * fg_ref[...]
    y_ref[...] = x


def _resid(x, peer_out, g2, final_g, seq, final):
    T, D = x.shape
    tm = 512
    per_seq = seq // tm
    return pl.pallas_call(
        functools.partial(_resid_kernel, final),
        grid=(T // tm,),
        in_specs=[
            pl.BlockSpec((tm, D), lambda i: (i, 0)),
            pl.BlockSpec((N_TABLE_BLOCKS, tm, SLAB_ROWS, LANES), lambda i: (0, i, 0, 0)),
            pl.BlockSpec((1, 1, D), lambda i: (i // per_seq, 0, 0)),
            pl.BlockSpec((1, D), lambda i: (0, 0)),
        ],
        out_specs=pl.BlockSpec((tm, D), lambda i: (i, 0)),
        out_shape=jax.ShapeDtypeStruct((T, D), F32),
        compiler_params=_cparams(("parallel",)),
        name="peer_residual",
    )(x, peer_out, g2, final_g)


def _pair_expand_matrix():
    p = np.arange(PAIRS)[:, None]
    c = np.arange(PAIRS * SLAB_ROWS)[None, :]
    return (p == c // SLAB_ROWS).astype(np.float32)


def kernel(x, c, rel_bias, ada_w, ada_b, norm1_g, w_in, dw_w, dw_b, conv_ln_g, conv_ln_b, w_conv_out,
           attn_sinks, w_attn_out, w_out, norm2_g, w_pq, sub_keys, peer_u, peer_v, final_g):
    B, S, D = x.shape
    L = ada_w.shape[0]
    T = B * S
    C = dw_w.shape[2]
    A = N_HEADS * HEAD_DIM
    KV = N_KV_HEADS * HEAD_DIM
    E = peer_u.shape[1]
    assert S % 1024 == 0 and D == SLAB_ROWS * LANES and E % N_TABLE_BLOCKS == 0

    o_a, o_b, o_q, o_k, o_v, o_g = 0, C, 2 * C, 2 * C + A, 2 * C + A + KV, 2 * C + A + 2 * KV
    order = np.concatenate([np.arange(o_a, o_a + 2 * C), np.arange(o_g, o_g + 2 * D),
                            np.arange(o_q, o_q + A + 2 * KV)])
    n_a, n_b, n_gc, n_ga, n_q, n_k, n_v = 0, C, 2 * C, 2 * C + D, 2 * C + 2 * D, 2 * C + 2 * D + A, 2 * C + 2 * D + A + KV

    c_pad = jnp.pad(c, ((0, 8 - B), (0, 0)))
    mod = _ada(c_pad, ada_w, ada_b)
    bias_tab = _rel_bias_table(rel_bias)
    expand = jnp.asarray(_pair_expand_matrix(), dtype=BF16)
    row_sum = jnp.asarray(np.arange(MXU_DEPTH)[None, :] // SLAB_WORDS == np.arange(MXU_DEPTH // SLAB_WORDS)[:, None],
                          dtype=BF16)
    final_g2 = final_g.reshape(1, D)

    xt = x.reshape(T, D)
    for l in range(L):
        m = mod[l, :B].reshape(B, 6, 1, D)
        sh1, sc1, g1, sh2, sc2, g2 = (m[:, k] for k in range(6))

        w_in_l = w_in[l][:, order].astype(BF16)
        proj = _inproj(xt, norm1_g[l].reshape(1, D), sc1, sh1, w_in_l, S)
        dw_rep = jnp.broadcast_to(dw_w[l][:, None, :], (CONV_WIDTH, 8, C))
        cact = _conv(proj, dw_rep, dw_b[l].reshape(1, C), conv_ln_g[l].reshape(1, C),
                     conv_ln_b[l].reshape(1, C), S, n_a // C, n_b // C)
        attn = _attn(proj, bias_tab, attn_sinks[l], S, n_q // A, n_k // KV, n_v // KV)
        xt = _mix(cact, attn, proj, xt, g1, w_conv_out[l].astype(BF16), w_attn_out[l].astype(BF16),
                  w_out[l].astype(BF16), S, n_gc // D, n_ga // D)

        keys = sub_keys[l].reshape(2 * PEER_HEADS, N_KEYS, PEER_HALF).astype(BF16)
        h2, st = _peer_q(xt, norm2_g[l].reshape(1, D), sc2, sh2, w_pq[l].astype(BF16), keys, S)
        idxl, gate, ovf = _route(st, E // N_TABLE_BLOCKS)
        u_tab = _pack_table(peer_u, l)
        v_tab = _pack_table(peer_v, l)
        act = _peer_u(idxl, ovf, h2, row_sum, u_tab)
        pout = _peer_v(idxl, ovf, gate, act, expand, v_tab)
        xt = _resid(xt, pout, g2, final_g2, S, l == L - 1)
    return xt.reshape(B, S, D)
```

```python
import functools
import math

import numpy as np
import jax
import jax.numpy as jnp
from jax import lax
from jax.experimental import pallas as pl
from jax.experimental.pallas import tpu as pltpu

F32 = jnp.float32
BF16 = jnp.bfloat16
I32 = jnp.int32

EPS = 1e-6
CONV_WIDTH = 31
HALO = 32
N_HEADS = 16
N_KV_HEADS = 4
HEAD_DIM = 64
BLOCK = 128
NUM_BUCKETS = 32
MAX_DISTANCE = 128
N_KEYS = 128
PEER_HEADS = 8
PEER_TOPK = 16
PEER_HALF = 128
PAIRS = PEER_HEADS * PEER_TOPK
SLAB_ROWS = 16
LANES = 128
N_TABLE_BLOCKS = 2
EXPERT_TILE = 128
TOKEN_UNROLL = EXPERT_TILE
HEAD_UNROLL = 4
MXU_DEPTH = 256
SLAB_WORDS = 8
WINDOW_SLOTS = 80
assert PAIRS & (PAIRS - 1) == 0 and WINDOW_SLOTS % 8 == 0 and 2 * WINDOW_SLOTS >= PAIRS
VMEM_LIMIT = 56 * 1024 * 1024


def _cparams(sem, vmem=VMEM_LIMIT):
    return pltpu.CompilerParams(dimension_semantics=sem, vmem_limit_bytes=vmem)


def _resident(block_shape, index_map):
    return pl.BlockSpec(block_shape, index_map, pipeline_mode=pl.Buffered(1))


def _ada_kernel(c_ref, w_ref, b_ref, o_ref):
    c = c_ref[...]
    cs = c * jax.nn.sigmoid(c)
    o_ref[0] = jnp.dot(cs.astype(BF16), w_ref[0].astype(BF16), preferred_element_type=F32) + b_ref[0]


def _ada(c_pad, ada_w, ada_b):
    L, D, N = ada_w.shape
    tn = 1024
    return pl.pallas_call(
        _ada_kernel,
        grid=(L, N // tn),
        in_specs=[
            pl.BlockSpec((8, D), lambda l, j: (0, 0)),
            pl.BlockSpec((1, D, tn), lambda l, j: (l, 0, j)),
            pl.BlockSpec((1, 1, tn), lambda l, j: (l, 0, j)),
        ],
        out_specs=pl.BlockSpec((1, 8, tn), lambda l, j: (l, 0, j)),
        out_shape=jax.ShapeDtypeStruct((L, 8, N), F32),
        compiler_params=_cparams(("parallel", "parallel")),
        name="ada_mod",
    )(c_pad, ada_w, ada_b.reshape(L, 1, N))


def _bias_kernel(bucket_ref, rb_ref, o_ref):
    h = pl.program_id(0)
    bucket = bucket_ref[...]
    acc = jnp.zeros(bucket.shape, F32)
    for b in range(NUM_BUCKETS):
        acc = jnp.where(bucket == b, rb_ref[b, h], acc)
    o_ref[0] = acc


def _t5_bucket_table():
    qi = np.arange(BLOCK)[:, None] + BLOCK
    kj = np.arange(2 * BLOCK)[None, :]
    d = np.maximum(qi - kj, 0)
    max_exact = NUM_BUCKETS // 2
    ratio = (np.log(np.maximum(d, 1).astype(np.float64) / max_exact)
             / math.log(MAX_DISTANCE / max_exact) * (NUM_BUCKETS - max_exact))
    large = np.minimum(max_exact + ratio.astype(np.int64), NUM_BUCKETS - 1)
    return np.where(d < max_exact, d, large).astype(np.int32)


def _rel_bias_table(rel_bias):
    bucket = jnp.asarray(_t5_bucket_table())
    return pl.pallas_call(
        _bias_kernel,
        grid=(N_HEADS,),
        in_specs=[
            pl.BlockSpec((BLOCK, 2 * BLOCK), lambda h: (0, 0)),
            pl.BlockSpec(memory_space=pltpu.SMEM),
        ],
        out_specs=pl.BlockSpec((1, BLOCK, 2 * BLOCK), lambda h: (h, 0, 0)),
        out_shape=jax.ShapeDtypeStruct((N_HEADS, BLOCK, 2 * BLOCK), F32),
        compiler_params=_cparams(("arbitrary",)),
        name="rel_bias_table",
    )(bucket, rel_bias)


def _norm_mod(x, g, sc, sh):
    ms = jnp.mean(x * x, axis=-1, keepdims=True)
    y = x * lax.rsqrt(ms + EPS)
    return (y * g) * (1.0 + sc) + sh


def _inproj_kernel(x_ref, g_ref, sc_ref, sh_ref, w_ref, o_ref, h_scr):
    @pl.when(pl.program_id(1) == 0)
    def _():
        h_scr[...] = _norm_mod(x_ref[...], g_ref[...], sc_ref[0], sh_ref[0]).astype(BF16)

    o_ref[...] = jnp.dot(h_scr[...], w_ref[...], preferred_element_type=F32)


def _inproj(x, g, sc, sh, w, seq):
    T, D = x.shape
    N = w.shape[1]
    tm, tn = 1024, 1536
    per_seq = seq // tm
    return pl.pallas_call(
        _inproj_kernel,
        grid=(T // tm, N // tn),
        in_specs=[
            pl.BlockSpec((tm, D), lambda i, j: (i, 0)),
            pl.BlockSpec((1, D), lambda i, j: (0, 0)),
            pl.BlockSpec((1, 1, D), lambda i, j: (i // per_seq, 0, 0)),
            pl.BlockSpec((1, 1, D), lambda i, j: (i // per_seq, 0, 0)),
            pl.BlockSpec((D, tn), lambda i, j: (0, j)),
        ],
        out_specs=pl.BlockSpec((tm, tn), lambda i, j: (i, j)),
        out_shape=jax.ShapeDtypeStruct((T, N), F32),
        scratch_shapes=[pltpu.VMEM((tm, D), BF16)],
        compiler_params=_cparams(("parallel", "arbitrary")),
        name="in_proj",
    )(x, g, sc, sh, w)


def _conv_kernel(per_seq, a_ref, b_ref, ha_ref, hb_ref, dw_ref, db_ref, lg_ref, lb_ref, o_ref, ext, acc_scr):
    ts = a_ref.shape[0]
    first = (pl.program_id(0) % per_seq) == 0
    hu = ha_ref[...] * jax.nn.sigmoid(hb_ref[...])
    ext[0:HALO, :] = jnp.where(first, 0.0, hu)
    ext[HALO:, :] = a_ref[...] * jax.nn.sigmoid(b_ref[...])
    base = HALO - (CONV_WIDTH - 1)
    n_ch = a_ref.shape[1]
    for c0 in range(0, n_ch, LANES):
        acc = jnp.zeros((ts // 8, 8, LANES), F32)
        for r in range(8):
            offs = [o for o in range(base, base + CONV_WIDTH) if o % 8 == r]
            if not offs:
                continue
            shifted = ext[r:r + 8 * (max(offs) // 8) + ts, c0:c0 + LANES]
            for o in offs:
                tap = dw_ref[o - base, :, c0:c0 + LANES]
                acc = acc + shifted[o - r:o - r + ts, :].reshape(ts // 8, 8, LANES) * tap[None]
        acc_scr[:, c0:c0 + LANES] = acc.reshape(ts, LANES)
    acc = acc_scr[...] + db_ref[...]
    mu = jnp.mean(acc, axis=-1, keepdims=True)
    cen = acc - mu
    var = jnp.mean(cen * cen, axis=-1, keepdims=True)
    y = cen * lax.rsqrt(var + EPS) * lg_ref[...] + lb_ref[...]
    o_ref[...] = (y * jax.nn.sigmoid(y)).astype(BF16)


def _conv(proj, dw_w, dw_b, ln_g, ln_b, seq, a_col, b_col):
    T = proj.shape[0]
    C = dw_w.shape[-1]
    ts = 256
    per_seq = seq // ts
    rb = ts // HALO
    halo_map = lambda col: (lambda i: (jnp.maximum(i * rb - 1, 0), col))
    return pl.pallas_call(
        functools.partial(_conv_kernel, per_seq),
        grid=(T // ts,),
        in_specs=[
            pl.BlockSpec((ts, C), lambda i: (i, a_col)),
            pl.BlockSpec((ts, C), lambda i: (i, b_col)),
            pl.BlockSpec((HALO, C), halo_map(a_col)),
            pl.BlockSpec((HALO, C), halo_map(b_col)),
            pl.BlockSpec((CONV_WIDTH, 8, C), lambda i: (0, 0, 0)),
            pl.BlockSpec((1, C), lambda i: (0, 0)),
            pl.BlockSpec((1, C), lambda i: (0, 0)),
            pl.BlockSpec((1, C), lambda i: (0, 0)),
        ],
        out_specs=pl.BlockSpec((ts, C), lambda i: (i, 0)),
        out_shape=jax.ShapeDtypeStruct((T, C), BF16),
        scratch_shapes=[pltpu.VMEM((HALO + ts, C), F32), pltpu.VMEM((ts, C), F32)],
        compiler_params=_cparams(("parallel",)),
        name="conv_branch",
    )(proj, proj, proj, proj, dw_w, dw_b, ln_g, ln_b)


def _attn_kernel(nb, q_ref, kc_ref, kp_ref, vc_ref, vp_ref, bias_ref, sink_ref, o_ref):
    first = (pl.program_id(0) % nb) == 0
    kk = jnp.concatenate([kp_ref[...], kc_ref[...]], axis=0).astype(BF16)
    vv = jnp.concatenate([vp_ref[...], vc_ref[...]], axis=0).astype(BF16)
    row = lax.broadcasted_iota(I32, (BLOCK, 2 * BLOCK), 0)
    col = lax.broadcasted_iota(I32, (BLOCK, 2 * BLOCK), 1)
    dist = row + BLOCK - col
    valid = (dist >= 0) & (dist < BLOCK) & (jnp.logical_not(first) | (col >= BLOCK))
    group = N_HEADS // N_KV_HEADS
    for kvh in range(N_KV_HEADS):
        k_h = kk[:, kvh * HEAD_DIM:(kvh + 1) * HEAD_DIM]
        v_h = vv[:, kvh * HEAD_DIM:(kvh + 1) * HEAD_DIM]
        qg = q_ref[:, kvh * group * HEAD_DIM:(kvh + 1) * group * HEAD_DIM].astype(BF16)
        outs = []
        for g in range(group):
            h = kvh * group + g
            q_h = qg[:, g * HEAD_DIM:(g + 1) * HEAD_DIM]
            s = lax.dot_general(q_h, k_h, (((1,), (1,)), ((), ())), preferred_element_type=F32)
            s = s * (HEAD_DIM ** -0.5) + bias_ref[h]
            s = jnp.where(valid, s, -jnp.inf)
            sink = sink_ref[h]
            m = jnp.maximum(jnp.max(s, axis=-1, keepdims=True), sink)
            e = jnp.exp(s - m)
            denom = jnp.sum(e, axis=-1, keepdims=True) + jnp.exp(sink - m)
            p = e / denom
            outs.append(jnp.dot(p.astype(BF16), v_h, preferred_element_type=F32))
        o_ref[:, kvh * group * HEAD_DIM:(kvh + 1) * group * HEAD_DIM] = (
            jnp.concatenate(outs, axis=1).astype(BF16))


def _attn(proj, bias_tab, sinks, seq, q_col, k_col, v_col):
    T = proj.shape[0]
    nb = seq // BLOCK
    attn_dim = N_HEADS * HEAD_DIM
    kv_dim = N_KV_HEADS * HEAD_DIM
    prev = lambda col: (lambda i: (jnp.where(i % nb == 0, i, i - 1), col))
    return pl.pallas_call(
        functools.partial(_attn_kernel, nb),
        grid=(T // BLOCK,),
        in_specs=[
            pl.BlockSpec((BLOCK, attn_dim), lambda i: (i, q_col)),
            pl.BlockSpec((BLOCK, kv_dim), lambda i: (i, k_col)),
            pl.BlockSpec((BLOCK, kv_dim), prev(k_col)),
            pl.BlockSpec((BLOCK, kv_dim), lambda i: (i, v_col)),
            pl.BlockSpec((BLOCK, kv_dim), prev(v_col)),
            pl.BlockSpec((N_HEADS, BLOCK, 2 * BLOCK), lambda i: (0, 0, 0)),
            pl.BlockSpec(memory_space=pltpu.SMEM),
        ],
        out_specs=pl.BlockSpec((BLOCK, attn_dim), lambda i: (i, 0)),
        out_shape=jax.ShapeDtypeStruct((T, attn_dim), BF16),
        compiler_params=_cparams(("parallel",)),
        name="swa_attention",
    )(proj, proj, proj, proj, proj, bias_tab, sinks)


def _mix_kernel(ca_ref, at_ref, gc_ref, ga_ref, x_ref, g1_ref, wc_ref, wa_ref, wo_ref, o_ref):
    yc = jnp.dot(ca_ref[...], wc_ref[...], preferred_element_type=F32)
    ya = jnp.dot(at_ref[...], wa_ref[...], preferred_element_type=F32)
    mixed = jax.nn.sigmoid(gc_ref[...]) * yc + jax.nn.sigmoid(ga_ref[...]) * ya
    y = jnp.dot(mixed.astype(BF16), wo_ref[...], preferred_element_type=F32)
    o_ref[...] = x_ref[...] + g1_ref[0] * y


def _mix(cact, attn, proj, x, g1, wc, wa, wo, seq, gc_col, ga_col):
    T, D = x.shape
    C = cact.shape[1]
    A = attn.shape[1]
    tm = 256
    per_seq = seq // tm
    return pl.pallas_call(
        _mix_kernel,
        grid=(T // tm,),
        in_specs=[
            pl.BlockSpec((tm, C), lambda i: (i, 0)),
            pl.BlockSpec((tm, A), lambda i: (i, 0)),
            pl.BlockSpec((tm, D), lambda i: (i, gc_col)),
            pl.BlockSpec((tm, D), lambda i: (i, ga_col)),
            pl.BlockSpec((tm, D), lambda i: (i, 0)),
            pl.BlockSpec((1, 1, D), lambda i: (i // per_seq, 0, 0)),
            _resident((C, D), lambda i: (0, 0)),
            _resident((A, D), lambda i: (0, 0)),
            _resident((D, D), lambda i: (0, 0)),
        ],
        out_specs=pl.BlockSpec((tm, D), lambda i: (i, 0)),
        out_shape=jax.ShapeDtypeStruct((T, D), F32),
        compiler_params=_cparams(("parallel",)),
        name="merge_out_proj",
    )(cact, attn, proj, proj, x, g1, wc, wa, wo)


def _store_packed_slabs(o_ref, xb):
    w = lax.bitcast_convert_type(xb.astype(F32), jnp.uint32)
    for i in range(SLAB_WORDS):
        lo = w[:, (2 * i) * LANES:(2 * i + 1) * LANES]
        hi = w[:, (2 * i + 1) * LANES:(2 * i + 2) * LANES]
        o_ref[:, i, :] = hi | (lo >> 16)


def _pack_kernel(x_ref, o_ref):
    _store_packed_slabs(o_ref, x_ref[0].astype(BF16))


def _pack_table(tabs, layer):
    _, n, d = tabs.shape
    te = 256
    packed = pl.pallas_call(
        _pack_kernel,
        grid=(n // te,),
        in_specs=[pl.BlockSpec((1, te, d), lambda i: (layer, i, 0))],
        out_specs=pl.BlockSpec((te, SLAB_WORDS, LANES), lambda i: (i, 0, 0)),
        out_shape=jax.ShapeDtypeStruct((n, SLAB_WORDS, LANES), jnp.uint32),
        compiler_params=_cparams(("parallel",)),
        name="pack_expert_table",
    )(tabs)
    return packed.reshape(n * SLAB_WORDS, LANES)


def _peer_q_kernel(x_ref, g_ref, sc_ref, sh_ref, w_ref, k_ref, h_ref, st_ref):
    hb = _norm_mod(x_ref[...], g_ref[...], sc_ref[0], sh_ref[0]).astype(BF16)
    _store_packed_slabs(h_ref, hb)
    qb = jnp.dot(hb, w_ref[...], preferred_element_type=F32).astype(BF16)
    for hp in range(2 * PEER_HEADS):
        q_hp = qb[:, hp * PEER_HALF:(hp + 1) * PEER_HALF]
        st_ref[hp * N_KEYS:(hp + 1) * N_KEYS, :] = lax.dot_general(
            k_ref[hp], q_hp, (((1,), (1,)), ((), ())), preferred_element_type=F32)


def _peer_q(x, g, sc, sh, w_pq, keys, seq):
    T, D = x.shape
    Q = w_pq.shape[1]
    tm = 256
    per_seq = seq // tm
    n_rows = 2 * PEER_HEADS * N_KEYS
    return pl.pallas_call(
        _peer_q_kernel,
        grid=(T // tm,),
        in_specs=[
            pl.BlockSpec((tm, D), lambda i: (i, 0)),
            pl.BlockSpec((1, D), lambda i: (0, 0)),
            pl.BlockSpec((1, 1, D), lambda i: (i // per_seq, 0, 0)),
            pl.BlockSpec((1, 1, D), lambda i: (i // per_seq, 0, 0)),
            _resident((D, Q), lambda i: (0, 0)),
            _resident((2 * PEER_HEADS, N_KEYS, PEER_HALF), lambda i: (0, 0, 0)),
        ],
        out_specs=[
            pl.BlockSpec((tm, SLAB_WORDS, LANES), lambda i: (i, 0, 0)),
            pl.BlockSpec((n_rows, tm), lambda i: (0, i)),
        ],
        out_shape=[
            jax.ShapeDtypeStruct((T, SLAB_WORDS, LANES), jnp.uint32),
            jax.ShapeDtypeStruct((n_rows, T), F32),
        ],
        compiler_params=_cparams(("parallel",)),
        name="peer_query_scores",
    )(x, g, sc, sh, w_pq, keys)


def _top_rows(x, payload, k, rank=None):
    rows = (lax.broadcasted_iota(I32, x.shape, 0) if rank is None else rank).astype(F32)
    vals, pays = [], []
    for _ in range(k):
        m = jnp.max(x, axis=0, keepdims=True)
        first = jnp.min(jnp.where(x == m, rows, jnp.float32(1 << 24)), axis=0, keepdims=True)
        onehot = rows == first
        if payload is None:
            pays.append(first.astype(I32))
        else:
            pays.append(jnp.sum(jnp.where(onehot, payload, 0), axis=0, keepdims=True))
        vals.append(m)
        x = jnp.where(onehot, -jnp.inf, x)
    return jnp.concatenate(vals, axis=0), jnp.concatenate(pays, axis=0)


def _sort_rows(keys, pays):
    n_groups = len(keys)
    n_rows = 8 * n_groups
    sub = lax.broadcasted_iota(I32, keys[0].shape, 0)
    k = 2
    while k <= n_rows:
        j = k // 2
        while j >= 1:
            if j >= 8:
                gj = j // 8
                for lo in range(n_groups):
                    if lo & gj:
                        continue
                    hi = lo | gj
                    ascending = ((8 * lo) & k) == 0
                    a, b = keys[lo], keys[hi]
                    swap = (a > b) if ascending else (a < b)
                    keys[lo], keys[hi] = jnp.where(swap, b, a), jnp.where(swap, a, b)
                    pa, pb = pays[lo], pays[hi]
                    pays[lo], pays[hi] = jnp.where(swap, pb, pa), jnp.where(swap, pa, pb)
            else:
                lower = (sub & j) == 0
                upper = (sub & j) != 0
                for g in range(n_groups):
                    if k >= 8:
                        ascending = ((8 * g) & k) == 0
                        want_max = upper if ascending else lower
                    else:
                        want_max = jnp.logical_xor(upper, (sub & k) != 0)
                    x, p = keys[g], pays[g]
                    xp = jnp.where(lower, pltpu.roll(x, 8 - j, 0), pltpu.roll(x, j, 0))
                    pp = jnp.where(lower, pltpu.roll(p, 8 - j, 0), pltpu.roll(p, j, 0))
                    swap = jnp.logical_xor(xp < x, want_max)
                    keys[g] = jnp.where(swap, xp, x)
                    pays[g] = jnp.where(swap, pp, p)
            j //= 2
        k *= 2
    return keys, pays


def _route_kernel(block_rows, st_ref, idxl_ref, gate_ref, ovf_ref, idx_scr, gate_scr):
    def head(h, carry):
        base = pl.multiple_of(h * (2 * N_KEYS), 2 * N_KEYS)
        v0, i0 = _top_rows(st_ref[pl.ds(base, N_KEYS), :], None, PEER_TOPK)
        v1, i1 = _top_rows(st_ref[pl.ds(base + N_KEYS, N_KEYS), :], None, PEER_TOPK)
        half = PEER_TOPK // 2
        sub = lax.broadcasted_iota(I32, (half, v0.shape[1]), 0)
        cand = [v0[k:k + 1] + v1[:half] for k in range(half)]
        cidx = [i0[k:k + 1] * N_KEYS + i1[:half] for k in range(half)]
        flat = [sub + k * PEER_TOPK for k in range(half)]
        cand += [v0[:1] + v1[half:], v0[half:] + v1[:1]]
        cidx += [i0[:1] * N_KEYS + i1[half:], i0[half:] * N_KEYS + i1[:1]]
        flat += [sub + half, (sub + half) * PEER_TOPK]
        best, eid = _top_rows(jnp.concatenate(cand, axis=0), jnp.concatenate(cidx, axis=0), PEER_TOPK,
                              rank=jnp.concatenate(flat, axis=0))
        e = jnp.exp(best - jnp.max(best, axis=0, keepdims=True))
        gates = e / jnp.sum(e, axis=0, keepdims=True)
        off = pl.multiple_of(h * PEER_TOPK, PEER_TOPK)
        idx_scr[pl.ds(off, PEER_TOPK), :] = eid
        gate_scr[pl.ds(off, PEER_TOPK), :] = gates
        return carry

    lax.fori_loop(0, PEER_HEADS, head, 0, unroll=HEAD_UNROLL)

    n_groups = PAIRS // 8
    sub = lax.broadcasted_iota(I32, (8, idx_scr.shape[1]), 0)
    keys = [idx_scr[8 * g:8 * g + 8, :] * PAIRS + (sub + 8 * g) for g in range(n_groups)]
    pays = [gate_scr[8 * g:8 * g + 8, :] for g in range(n_groups)]
    keys, pays = _sort_rows(keys, pays)
    eids = [jnp.right_shift(k, PAIRS.bit_length() - 1) for k in keys]
    row_hi, row_lo = WINDOW_SLOTS, PAIRS - WINDOW_SLOTS - 1
    ovf0 = jnp.where(eids[row_hi // 8][row_hi % 8:row_hi % 8 + 1, :] < block_rows, 1, 0)
    ovf1 = jnp.where(eids[row_lo // 8][row_lo % 8:row_lo % 8 + 1, :] >= block_rows, 1, 0)
    flags = jnp.concatenate([ovf0, ovf1], axis=0).astype(I32)
    tile_any = jnp.broadcast_to(jnp.max(flags, axis=1, keepdims=True), flags.shape)
    ovf_ref[...] = jnp.concatenate([flags, tile_any], axis=0)
    rot = (PAIRS - WINDOW_SLOTS) // 8
    for b in range(N_TABLE_BLOCKS):
        order = [(g + b * rot) % n_groups for g in range(n_groups)]
        e = jnp.concatenate([eids[g] for g in order], axis=0)
        inb = (e >= b * block_rows) & (e < (b + 1) * block_rows)
        local = jnp.where(inb, (e - b * block_rows) * SLAB_WORDS, 0)
        gate = jnp.where(inb, jnp.concatenate([pays[g] for g in order], axis=0), 0.0)
        idxl_ref[b] = local.astype(F32).T.astype(I32)
        gate_ref[b] = gate.T


def _route(st, block_rows):
    n_rows, T = st.shape
    tr = EXPERT_TILE
    return pl.pallas_call(
        functools.partial(_route_kernel, block_rows),
        grid=(T // tr,),
        in_specs=[pl.BlockSpec((n_rows, tr), lambda i: (0, i))],
        out_specs=[
            pl.BlockSpec((N_TABLE_BLOCKS, tr, PAIRS), lambda i: (0, i, 0)),
            pl.BlockSpec((N_TABLE_BLOCKS, tr, PAIRS), lambda i: (0, i, 0)),
            pl.BlockSpec((2 * N_TABLE_BLOCKS, tr), lambda i: (0, i)),
        ],
        out_shape=[
            jax.ShapeDtypeStruct((N_TABLE_BLOCKS, T, PAIRS), I32),
            jax.ShapeDtypeStruct((N_TABLE_BLOCKS, T, PAIRS), F32),
            jax.ShapeDtypeStruct((2 * N_TABLE_BLOCKS, T), I32),
        ],
        scratch_shapes=[pltpu.VMEM((PAIRS, tr), I32), pltpu.VMEM((PAIRS, tr), F32)],
        compiler_params=_cparams(("parallel",)),
        name="peer_route",
    )(st)


def _diag_mask():
    r = lax.broadcasted_iota(I32, (SLAB_ROWS, PAIRS * SLAB_ROWS), 0)
    c = lax.broadcasted_iota(I32, (SLAB_ROWS, PAIRS * SLAB_ROWS), 1)
    return r == (c % SLAB_ROWS)


def _for_overflow_tokens(ovf_ref, b, tm, body):
    group = 8
    assert tm == EXPERT_TILE

    def step(g, carry):
        flags = [ovf_ref[b, g * group + i] for i in range(group)]
        any_set = functools.reduce(lambda x, y: x | y, flags)

        @pl.when(any_set != 0)
        def _():
            def one(i, c):
                t = g * group + i

                @pl.when(ovf_ref[b, t] != 0)
                def _():
                    body(t)
                return c

            lax.fori_loop(0, group, one, 0)
        return carry

    @pl.when(ovf_ref[N_TABLE_BLOCKS + b, 0] != 0)
    def _():
        lax.fori_loop(0, tm // group, step, 0)


def _gather_slabs(tab_ref, idx_ref, t, slots):
    words = [tab_ref[pl.ds(pl.multiple_of(idx_ref[0, t, j], SLAB_WORDS), SLAB_WORDS), :] for j in slots]
    return pltpu.bitcast(jnp.concatenate(words, axis=0), BF16)


def _peer_u_kernel(idx_ref, ovf_ref, h_ref, sel_ref, u_ref, o_ref, part_scr):
    tm = h_ref.shape[0]
    b = pl.program_id(0)
    sel = sel_ref[...]
    per_tile = sel.shape[0]
    hi_mask = jnp.uint32(0xFFFF0000)

    def halves(words):
        return (lax.bitcast_convert_type(words << 16, F32), lax.bitcast_convert_type(words & hi_mask, F32))

    def lane_partials(t, slots):
        h_even, h_odd = halves(h_ref[t])
        sums = []
        for j in slots:
            u_even, u_odd = halves(u_ref[pl.ds(pl.multiple_of(idx_ref[0, t, j], SLAB_WORDS), SLAB_WORDS), :])
            sums.append(u_even * h_even + u_odd * h_odd)
        for k in range(0, len(sums), per_tile):
            grp = sums[k:k + per_tile]
            part_scr[t, slots[k]:slots[k] + len(grp), :] = jnp.dot(
                sel[:len(grp), :SLAB_WORDS * len(grp)], jnp.concatenate(grp, axis=0).astype(BF16),
                preferred_element_type=F32)

    def gather_trip(c):
        for i in range(TOKEN_UNROLL):
            lane_partials(c * TOKEN_UNROLL + i, range(WINDOW_SLOTS))

    def lane_sum_trip(c):
        for k in range(0, TOKEN_UNROLL, 8):
            r = pl.multiple_of(c * TOKEN_UNROLL + k, 8)
            o_ref[0, pl.ds(r, 8), 0:WINDOW_SLOTS] = jnp.sum(part_scr[pl.ds(r, 8), 0:WINDOW_SLOTS, :], axis=-1)

    def rare(t):
        lane_partials(t, range(WINDOW_SLOTS, PAIRS))
        o_ref[0, pl.ds(t, 1), WINDOW_SLOTS:] = jnp.sum(part_scr[pl.ds(t, 1), WINDOW_SLOTS:, :], axis=-1)

    def trip(c, carry):
        gather_trip(c)
        lane_sum_trip(c - 1)
        return carry

    n_trips = tm // TOKEN_UNROLL
    o_ref[0, :, WINDOW_SLOTS:] = jnp.zeros((tm, PAIRS - WINDOW_SLOTS), F32)
    gather_trip(0)
    lax.fori_loop(1, n_trips, trip, 0)
    lane_sum_trip(n_trips - 1)
    _for_overflow_tokens(ovf_ref, b, tm, rare)


def _peer_u(idxl, ovf, h_slab, row_sum, u_tab):
    T = h_slab.shape[0]
    block_words = u_tab.shape[0] // N_TABLE_BLOCKS
    tm = EXPERT_TILE
    return pl.pallas_call(
        _peer_u_kernel,
        grid=(N_TABLE_BLOCKS, T // tm),
        in_specs=[
            pl.BlockSpec((1, tm, PAIRS), lambda b, i: (b, i, 0), memory_space=pltpu.SMEM),
            pl.BlockSpec((2 * N_TABLE_BLOCKS, tm), lambda b, i: (0, i), memory_space=pltpu.SMEM),
            pl.BlockSpec((tm, SLAB_WORDS, LANES), lambda b, i: (i, 0, 0)),
            _resident(row_sum.shape, lambda b, i: (0, 0)),
            _resident((block_words, LANES), lambda b, i: (b, 0)),
        ],
        out_specs=pl.BlockSpec((1, tm, PAIRS), lambda b, i: (b, i, 0)),
        out_shape=jax.ShapeDtypeStruct((N_TABLE_BLOCKS, T, PAIRS), F32),
        scratch_shapes=[pltpu.VMEM((tm, PAIRS, LANES), F32)],
        compiler_params=_cparams(("arbitrary", "arbitrary")),
        name="peer_expert_u",
    )(idxl, ovf, h_slab, row_sum, u_tab)


def _peer_v_kernel(idx_ref, ovf_ref, gate_ref, act_ref, ex_ref, v_ref, o_ref, wexp):
    tm = gate_ref.shape[1]
    b = pl.program_id(0)
    act = act_ref[0]
    gelu = 0.5 * act * (1.0 + lax.erf(act * np.float32(math.sqrt(0.5))))
    w = gate_ref[0] * gelu
    wexp[...] = jnp.dot(w.astype(BF16), ex_ref[...], preferred_element_type=F32)
    eye = _diag_mask()
    wcols = WINDOW_SLOTS * SLAB_ROWS

    def weighted(t, slots, lo, hi):
        vsel = _gather_slabs(v_ref, idx_ref, t, slots)
        wrow = jnp.broadcast_to(wexp[pl.ds(t, 1), lo:hi], (SLAB_ROWS, hi - lo))
        wbd = jnp.where(eye[:, lo:hi], wrow, 0.0).astype(BF16)
        return jnp.dot(wbd, vsel, preferred_element_type=F32)

    def tok(t, carry):
        o_ref[0, t] = weighted(t, range(WINDOW_SLOTS), 0, wcols)
        return carry

    def rare(t):
        o_ref[0, t] = o_ref[0, t] + weighted(t, range(WINDOW_SLOTS, PAIRS), wcols, PAIRS * SLAB_ROWS)

    lax.fori_loop(0, tm, tok, 0, unroll=TOKEN_UNROLL)
    _for_overflow_tokens(ovf_ref, b, tm, rare)


def _peer_v(idxl, ovf, gate, act, expand, v_tab):
    T = gate.shape[1]
    block_words = v_tab.shape[0] // N_TABLE_BLOCKS
    tm = EXPERT_TILE
    return pl.pallas_call(
        _peer_v_kernel,
        grid=(N_TABLE_BLOCKS, T // tm),
        in_specs=[
            pl.BlockSpec((1, tm, PAIRS), lambda b, i: (b, i, 0), memory_space=pltpu.SMEM,
                         pipeline_mode=pl.Buffered(1)),
            pl.BlockSpec((2 * N_TABLE_BLOCKS, tm), lambda b, i: (0, i), memory_space=pltpu.SMEM),
            pl.BlockSpec((1, tm, PAIRS), lambda b, i: (b, i, 0)),
            pl.BlockSpec((1, tm, PAIRS), lambda b, i: (b, i, 0)),
            _resident((PAIRS, PAIRS * SLAB_ROWS), lambda b, i: (0, 0)),
            _resident((block_words, LANES), lambda b, i: (b, 0)),
        ],
        out_specs=pl.BlockSpec((1, tm, SLAB_ROWS, LANES), lambda b, i: (b, i, 0, 0)),
        out_shape=jax.ShapeDtypeStruct((N_TABLE_BLOCKS, T, SLAB_ROWS, LANES), F32),
        scratch_shapes=[pltpu.VMEM((tm, PAIRS * SLAB_ROWS), F32)],
        compiler_params=_cparams(("arbitrary", "arbitrary")),
        name="peer_expert_v",
    )(idxl, ovf, gate, act, expand, v_tab)


def _resid_kernel(final, x_ref, o_ref, g2_ref, fg_ref, y_ref):
    cols = []
    for s in range(SLAB_ROWS):
        part = o_ref[0, :, s, :]
        for k in range(1, N_TABLE_BLOCKS):
            part = part + o_ref[k, :, s, :]
        cols.append(part)
    x = x_ref[...] + g2_ref[0] * jnp.concatenate(cols, axis=1)
    if final:
        ms = jnp.mean(x * x, axis=-1, keepdims=True)
        x = (x * lax.rsqrt(ms + EPS)) * fg_ref[...]
    y_ref[...] = x


def _resid(x, peer_out, g2, final_g, seq, final):
    T, D = x.shape
    tm = 512
    per_seq = seq // tm
    return pl.pallas_call(
        functools.partial(_resid_kernel, final),
        grid=(T // tm,),
        in_specs=[
            pl.BlockSpec((tm, D), lambda i: (i, 0)),
            pl.BlockSpec((N_TABLE_BLOCKS, tm, SLAB_ROWS, LANES), lambda i: (0, i, 0, 0)),
            pl.BlockSpec((1, 1, D), lambda i: (i // per_seq, 0, 0)),
            pl.BlockSpec((1, D), lambda i: (0, 0)),
        ],
        out_specs=pl.BlockSpec((tm, D), lambda i: (i, 0)),
        out_shape=jax.ShapeDtypeStruct((T, D), F32),
        compiler_params=_cparams(("parallel",)),
        name="peer_residual",
    )(x, peer_out, g2, final_g)


def _pair_expand_matrix():
    p = np.arange(PAIRS)[:, None]
    c = np.arange(PAIRS * SLAB_ROWS)[None, :]
    return (p == c // SLAB_ROWS).astype(np.float32)


def kernel(x, c, rel_bias, ada_w, ada_b, norm1_g, w_in, dw_w, dw_b, conv_ln_g, conv_ln_b, w_conv_out,
           attn_sinks, w_attn_out, w_out, norm2_g, w_pq, sub_keys, peer_u, peer_v, final_g):
    B, S, D = x.shape
    L = ada_w.shape[0]
    T = B * S
    C = dw_w.shape[2]
    A = N_HEADS * HEAD_DIM
    KV = N_KV_HEADS * HEAD_DIM
    E = peer_u.shape[1]
    assert S % 1024 == 0 and D == SLAB_ROWS * LANES and E % N_TABLE_BLOCKS == 0

    o_a, o_b, o_q, o_k, o_v, o_g = 0, C, 2 * C, 2 * C + A, 2 * C + A + KV, 2 * C + A + 2 * KV
    order = np.concatenate([np.arange(o_a, o_a + 2 * C), np.arange(o_g, o_g + 2 * D),
                            np.arange(o_q, o_q + A + 2 * KV)])
    n_a, n_b, n_gc, n_ga, n_q, n_k, n_v = 0, C, 2 * C, 2 * C + D, 2 * C + 2 * D, 2 * C + 2 * D + A, 2 * C + 2 * D + A + KV

    c_pad = jnp.pad(c, ((0, 8 - B), (0, 0)))
    mod = _ada(c_pad, ada_w, ada_b)
    bias_tab = _rel_bias_table(rel_bias)
    expand = jnp.asarray(_pair_expand_matrix(), dtype=BF16)
    row_sum = jnp.asarray(np.arange(MXU_DEPTH)[None, :] // SLAB_WORDS == np.arange(MXU_DEPTH // SLAB_WORDS)[:, None],
                          dtype=BF16)
    final_g2 = final_g.reshape(1, D)

    xt = x.reshape(T, D)
    for l in range(L):
        m = mod[l, :B].reshape(B, 6, 1, D)
        sh1, sc1, g1, sh2, sc2, g2 = (m[:, k] for k in range(6))

        w_in_l = w_in[l][:, order].astype(BF16)
        proj = _inproj(xt, norm1_g[l].reshape(1, D), sc1, sh1, w_in_l, S)
        dw_rep = jnp.broadcast_to(dw_w[l][:, None, :], (CONV_WIDTH, 8, C))
        cact = _conv(proj, dw_rep, dw_b[l].reshape(1, C), conv_ln_g[l].reshape(1, C),
                     conv_ln_b[l].reshape(1, C), S, n_a // C, n_b // C)
        attn = _attn(proj, bias_tab, attn_sinks[l], S, n_q // A, n_k // KV, n_v // KV)
        xt = _mix(cact, attn, proj, xt, g1, w_conv_out[l].astype(BF16), w_attn_out[l].astype(BF16),
                  w_out[l].astype(BF16), S, n_gc // D, n_ga // D)

        keys = sub_keys[l].reshape(2 * PEER_HEADS, N_KEYS, PEER_HALF).astype(BF16)
        h2, st = _peer_q(xt, norm2_g[l].reshape(1, D), sc2, sh2, w_pq[l].astype(BF16), keys, S)
        idxl, gate, ovf = _route(st, E // N_TABLE_BLOCKS)
        u_tab = _pack_table(peer_u, l)
        v_tab = _pack_table(peer_v, l)
        act = _peer_u(idxl, ovf, h2, row_sum, u_tab)
        pout = _peer_v(idxl, ovf, gate, act, expand, v_tab)
        xt = _resid(xt, pout, g2, final_g2, S, l == L - 1)
    return xt.reshape(B, S, D)
```

```python
import functools
import math

import numpy as np
import jax
import jax.numpy as jnp
from jax import lax
from jax.experimental import pallas as pl
from jax.experimental.pallas import tpu as pltpu

F32 = jnp.float32
BF16 = jnp.bfloat16
I32 = jnp.int32

EPS = 1e-6
CONV_WIDTH = 31
HALO = 32
N_HEADS = 16
N_KV_HEADS = 4
HEAD_DIM = 64
BLOCK = 128
NUM_BUCKETS = 32
MAX_DISTANCE = 128
N_KEYS = 128
PEER_HEADS = 8
PEER_TOPK = 16
PEER_HALF = 128
PAIRS = PEER_HEADS * PEER_TOPK
SLAB_ROWS = 16
LANES = 128
N_TABLE_BLOCKS = 2
EXPERT_TILE = 128
TOKEN_UNROLL = EXPERT_TILE
HEAD_UNROLL = 4
MXU_DEPTH = 256
SLAB_WORDS = 8
WINDOW_SLOTS = 80
assert PAIRS & (PAIRS - 1) == 0 and WINDOW_SLOTS % 8 == 0 and 2 * WINDOW_SLOTS >= PAIRS
VMEM_LIMIT = 56 * 1024 * 1024


def _cparams(sem, vmem=VMEM_LIMIT):
    return pltpu.CompilerParams(dimension_semantics=sem, vmem_limit_bytes=vmem)


def _resident(block_shape, index_map):
    return pl.BlockSpec(block_shape, index_map, pipeline_mode=pl.Buffered(1))


def _ada_kernel(c_ref, w_ref, b_ref, o_ref):
    c = c_ref[...]
    cs = c * jax.nn.sigmoid(c)
    o_ref[0] = jnp.dot(cs.astype(BF16), w_ref[0].astype(BF16), preferred_element_type=F32) + b_ref[0]


def _ada(c_pad, ada_w, ada_b):
    L, D, N = ada_w.shape
    tn = 1024
    return pl.pallas_call(
        _ada_kernel,
        grid=(L, N // tn),
        in_specs=[
            pl.BlockSpec((8, D), lambda l, j: (0, 0)),
            pl.BlockSpec((1, D, tn), lambda l, j: (l, 0, j)),
            pl.BlockSpec((1, 1, tn), lambda l, j: (l, 0, j)),
        ],
        out_specs=pl.BlockSpec((1, 8, tn), lambda l, j: (l, 0, j)),
        out_shape=jax.ShapeDtypeStruct((L, 8, N), F32),
        compiler_params=_cparams(("parallel", "parallel")),
        name="ada_mod",
    )(c_pad, ada_w, ada_b.reshape(L, 1, N))


def _bias_kernel(bucket_ref, rb_ref, o_ref):
    h = pl.program_id(0)
    bucket = bucket_ref[...]
    acc = jnp.zeros(bucket.shape, F32)
    for b in range(NUM_BUCKETS):
        acc = jnp.where(bucket == b, rb_ref[b, h], acc)
    o_ref[0] = acc


def _t5_bucket_table():
    qi = np.arange(BLOCK)[:, None] + BLOCK
    kj = np.arange(2 * BLOCK)[None, :]
    d = np.maximum(qi - kj, 0)
    max_exact = NUM_BUCKETS // 2
    ratio = (np.log(np.maximum(d, 1).astype(np.float64) / max_exact)
             / math.log(MAX_DISTANCE / max_exact) * (NUM_BUCKETS - max_exact))
    large = np.minimum(max_exact + ratio.astype(np.int64), NUM_BUCKETS - 1)
    return np.where(d < max_exact, d, large).astype(np.int32)


def _rel_bias_table(rel_bias):
    bucket = jnp.asarray(_t5_bucket_table())
    return pl.pallas_call(
        _bias_kernel,
        grid=(N_HEADS,),
        in_specs=[
            pl.BlockSpec((BLOCK, 2 * BLOCK), lambda h: (0, 0)),
            pl.BlockSpec(memory_space=pltpu.SMEM),
        ],
        out_specs=pl.BlockSpec((1, BLOCK, 2 * BLOCK), lambda h: (h, 0, 0)),
        out_shape=jax.ShapeDtypeStruct((N_HEADS, BLOCK, 2 * BLOCK), F32),
        compiler_params=_cparams(("arbitrary",)),
        name="rel_bias_table",
    )(bucket, rel_bias)


def _norm_mod(x, g, sc, sh):
    ms = jnp.mean(x * x, axis=-1, keepdims=True)
    y = x * lax.rsqrt(ms + EPS)
    return (y * g) * (1.0 + sc) + sh


def _inproj_kernel(x_ref, g_ref, sc_ref, sh_ref, w_ref, o_ref, h_scr):
    @pl.when(pl.program_id(1) == 0)
    def _():
        h_scr[...] = _norm_mod(x_ref[...], g_ref[...], sc_ref[0], sh_ref[0]).astype(BF16)

    o_ref[...] = jnp.dot(h_scr[...], w_ref[...], preferred_element_type=F32)


def _inproj(x, g, sc, sh, w, seq):
    T, D = x.shape
    N = w.shape[1]
    tm, tn = 1024, 1536
    per_seq = seq // tm
    return pl.pallas_call(
        _inproj_kernel,
        grid=(T // tm, N // tn),
        in_specs=[
            pl.BlockSpec((tm, D), lambda i, j: (i, 0)),
            pl.BlockSpec((1, D), lambda i, j: (0, 0)),
            pl.BlockSpec((1, 1, D), lambda i, j: (i // per_seq, 0, 0)),
            pl.BlockSpec((1, 1, D), lambda i, j: (i // per_seq, 0, 0)),
            pl.BlockSpec((D, tn), lambda i, j: (0, j)),
        ],
        out_specs=pl.BlockSpec((tm, tn), lambda i, j: (i, j)),
        out_shape=jax.ShapeDtypeStruct((T, N), F32),
        scratch_shapes=[pltpu.VMEM((tm, D), BF16)],
        compiler_params=_cparams(("parallel", "arbitrary")),
        name="in_proj",
    )(x, g, sc, sh, w)


def _conv_kernel(per_seq, a_ref, b_ref, ha_ref, hb_ref, dw_ref, db_ref, lg_ref, lb_ref, o_ref, ext, acc_scr):
    ts = a_ref.shape[0]
    first = (pl.program_id(0) % per_seq) == 0
    hu = ha_ref[...] * jax.nn.sigmoid(hb_ref[...])
    ext[0:HALO, :] = jnp.where(first, 0.0, hu)
    ext[HALO:, :] = a_ref[...] * jax.nn.sigmoid(b_ref[...])
    base = HALO - (CONV_WIDTH - 1)
    n_ch = a_ref.shape[1]
    for c0 in range(0, n_ch, LANES):
        acc = jnp.zeros((ts // 8, 8, LANES), F32)
        for r in range(8):
            offs = [o for o in range(base, base + CONV_WIDTH) if o % 8 == r]
            if not offs:
                continue
            shifted = ext[r:r + 8 * (max(offs) // 8) + ts, c0:c0 + LANES]
            for o in offs:
                tap = dw_ref[o - base, :, c0:c0 + LANES]
                acc = acc + shifted[o - r:o - r + ts, :].reshape(ts // 8, 8, LANES) * tap[None]
        acc_scr[:, c0:c0 + LANES] = acc.reshape(ts, LANES)
    acc = acc_scr[...] + db_ref[...]
    mu = jnp.mean(acc, axis=-1, keepdims=True)
    cen = acc - mu
    var = jnp.mean(cen * cen, axis=-1, keepdims=True)
    y = cen * lax.rsqrt(var + EPS) * lg_ref[...] + lb_ref[...]
    o_ref[...] = (y * jax.nn.sigmoid(y)).astype(BF16)


def _conv(proj, dw_w, dw_b, ln_g, ln_b, seq, a_col, b_col):
    T = proj.shape[0]
    C = dw_w.shape[-1]
    ts = 256
    per_seq = seq // ts
    rb = ts // HALO
    halo_map = lambda col: (lambda i: (jnp.maximum(i * rb - 1, 0), col))
    return pl.pallas_call(
        functools.partial(_conv_kernel, per_seq),
        grid=(T // ts,),
        in_specs=[
            pl.BlockSpec((ts, C), lambda i: (i, a_col)),
            pl.BlockSpec((ts, C), lambda i: (i, b_col)),
            pl.BlockSpec((HALO, C), halo_map(a_col)),
            pl.BlockSpec((HALO, C), halo_map(b_col)),
            pl.BlockSpec((CONV_WIDTH, 8, C), lambda i: (0, 0, 0)),
            pl.BlockSpec((1, C), lambda i: (0, 0)),
            pl.BlockSpec((1, C), lambda i: (0, 0)),
            pl.BlockSpec((1, C), lambda i: (0, 0)),
        ],
        out_specs=pl.BlockSpec((ts, C), lambda i: (i, 0)),
        out_shape=jax.ShapeDtypeStruct((T, C), BF16),
        scratch_shapes=[pltpu.VMEM((HALO + ts, C), F32), pltpu.VMEM((ts, C), F32)],
        compiler_params=_cparams(("parallel",)),
        name="conv_branch",
    )(proj, proj, proj, proj, dw_w, dw_b, ln_g, ln_b)


def _attn_kernel(nb, q_ref, kc_ref, kp_ref, vc_ref, vp_ref, bias_ref, sink_ref, o_ref):
    first = (pl.program_id(0) % nb) == 0
    kk = jnp.concatenate([kp_ref[...], kc_ref[...]], axis=0).astype(BF16)
    vv = jnp.concatenate([vp_ref[...], vc_ref[...]], axis=0).astype(BF16)
    row = lax.broadcasted_iota(I32, (BLOCK, 2 * BLOCK), 0)
    col = lax.broadcasted_iota(I32, (BLOCK, 2 * BLOCK), 1)
    dist = row + BLOCK - col
    valid = (dist >= 0) & (dist < BLOCK) & (jnp.logical_not(first) | (col >= BLOCK))
    group = N_HEADS // N_KV_HEADS
    for kvh in range(N_KV_HEADS):
        k_h = kk[:, kvh * HEAD_DIM:(kvh + 1) * HEAD_DIM]
        v_h = vv[:, kvh * HEAD_DIM:(kvh + 1) * HEAD_DIM]
        qg = q_ref[:, kvh * group * HEAD_DIM:(kvh + 1) * group * HEAD_DIM].astype(BF16)
        outs = []
        for g in range(group):
            h = kvh * group + g
            q_h = qg[:, g * HEAD_DIM:(g + 1) * HEAD_DIM]
            s = lax.dot_general(q_h, k_h, (((1,), (1,)), ((), ())), preferred_element_type=F32)
            s = s * (HEAD_DIM ** -0.5) + bias_ref[h]
            s = jnp.where(valid, s, -jnp.inf)
            sink = sink_ref[h]
            m = jnp.maximum(jnp.max(s, axis=-1, keepdims=True), sink)
            e = jnp.exp(s - m)
            denom = jnp.sum(e, axis=-1, keepdims=True) + jnp.exp(sink - m)
            p = e / denom
            outs.append(jnp.dot(p.astype(BF16), v_h, preferred_element_type=F32))
        o_ref[:, kvh * group * HEAD_DIM:(kvh + 1) * group * HEAD_DIM] = (
            jnp.concatenate(outs, axis=1).astype(BF16))


def _attn(proj, bias_tab, sinks, seq, q_col, k_col, v_col):
    T = proj.shape[0]
    nb = seq // BLOCK
    attn_dim = N_HEADS * HEAD_DIM
    kv_dim = N_KV_HEADS * HEAD_DIM
    prev = lambda col: (lambda i: (jnp.where(i % nb == 0, i, i - 1), col))
    return pl.pallas_call(
        functools.partial(_attn_kernel, nb),
        grid=(T // BLOCK,),
        in_specs=[
            pl.BlockSpec((BLOCK, attn_dim), lambda i: (i, q_col)),
            pl.BlockSpec((BLOCK, kv_dim), lambda i: (i, k_col)),
            pl.BlockSpec((BLOCK, kv_dim), prev(k_col)),
            pl.BlockSpec((BLOCK, kv_dim), lambda i: (i, v_col)),
            pl.BlockSpec((BLOCK, kv_dim), prev(v_col)),
            pl.BlockSpec((N_HEADS, BLOCK, 2 * BLOCK), lambda i: (0, 0, 0)),
            pl.BlockSpec(memory_space=pltpu.SMEM),
        ],
        out_specs=pl.BlockSpec((BLOCK, attn_dim), lambda i: (i, 0)),
        out_shape=jax.ShapeDtypeStruct((T, attn_dim), BF16),
        compiler_params=_cparams(("parallel",)),
        name="swa_attention",
    )(proj, proj, proj, proj, proj, bias_tab, sinks)


def _mix_kernel(ca_ref, at_ref, gc_ref, ga_ref, x_ref, g1_ref, wc_ref, wa_ref, wo_ref, o_ref):
    yc = jnp.dot(ca_ref[...], wc_ref[...], preferred_element_type=F32)
    ya = jnp.dot(at_ref[...], wa_ref[...], preferred_element_type=F32)
    mixed = jax.nn.sigmoid(gc_ref[...]) * yc + jax.nn.sigmoid(ga_ref[...]) * ya
    y = jnp.dot(mixed.astype(BF16), wo_ref[...], preferred_element_type=F32)
    o_ref[...] = x_ref[...] + g1_ref[0] * y


def _mix(cact, attn, proj, x, g1, wc, wa, wo, seq, gc_col, ga_col):
    T, D = x.shape
    C = cact.shape[1]
    A = attn.shape[1]
    tm = 256
    per_seq = seq // tm
    return pl.pallas_call(
        _mix_kernel,
        grid=(T // tm,),
        in_specs=[
            pl.BlockSpec((tm, C), lambda i: (i, 0)),
            pl.BlockSpec((tm, A), lambda i: (i, 0)),
            pl.BlockSpec((tm, D), lambda i: (i, gc_col)),
            pl.BlockSpec((tm, D), lambda i: (i, ga_col)),
            pl.BlockSpec((tm, D), lambda i: (i, 0)),
            pl.BlockSpec((1, 1, D), lambda i: (i // per_seq, 0, 0)),
            _resident((C, D), lambda i: (0, 0)),
            _resident((A, D), lambda i: (0, 0)),
            _resident((D, D), lambda i: (0, 0)),
        ],
        out_specs=pl.BlockSpec((tm, D), lambda i: (i, 0)),
        out_shape=jax.ShapeDtypeStruct((T, D), F32),
        compiler_params=_cparams(("parallel",)),
        name="merge_out_proj",
    )(cact, attn, proj, proj, x, g1, wc, wa, wo)


def _store_packed_slabs(o_ref, xb):
    w = lax.bitcast_convert_type(xb.astype(F32), jnp.uint32)
    for i in range(SLAB_WORDS):
        lo = w[:, (2 * i) * LANES:(2 * i + 1) * LANES]
        hi = w[:, (2 * i + 1) * LANES:(2 * i + 2) * LANES]
        o_ref[:, i, :] = hi | (lo >> 16)


def _pack_kernel(x_ref, o_ref):
    _store_packed_slabs(o_ref, x_ref[0].astype(BF16))


def _pack_table(tabs, layer):
    _, n, d = tabs.shape
    te = 256
    packed = pl.pallas_call(
        _pack_kernel,
        grid=(n // te,),
        in_specs=[pl.BlockSpec((1, te, d), lambda i: (layer, i, 0))],
        out_specs=pl.BlockSpec((te, SLAB_WORDS, LANES), lambda i: (i, 0, 0)),
        out_shape=jax.ShapeDtypeStruct((n, SLAB_WORDS, LANES), jnp.uint32),
        compiler_params=_cparams(("parallel",)),
        name="pack_expert_table",
    )(tabs)
    return packed.reshape(n * SLAB_WORDS, LANES)


def _peer_q_kernel(x_ref, g_ref, sc_ref, sh_ref, w_ref, k_ref, h_ref, st_ref):
    hb = _norm_mod(x_ref[...], g_ref[...], sc_ref[0], sh_ref[0]).astype(BF16)
    _store_packed_slabs(h_ref, hb)
    qb = jnp.dot(hb, w_ref[...], preferred_element_type=F32).astype(BF16)
    for hp in range(2 * PEER_HEADS):
        q_hp = qb[:, hp * PEER_HALF:(hp + 1) * PEER_HALF]
        st_ref[hp * N_KEYS:(hp + 1) * N_KEYS, :] = lax.dot_general(
            k_ref[hp], q_hp, (((1,), (1,)), ((), ())), preferred_element_type=F32)


def _peer_q(x, g, sc, sh, w_pq, keys, seq):
    T, D = x.shape
    Q = w_pq.shape[1]
    tm = 256
    per_seq = seq // tm
    n_rows = 2 * PEER_HEADS * N_KEYS
    return pl.pallas_call(
        _peer_q_kernel,
        grid=(T // tm,),
        in_specs=[
            pl.BlockSpec((tm, D), lambda i: (i, 0)),
            pl.BlockSpec((1, D), lambda i: (0, 0)),
            pl.BlockSpec((1, 1, D), lambda i: (i // per_seq, 0, 0)),
            pl.BlockSpec((1, 1, D), lambda i: (i // per_seq, 0, 0)),
            _resident((D, Q), lambda i: (0, 0)),
            _resident((2 * PEER_HEADS, N_KEYS, PEER_HALF), lambda i: (0, 0, 0)),
        ],
        out_specs=[
            pl.BlockSpec((tm, SLAB_WORDS, LANES), lambda i: (i, 0, 0)),
            pl.BlockSpec((n_rows, tm), lambda i: (0, i)),
        ],
        out_shape=[
            jax.ShapeDtypeStruct((T, SLAB_WORDS, LANES), jnp.uint32),
            jax.ShapeDtypeStruct((n_rows, T), F32),
        ],
        compiler_params=_cparams(("parallel",)),
        name="peer_query_scores",
    )(x, g, sc, sh, w_pq, keys)


def _top_rows(x, payload, k, rank=None):
    rows = (lax.broadcasted_iota(I32, x.shape, 0) if rank is None else rank).astype(F32)
    vals, pays = [], []
    for _ in range(k):
        m = jnp.max(x, axis=0, keepdims=True)
        first = jnp.min(jnp.where(x == m, rows, jnp.float32(1 << 24)), axis=0, keepdims=True)
        onehot = rows == first
        if payload is None:
            pays.append(first.astype(I32))
        else:
            pays.append(jnp.sum(jnp.where(onehot, payload, 0), axis=0, keepdims=True))
        vals.append(m)
        x = jnp.where(onehot, -jnp.inf, x)
    return jnp.concatenate(vals, axis=0), jnp.concatenate(pays, axis=0)


def _sort_rows(keys, pays):
    n_groups = len(keys)
    n_rows = 8 * n_groups
    sub = lax.broadcasted_iota(I32, keys[0].shape, 0)
    k = 2
    while k <= n_rows:
        j = k // 2
        while j >= 1:
            if j >= 8:
                gj = j // 8
                for lo in range(n_groups):
                    if lo & gj:
                        continue
                    hi = lo | gj
                    ascending = ((8 * lo) & k) == 0
                    a, b = keys[lo], keys[hi]
                    swap = (a > b) if ascending else (a < b)
                    keys[lo], keys[hi] = jnp.where(swap, b, a), jnp.where(swap, a, b)
                    pa, pb = pays[lo], pays[hi]
                    pays[lo], pays[hi] = jnp.where(swap, pb, pa), jnp.where(swap, pa, pb)
            else:
                lower = (sub & j) == 0
                upper = (sub & j) != 0
                for g in range(n_groups):
                    if k >= 8:
                        ascending = ((8 * g) & k) == 0
                        want_max = upper if ascending else lower
                    else:
                        want_max = jnp.logical_xor(upper, (sub & k) != 0)
                    x, p = keys[g], pays[g]
                    xp = jnp.where(lower, pltpu.roll(x, 8 - j, 0), pltpu.roll(x, j, 0))
                    pp = jnp.where(lower, pltpu.roll(p, 8 - j, 0), pltpu.roll(p, j, 0))
                    swap = jnp.logical_xor(xp < x, want_max)
                    keys[g] = jnp.where(swap, xp, x)
                    pays[g] = jnp.where(swap, pp, p)
            j //= 2
        k *= 2
    return keys, pays


def _route_kernel(block_rows, st_ref, idxl_ref, gate_ref, ovf_ref, idx_scr, gate_scr):
    def head(h, carry):
        base = pl.multiple_of(h * (2 * N_KEYS), 2 * N_KEYS)
        v0, i0 = _top_rows(st_ref[pl.ds(base, N_KEYS), :], None, PEER_TOPK)
        v1, i1 = _top_rows(st_ref[pl.ds(base + N_KEYS, N_KEYS), :], None, PEER_TOPK)
        half = PEER_TOPK // 2
        sub = lax.broadcasted_iota(I32, (half, v0.shape[1]), 0)
        cand = [v0[k:k + 1] + v1[:half] for k in range(half)]
        cidx = [i0[k:k + 1] * N_KEYS + i1[:half] for k in range(half)]
        flat = [sub + k * PEER_TOPK for k in range(half)]
        cand += [v0[:1] + v1[half:], v0[half:] + v1[:1]]
        cidx += [i0[:1] * N_KEYS + i1[half:], i0[half:] * N_KEYS + i1[:1]]
        flat += [sub + half, (sub + half) * PEER_TOPK]
        best, eid = _top_rows(jnp.concatenate(cand, axis=0), jnp.concatenate(cidx, axis=0), PEER_TOPK,
                              rank=jnp.concatenate(flat, axis=0))
        e = jnp.exp(best - jnp.max(best, axis=0, keepdims=True))
        gates = e / jnp.sum(e, axis=0, keepdims=True)
        off = pl.multiple_of(h * PEER_TOPK, PEER_TOPK)
        idx_scr[pl.ds(off, PEER_TOPK), :] = eid
        gate_scr[pl.ds(off, PEER_TOPK), :] = gates
        return carry

    lax.fori_loop(0, PEER_HEADS, head, 0, unroll=HEAD_UNROLL)

    n_groups = PAIRS // 8
    sub = lax.broadcasted_iota(I32, (8, idx_scr.shape[1]), 0)
    keys = [idx_scr[8 * g:8 * g + 8, :] * PAIRS + (sub + 8 * g) for g in range(n_groups)]
    pays = [gate_scr[8 * g:8 * g + 8, :] for g in range(n_groups)]
    keys, pays = _sort_rows(keys, pays)
    eids = [jnp.right_shift(k, PAIRS.bit_length() - 1) for k in keys]
    row_hi, row_lo = WINDOW_SLOTS, PAIRS - WINDOW_SLOTS - 1
    ovf0 = jnp.where(eids[row_hi // 8][row_hi % 8:row_hi % 8 + 1, :] < block_rows, 1, 0)
    ovf1 = jnp.where(eids[row_lo // 8][row_lo % 8:row_lo % 8 + 1, :] >= block_rows, 1, 0)
    flags = jnp.concatenate([ovf0, ovf1], axis=0).astype(I32)
    tile_any = jnp.broadcast_to(jnp.max(flags, axis=1, keepdims=True), flags.shape)
    ovf_ref[...] = jnp.concatenate([flags, tile_any], axis=0)
    rot = (PAIRS - WINDOW_SLOTS) // 8
    for b in range(N_TABLE_BLOCKS):
        order = [(g + b * rot) % n_groups for g in range(n_groups)]
        e = jnp.concatenate([eids[g] for g in order], axis=0)
        inb = (e >= b * block_rows) & (e < (b + 1) * block_rows)
        local = jnp.where(inb, (e - b * block_rows) * SLAB_WORDS, 0)
        gate = jnp.where(inb, jnp.concatenate([pays[g] for g in order], axis=0), 0.0)
        idxl_ref[b] = local.astype(F32).T.astype(I32)
        gate_ref[b] = gate.T


def _route(st, block_rows):
    n_rows, T = st.shape
    tr = EXPERT_TILE
    return pl.pallas_call(
        functools.partial(_route_kernel, block_rows),
        grid=(T // tr,),
        in_specs=[pl.BlockSpec((n_rows, tr), lambda i: (0, i))],
        out_specs=[
            pl.BlockSpec((N_TABLE_BLOCKS, tr, PAIRS), lambda i: (0, i, 0)),
            pl.BlockSpec((N_TABLE_BLOCKS, tr, PAIRS), lambda i: (0, i, 0)),
            pl.BlockSpec((2 * N_TABLE_BLOCKS, tr), lambda i: (0, i)),
        ],
        out_shape=[
            jax.ShapeDtypeStruct((N_TABLE_BLOCKS, T, PAIRS), I32),
            jax.ShapeDtypeStruct((N_TABLE_BLOCKS, T, PAIRS), F32),
            jax.ShapeDtypeStruct((2 * N_TABLE_BLOCKS, T), I32),
        ],
        scratch_shapes=[pltpu.VMEM((PAIRS, tr), I32), pltpu.VMEM((PAIRS, tr), F32)],
        compiler_params=_cparams(("parallel",)),
        name="peer_route",
    )(st)


def _diag_mask():
    r = lax.broadcasted_iota(I32, (SLAB_ROWS, PAIRS * SLAB_ROWS), 0)
    c = lax.broadcasted_iota(I32, (SLAB_ROWS, PAIRS * SLAB_ROWS), 1)
    return r == (c % SLAB_ROWS)


def _for_overflow_tokens(ovf_ref, b, tm, body):
    group = 8
    assert tm == EXPERT_TILE

    def step(g, carry):
        flags = [ovf_ref[b, g * group + i] for i in range(group)]
        any_set = functools.reduce(lambda x, y: x | y, flags)

        @pl.when(any_set != 0)
        def _():
            def one(i, c):
                t = g * group + i

                @pl.when(ovf_ref[b, t] != 0)
                def _():
                    body(t)
                return c

            lax.fori_loop(0, group, one, 0)
        return carry

    @pl.when(ovf_ref[N_TABLE_BLOCKS + b, 0] != 0)
    def _():
        lax.fori_loop(0, tm // group, step, 0)


def _gather_slabs(tab_ref, idx_ref, t, slots):
    words = [tab_ref[pl.ds(pl.multiple_of(idx_ref[0, t, j], SLAB_WORDS), SLAB_WORDS), :] for j in slots]
    return pltpu.bitcast(jnp.concatenate(words, axis=0), BF16)


def _peer_u_kernel(idx_ref, ovf_ref, h_ref, sel_ref, u_ref, o_ref, part_scr):
    tm = h_ref.shape[0]
    b = pl.program_id(0)
    sel = sel_ref[...]
    per_tile = sel.shape[0]
    hi_mask = jnp.uint32(0xFFFF0000)

    def halves(words):
        return (lax.bitcast_convert_type(words << 16, F32), lax.bitcast_convert_type(words & hi_mask, F32))

    def lane_partials(t, slots):
        h = pltpu.bitcast(h_ref[t], BF16)
        sums = []
        for j in slots:
            words = u_ref[pl.ds(pl.multiple_of(idx_ref[0, t, j], SLAB_WORDS), SLAB_WORDS), :]
            p_even, p_odd = halves(pltpu.bitcast(pltpu.bitcast(words, BF16) * h, jnp.uint32))
            sums.append(p_even + p_odd)
        for k in range(0, len(sums), per_tile):
            grp = sums[k:k + per_tile]
            part_scr[t, slots[k]:slots[k] + len(grp), :] = jnp.dot(
                sel[:len(grp), :SLAB_WORDS * len(grp)], jnp.concatenate(grp, axis=0).astype(BF16),
                preferred_element_type=F32)

    def gather_trip(c):
        for i in range(TOKEN_UNROLL):
            lane_partials(c * TOKEN_UNROLL + i, range(WINDOW_SLOTS))

    def lane_sum_trip(c):
        for k in range(0, TOKEN_UNROLL, 8):
            r = pl.multiple_of(c * TOKEN_UNROLL + k, 8)
            o_ref[0, pl.ds(r, 8), 0:WINDOW_SLOTS] = jnp.sum(part_scr[pl.ds(r, 8), 0:WINDOW_SLOTS, :], axis=-1)

    def rare(t):
        lane_partials(t, range(WINDOW_SLOTS, PAIRS))
        o_ref[0, pl.ds(t, 1), WINDOW_SLOTS:] = jnp.sum(part_scr[pl.ds(t, 1), WINDOW_SLOTS:, :], axis=-1)

    def trip(c, carry):
        gather_trip(c)
        lane_sum_trip(c - 1)
        return carry

    n_trips = tm // TOKEN_UNROLL
    o_ref[0, :, WINDOW_SLOTS:] = jnp.zeros((tm, PAIRS - WINDOW_SLOTS), F32)
    gather_trip(0)
    lax.fori_loop(1, n_trips, trip, 0)
    lane_sum_trip(n_trips - 1)
    _for_overflow_tokens(ovf_ref, b, tm, rare)


def _peer_u(idxl, ovf, h_slab, row_sum, u_tab):
    T = h_slab.shape[0]
    block_words = u_tab.shape[0] // N_TABLE_BLOCKS
    tm = EXPERT_TILE
    return pl.pallas_call(
        _peer_u_kernel,
        grid=(N_TABLE_BLOCKS, T // tm),
        in_specs=[
            pl.BlockSpec((1, tm, PAIRS), lambda b, i: (b, i, 0), memory_space=pltpu.SMEM,
                         pipeline_mode=pl.Buffered(1)),
            pl.BlockSpec((2 * N_TABLE_BLOCKS, tm), lambda b, i: (0, i), memory_space=pltpu.SMEM),
            pl.BlockSpec((tm, SLAB_WORDS, LANES), lambda b, i: (i, 0, 0)),
            _resident(row_sum.shape, lambda b, i: (0, 0)),
            _resident((block_words, LANES), lambda b, i: (b, 0)),
        ],
        out_specs=pl.BlockSpec((1, tm, PAIRS), lambda b, i: (b, i, 0)),
        out_shape=jax.ShapeDtypeStruct((N_TABLE_BLOCKS, T, PAIRS), F32),
        scratch_shapes=[pltpu.VMEM((tm, PAIRS, LANES), F32)],
        compiler_params=_cparams(("arbitrary", "arbitrary")),
        name="peer_expert_u",
    )(idxl, ovf, h_slab, row_sum, u_tab)


def _peer_v_kernel(idx_ref, ovf_ref, gate_ref, act_ref, ex_ref, v_ref, o_ref, wexp):
    tm = gate_ref.shape[1]
    b = pl.program_id(0)
    act = act_ref[0]
    gelu = 0.5 * act * (1.0 + lax.erf(act * np.float32(math.sqrt(0.5))))
    w = gate_ref[0] * gelu
    wexp[...] = jnp.dot(w.astype(BF16), ex_ref[...], preferred_element_type=F32)
    eye = _diag_mask()
    wcols = WINDOW_SLOTS * SLAB_ROWS

    def weighted(t, slots, lo, hi):
        vsel = _gather_slabs(v_ref, idx_ref, t, slots)
        wrow = jnp.broadcast_to(wexp[pl.ds(t, 1), lo:hi], (SLAB_ROWS, hi - lo))
        wbd = jnp.where(eye[:, lo:hi], wrow, 0.0).astype(BF16)
        return jnp.dot(wbd, vsel, preferred_element_type=F32)

    def tok(t, carry):
        o_ref[0, t] = weighted(t, range(WINDOW_SLOTS), 0, wcols)
        return carry

    def rare(t):
        o_ref[0, t] = o_ref[0, t] + weighted(t, range(WINDOW_SLOTS, PAIRS), wcols, PAIRS * SLAB_ROWS)

    lax.fori_loop(0, tm, tok, 0, unroll=TOKEN_UNROLL)
    _for_overflow_tokens(ovf_ref, b, tm, rare)


def _peer_v(idxl, ovf, gate, act, expand, v_tab):
    T = gate.shape[1]
    block_words = v_tab.shape[0] // N_TABLE_BLOCKS
    tm = EXPERT_TILE
    return pl.pallas_call(
        _peer_v_kernel,
        grid=(N_TABLE_BLOCKS, T // tm),
        in_specs=[
            pl.BlockSpec((1, tm, PAIRS), lambda b, i: (b, i, 0), memory_space=pltpu.SMEM,
                         pipeline_mode=pl.Buffered(1)),
            pl.BlockSpec((2 * N_TABLE_BLOCKS, tm), lambda b, i: (0, i), memory_space=pltpu.SMEM),
            pl.BlockSpec((1, tm, PAIRS), lambda b, i: (b, i, 0)),
            pl.BlockSpec((1, tm, PAIRS), lambda b, i: (b, i, 0)),
            _resident((PAIRS, PAIRS * SLAB_ROWS), lambda b, i: (0, 0)),
            _resident((block_words, LANES), lambda b, i: (b, 0)),
        ],
        out_specs=pl.BlockSpec((1, tm, SLAB_ROWS, LANES), lambda b, i: (b, i, 0, 0)),
        out_shape=jax.ShapeDtypeStruct((N_TABLE_BLOCKS, T, SLAB_ROWS, LANES), F32),
        scratch_shapes=[pltpu.VMEM((tm, PAIRS * SLAB_ROWS), F32)],
        compiler_params=_cparams(("arbitrary", "arbitrary")),
        name="peer_expert_v",
    )(idxl, ovf, gate, act, expand, v_tab)


def _resid_kernel(final, x_ref, o_ref, g2_ref, fg_ref, y_ref):
    cols = []
    for s in range(SLAB_ROWS):
        part = o_ref[0, :, s, :]
        for k in range(1, N_TABLE_BLOCKS):
            part = part + o_ref[k, :, s, :]
        cols.append(part)
    x = x_ref[...] + g2_ref[0] * jnp.concatenate(cols, axis=1)
    if final:
        ms = jnp.mean(x * x, axis=-1, keepdims=True)
        x = (x * lax.rsqrt(ms + EPS)) * fg_ref[...]
    y_ref[...] = x


def _resid(x, peer_out, g2, final_g, seq, final):
    T, D = x.shape
    tm = 512
    per_seq = seq // tm
    return pl.pallas_call(
        functools.partial(_resid_kernel, final),
        grid=(T // tm,),
        in_specs=[
            pl.BlockSpec((tm, D), lambda i: (i, 0)),
            pl.BlockSpec((N_TABLE_BLOCKS, tm, SLAB_ROWS, LANES), lambda i: (0, i, 0, 0)),
            pl.BlockSpec((1, 1, D), lambda i: (i // per_seq, 0, 0)),
            pl.BlockSpec((1, D), lambda i: (0, 0)),
        ],
        out_specs=pl.BlockSpec((tm, D), lambda i: (i, 0)),
        out_shape=jax.ShapeDtypeStruct((T, D), F32),
        compiler_params=_cparams(("parallel",)),
        name="peer_residual",
    )(x, peer_out, g2, final_g)


def _pair_expand_matrix():
    p = np.arange(PAIRS)[:, None]
    c = np.arange(PAIRS * SLAB_ROWS)[None, :]
    return (p == c // SLAB_ROWS).astype(np.float32)


def kernel(x, c, rel_bias, ada_w, ada_b, norm1_g, w_in, dw_w, dw_b, conv_ln_g, conv_ln_b, w_conv_out,
           attn_sinks, w_attn_out, w_out, norm2_g, w_pq, sub_keys, peer_u, peer_v, final_g):
    B, S, D = x.shape
    L = ada_w.shape[0]
    T = B * S
    C = dw_w.shape[2]
    A = N_HEADS * HEAD_DIM
    KV = N_KV_HEADS * HEAD_DIM
    E = peer_u.shape[1]
    assert S % 1024 == 0 and D == SLAB_ROWS * LANES and E % N_TABLE_BLOCKS == 0

    o_a, o_b, o_q, o_k, o_v, o_g = 0, C, 2 * C, 2 * C + A, 2 * C + A + KV, 2 * C + A + 2 * KV
    order = np.concatenate([np.arange(o_a, o_a + 2 * C), np.arange(o_g, o_g + 2 * D),
                            np.arange(o_q, o_q + A + 2 * KV)])
    n_a, n_b, n_gc, n_ga, n_q, n_k, n_v = 0, C, 2 * C, 2 * C + D, 2 * C + 2 * D, 2 * C + 2 * D + A, 2 * C + 2 * D + A + KV

    c_pad = jnp.pad(c, ((0, 8 - B), (0, 0)))
    mod = _ada(c_pad, ada_w, ada_b)
    bias_tab = _rel_bias_table(rel_bias)
    expand = jnp.asarray(_pair_expand_matrix(), dtype=BF16)
    row_sum = jnp.asarray(np.arange(MXU_DEPTH)[None, :] // SLAB_WORDS == np.arange(MXU_DEPTH // SLAB_WORDS)[:, None],
                          dtype=BF16)
    final_g2 = final_g.reshape(1, D)

    xt = x.reshape(T, D)
    for l in range(L):
        m = mod[l, :B].reshape(B, 6, 1, D)
        sh1, sc1, g1, sh2, sc2, g2 = (m[:, k] for k in range(6))

        w_in_l = w_in[l][:, order].astype(BF16)
        proj = _inproj(xt, norm1_g[l].reshape(1, D), sc1, sh1, w_in_l, S)
        dw_rep = jnp.broadcast_to(dw_w[l][:, None, :], (CONV_WIDTH, 8, C))
        cact = _conv(proj, dw_rep, dw_b[l].reshape(1, C), conv_ln_g[l].reshape(1, C),
                     conv_ln_b[l].reshape(1, C), S, n_a // C, n_b // C)
        attn = _attn(proj, bias_tab, attn_sinks[l], S, n_q // A, n_k // KV, n_v // KV)
        xt = _mix(cact, attn, proj, xt, g1, w_conv_out[l].astype(BF16), w_attn_out[l].astype(BF16),
                  w_out[l].astype(BF16), S, n_gc // D, n_ga // D)

        keys = sub_keys[l].reshape(2 * PEER_HEADS, N_KEYS, PEER_HALF).astype(BF16)
        h2, st = _peer_q(xt, norm2_g[l].reshape(1, D), sc2, sh2, w_pq[l].astype(BF16), keys, S)
        idxl, gate, ovf = _route(st, E // N_TABLE_BLOCKS)
        u_tab = _pack_table(peer_u, l)
        v_tab = _pack_table(peer_v, l)
        act = _peer_u(idxl, ovf, h2, row_sum, u_tab)
        pout = _peer_v(idxl, ovf, gate, act, expand, v_tab)
        xt = _resid(xt, pout, g2, final_g2, S, l == L - 1)
    return xt.reshape(B, S, D)
```
